```python
import jax, jax.numpy as jnp
from jax import lax
import numpy as np

D_MODEL = 2048
BATCH = 8
SEQ = 2048
DEPTH = 1

MEM_LEN = 256
MIX_WIDTH = D_MODEL
A_WIDTH = MIX_WIDTH // 2
A_HEAD_DIM = 128
A_HEADS = A_WIDTH // A_HEAD_DIM
CONV_K = 5
CHUNK = 64
B_WIDTH = MIX_WIDTH // 4
B_HEAD_DIM = 64
B_Q_HEADS = B_WIDTH // B_HEAD_DIM
B_KV_HEADS = B_Q_HEADS // 4
WINDOW = 128
C_WIDTH = MIX_WIDTH // 4
C_HEADS = 4
C_HEAD_DIM = C_WIDTH // C_HEADS

ROPE_THETA = 10000.0
EPS = 1e-6
IN_SIZES = (A_WIDTH, A_WIDTH, A_WIDTH, A_WIDTH, A_HEADS, A_HEADS, A_HEADS, A_HEADS,
            B_WIDTH, B_KV_HEADS * B_HEAD_DIM, B_KV_HEADS * B_HEAD_DIM, B_WIDTH,
            C_WIDTH, C_WIDTH)
IN_WIDTH = sum(IN_SIZES)

kernel_name = "hybrid_deltanet_swa_sink_memxattn_layer"

F32 = jnp.float32


def rms_norm(t, w):
    tf = t.astype(F32)
    y = tf * lax.rsqrt(jnp.mean(tf * tf, axis=-1, keepdims=True) + EPS)
    return (y * w.astype(F32)).astype(t.dtype)


def l2_norm(t):
    tf = t.astype(F32)
    return tf * lax.rsqrt(jnp.sum(tf * tf, axis=-1, keepdims=True) + EPS)


def rope(t):
    seq, d = t.shape[1], t.shape[-1]
    inv = ROPE_THETA ** (-jnp.arange(0, d, 2, dtype=F32) / d)
    ang = jnp.arange(seq, dtype=F32)[:, None] * inv[None, :]
    cos = jnp.cos(ang)[None, :, None, :]
    sin = jnp.sin(ang)[None, :, None, :]
    tf = t.astype(F32)
    t1, t2 = tf[..., : d // 2], tf[..., d // 2:]
    return jnp.concatenate([t1 * cos - t2 * sin, t2 * cos + t1 * sin], axis=-1).astype(t.dtype)


def centred_depthwise_conv(t, w):
    c = t.shape[-1]
    pad = CONV_K // 2
    return lax.conv_general_dilated(
        t, w.astype(t.dtype)[:, None, :], window_strides=(1,), padding=[(pad, pad)],
        dimension_numbers=("NWC", "WIO", "NWC"), feature_group_count=c)


def gated_delta_rule_chunked(q, k, v, g, beta):
    bsz, seq, nh, dk = q.shape
    dv = v.shape[-1]
    n = seq // CHUNK

    def to_chunks(t):
        t = t.astype(F32).reshape((bsz, n, CHUNK, nh) + t.shape[3:])
        return jnp.moveaxis(t, 3, 1)

    q = to_chunks(q) * (dk ** -0.5)
    k, v, g, beta = to_chunks(k), to_chunks(v), to_chunks(g), to_chunks(beta)
    gc = jnp.cumsum(g, axis=-1)
    idx = jnp.arange(CHUNK)
    incl = idx[:, None] >= idx[None, :]
    strict = idx[:, None] > idx[None, :]
    diff = gc[..., :, None] - gc[..., None, :]
    decay = jnp.where(incl, jnp.exp(jnp.where(incl, diff, 0.0)), 0.0)
    kb = k * beta[..., None]
    vb = v * beta[..., None]
    a = -jnp.where(strict, jnp.einsum('bhncd,bhnkd->bhnck', kb, k) * decay, 0.0)

    def fwd_sub(i, tm):
        row = tm[..., i, :]
        return tm.at[..., i, :].set(row + jnp.einsum('...j,...jk->...k', row, tm))

    tinv = lax.fori_loop(1, CHUNK, fwd_sub, a) + jnp.eye(CHUNK, dtype=F32)
    u = jnp.einsum('bhnck,bhnkd->bhncd', tinv, vb)
    w = jnp.einsum('bhnck,bhnkd->bhncd', tinv, kb * jnp.exp(gc)[..., None])
    qk = jnp.einsum('bhncd,bhnkd->bhnck', q, k) * decay
    q_dec = q * jnp.exp(gc)[..., None]
    k_dec = k * jnp.exp(gc[..., -1:] - gc)[..., None]
    chunk_decay = jnp.exp(gc[..., -1])
    xs = tuple(jnp.moveaxis(t_, 2, 0) for t_ in (u, w, qk, q_dec, k_dec, chunk_decay))

    def step(state, xs_i):
        u_i, w_i, qk_i, qd_i, kd_i, cd_i = xs_i
        v_new = u_i - jnp.einsum('bhcd,bhde->bhce', w_i, state)
        o = (jnp.einsum('bhcd,bhde->bhce', qd_i, state)
             + jnp.einsum('bhck,bhke->bhce', qk_i, v_new))
        state = state * cd_i[..., None, None] + jnp.einsum('bhcd,bhce->bhde', kd_i, v_new)
        return state, o

    s0 = jnp.zeros((bsz, nh, dk, dv), F32)
    _, o = lax.scan(step, s0, xs)
    return jnp.transpose(o, (1, 0, 3, 2, 4)).reshape(bsz, seq, nh, dv)


def windowed_gqa_with_sink(q, k, v, sink):
    bsz, seq, hq, d = q.shape
    hkv = k.shape[2]
    grp = hq // hkv
    nb = seq // WINDOW
    qb = q.reshape(bsz, nb, WINDOW, hkv, grp, d)

    def band(t):
        tp = jnp.pad(t, ((0, 0), (WINDOW, WINDOW), (0, 0), (0, 0)))
        tp = tp.reshape(bsz, nb + 2, WINDOW, hkv, d)
        return jnp.concatenate([tp[:, :-2], tp[:, 1:-1], tp[:, 2:]], axis=2)

    kw, vw = band(k), band(v)
    s = jnp.einsum('bnqhgd,bnkhd->bnhgqk', qb, kw).astype(F32) * (d ** -0.5)
    i = jnp.arange(WINDOW)
    j = jnp.arange(3 * WINDOW)
    blk = jnp.arange(nb)
    rel = j[None, :] - WINDOW - i[:, None]
    kpos = blk[:, None] * WINDOW - WINDOW + j[None, :]
    mask = (jnp.abs(rel) <= WINDOW)[None] & ((kpos >= 0) & (kpos < seq))[:, None, :]
    mask = mask[None, :, None, None]
    s = jnp.where(mask, s, -jnp.inf)
    sink_l = sink.astype(F32).reshape(1, 1, hkv, grp, 1, 1)
    m = jnp.maximum(jnp.max(s, axis=-1, keepdims=True), sink_l)
    p = jnp.exp(s - m)
    p = p / (jnp.sum(p, axis=-1, keepdims=True) + jnp.exp(sink_l - m))
    o = jnp.einsum('bnhgqk,bnkhd->bnqhgd', p.astype(v.dtype), vw)
    return o.reshape(bsz, seq, hq, d)


def memory_cross_attention(q, k, v):
    d = q.shape[-1]
    s = jnp.einsum('bshd,bmhd->bhsm', q, k).astype(F32) * (d ** -0.5)
    p = jax.nn.softmax(s, axis=-1)
    return jnp.einsum('bhsm,bmhd->bshd', p.astype(v.dtype), v)


def hybrid_layer(h, mem, norm_w, w_in, conv_w_a, a_log_fwd, a_log_bwd, dt_bias_fwd,
                 dt_bias_bwd, o_norm_a, q_norm_b, k_norm_b, sink_b, mem_norm_w, w_mem_kv,
                 q_norm_c, k_norm_c, w_out):
    bsz, seq, _ = h.shape
    hn = rms_norm(h, norm_w)
    proj = hn @ w_in.astype(hn.dtype)
    (qa, ka, va, za, alpha_f, alpha_b, beta_f, beta_b,
     qb, kb, vb, zb, qc, zc) = jnp.split(proj, np.cumsum(IN_SIZES)[:-1].tolist(), axis=-1)

    qkv = jax.nn.silu(centred_depthwise_conv(jnp.concatenate([qa, ka, va], axis=-1), conv_w_a))
    qa, ka, va = jnp.split(qkv, 3, axis=-1)
    qa = l2_norm(qa.reshape(bsz, seq, A_HEADS, A_HEAD_DIM))
    ka = l2_norm(ka.reshape(bsz, seq, A_HEADS, A_HEAD_DIM))
    va = va.reshape(bsz, seq, A_HEADS, A_HEAD_DIM).astype(F32)
    g_f = -jnp.exp(a_log_fwd.astype(F32)) * jax.nn.softplus(alpha_f.astype(F32) + dt_bias_fwd.astype(F32))
    g_b = -jnp.exp(a_log_bwd.astype(F32)) * jax.nn.softplus(alpha_b.astype(F32) + dt_bias_bwd.astype(F32))
    bt_f = jax.nn.sigmoid(beta_f.astype(F32))
    bt_b = jax.nn.sigmoid(beta_b.astype(F32))
    flip = lambda t: jnp.flip(t, axis=1)
    o_fwd = gated_delta_rule_chunked(qa, ka, va, g_f, bt_f)
    o_bwd = flip(gated_delta_rule_chunked(flip(qa), flip(ka), flip(va), flip(g_b), flip(bt_b)))
    o_a = rms_norm((o_fwd + o_bwd).astype(h.dtype), o_norm_a).reshape(bsz, seq, A_WIDTH)
    o_a = o_a * jax.nn.silu(za)

    qb = rope(rms_norm(qb.reshape(bsz, seq, B_Q_HEADS, B_HEAD_DIM), q_norm_b))
    kb = rope(rms_norm(kb.reshape(bsz, seq, B_KV_HEADS, B_HEAD_DIM), k_norm_b))
    vb = vb.reshape(bsz, seq, B_KV_HEADS, B_HEAD_DIM)
    o_b = windowed_gqa_with_sink(qb, kb, vb, sink_b).reshape(bsz, seq, B_WIDTH)
    o_b = o_b * jax.nn.silu(zb)

    mn = rms_norm(mem, mem_norm_w)
    kv_m = mn @ w_mem_kv.astype(mn.dtype)
    km, vm = jnp.split(kv_m, 2, axis=-1)
    mlen = mem.shape[1]
    km = rms_norm(km.reshape(bsz, mlen, C_HEADS, C_HEAD_DIM), k_norm_c)
    vm = vm.reshape(bsz, mlen, C_HEADS, C_HEAD_DIM)
    qc = rms_norm(qc.reshape(bsz, seq, C_HEADS, C_HEAD_DIM), q_norm_c)
    o_c = memory_cross_attention(qc, km, vm).reshape(bsz, seq, C_WIDTH)
    o_c = o_c * jax.nn.silu(zc)

    mixed = jnp.concatenate([o_a.astype(h.dtype), o_b.astype(h.dtype), o_c.astype(h.dtype)], axis=-1)
    return h + mixed @ w_out.astype(h.dtype)


def setup_inputs(seed: int = 0) -> dict:
    key = jax.random.key(seed)
    ks = jax.random.split(key, 20)
    nrm = lambda k_, shape: jax.random.normal(k_, shape, F32)
    x = nrm(ks[0], (BATCH, SEQ, D_MODEL))
    mem = nrm(ks[1], (BATCH, MEM_LEN, D_MODEL))
    norm_w = 1.0 + 0.02 * nrm(ks[2], (DEPTH, D_MODEL))
    w_in = nrm(ks[3], (DEPTH, D_MODEL, IN_WIDTH)) * D_MODEL ** -0.5
    conv_w_a = nrm(ks[4], (DEPTH, CONV_K, 3 * A_WIDTH)) * CONV_K ** -0.5
    a_log_fwd = jnp.log(jax.random.uniform(ks[5], (DEPTH, A_HEADS), F32, 1.0, 16.0))
    a_log_bwd = jnp.log(jax.random.uniform(ks[6], (DEPTH, A_HEADS), F32, 1.0, 16.0))
    dt_f = jnp.exp(jax.random.uniform(ks[7], (DEPTH, A_HEADS), F32, jnp.log(1e-3), jnp.log(1e-1)))
    dt_b = jnp.exp(jax.random.uniform(ks[8], (DEPTH, A_HEADS), F32, jnp.log(1e-3), jnp.log(1e-1)))
    dt_bias_fwd = dt_f + jnp.log(-jnp.expm1(-dt_f))
    dt_bias_bwd = dt_b + jnp.log(-jnp.expm1(-dt_b))
    o_norm_a = 1.0 + 0.02 * nrm(ks[9], (DEPTH, A_HEAD_DIM))
    q_norm_b = 1.0 + 0.02 * nrm(ks[10], (DEPTH, B_HEAD_DIM))
    k_norm_b = 1.0 + 0.02 * nrm(ks[11], (DEPTH, B_HEAD_DIM))
    sink_b = 0.5 * nrm(ks[12], (DEPTH, B_Q_HEADS))
    mem_norm_w = 1.0 + 0.02 * nrm(ks[13], (DEPTH, D_MODEL))
    w_mem_kv = nrm(ks[14], (DEPTH, D_MODEL, 2 * C_WIDTH)) * D_MODEL ** -0.5
    q_norm_c = 1.0 + 0.02 * nrm(ks[15], (DEPTH, C_HEAD_DIM))
    k_norm_c = 1.0 + 0.02 * nrm(ks[16], (DEPTH, C_HEAD_DIM))
    w_out = nrm(ks[17], (DEPTH, MIX_WIDTH, D_MODEL)) * MIX_WIDTH ** -0.5
    return {"x": x, "mem": mem, "norm_w": norm_w, "w_in": w_in, "conv_w_a": conv_w_a,
            "a_log_fwd": a_log_fwd, "a_log_bwd": a_log_bwd, "dt_bias_fwd": dt_bias_fwd,
            "dt_bias_bwd": dt_bias_bwd, "o_norm_a": o_norm_a, "q_norm_b": q_norm_b,
            "k_norm_b": k_norm_b, "sink_b": sink_b, "mem_norm_w": mem_norm_w,
            "w_mem_kv": w_mem_kv, "q_norm_c": q_norm_c, "k_norm_c": k_norm_c, "w_out": w_out}


def reference(x, mem, norm_w, w_in, conv_w_a, a_log_fwd, a_log_bwd, dt_bias_fwd, dt_bias_bwd,
              o_norm_a, q_norm_b, k_norm_b, sink_b, mem_norm_w, w_mem_kv, q_norm_c, k_norm_c,
              w_out):
    h = x
    for l in range(DEPTH):
        h = hybrid_layer(h, mem, norm_w[l], w_in[l], conv_w_a[l], a_log_fwd[l], a_log_bwd[l],
                         dt_bias_fwd[l], dt_bias_bwd[l], o_norm_a[l], q_norm_b[l], k_norm_b[l],
                         sink_b[l], mem_norm_w[l], w_mem_kv[l], q_norm_c[l], k_norm_c[l], w_out[l])
    return h
```

```python
import functools

import numpy as np
import jax
import jax.numpy as jnp
from jax import lax
from jax.experimental import pallas as pl
from jax.experimental.pallas import tpu as pltpu

F32 = jnp.float32
BF16 = jnp.bfloat16

D_MODEL = 2048
A_WIDTH = 1024
A_HEAD_DIM = 128
A_HEADS = 8
CONV_K = 5
CHUNK = 64
B_WIDTH = 512
B_HEAD_DIM = 64
B_Q_HEADS = 8
B_KV_HEADS = 2
WINDOW = 128
C_WIDTH = 512
C_HEADS = 4
C_HEAD_DIM = 128
ROPE_THETA = 10000.0
EPS = 1e-6

LANES = 128
GATE_COLS = 4 * A_HEADS
C_QKV = 0
C_ZA = 3 * A_WIDTH
C_AB = C_ZA + A_WIDTH
C_QB = C_AB + LANES
C_KB = C_QB + B_WIDTH
C_VB = C_KB + B_KV_HEADS * B_HEAD_DIM
C_ZB = C_VB + B_KV_HEADS * B_HEAD_DIM
C_QC = C_ZB + B_WIDTH
C_ZC = C_QC + C_WIDTH
IN_PAD = C_ZC + C_WIDTH

VMEM_LIMIT = 56 * 1024 * 1024

_NT = (((1,), (1,)), ((), ()))
_TN = (((0,), (0,)), ((), ()))


def _dot(a, b):
    return jnp.dot(a, b, preferred_element_type=F32)


def _dot_nt(a, b):
    return lax.dot_general(a, b, _NT, preferred_element_type=F32)


def _silu(z):
    return z * jax.nn.sigmoid(z)


def _softplus(z):
    return jnp.maximum(z, 0.0) + jnp.log1p(jnp.exp(-jnp.abs(z)))


def _rope_partner(x):
    lane = lax.broadcasted_iota(jnp.int32, x.shape, 1)
    first_half = (lane % B_HEAD_DIM) < (B_HEAD_DIM // 2)
    up = pltpu.roll(x, LANES - B_HEAD_DIM // 2, axis=1)
    down = pltpu.roll(x, B_HEAD_DIM // 2, axis=1)
    return jnp.where(first_half, up, down)


def _head_rms(x, ones_blockdiag, head_dim, w):
    ss = _dot((x * x).astype(BF16), ones_blockdiag)
    return x * lax.rsqrt(ss * (1.0 / head_dim) + EPS) * w


def _in_proj_kernel(x_ref, nw_ref, w_ref, cos_ref, sin_ref, alog_ref, dtb_ref, qnb_ref, knb_ref,
                    qnc_ref, e64_ref, e128_ref,
                    qkv_ref, ga_ref, ab_ref, abt_ref, qb_ref, kb_ref, vb_ref, gb_ref, qc_ref, gc_ref):
    x = x_ref[...]
    ms = jnp.mean(x * x, axis=-1, keepdims=True)
    hn = (x * lax.rsqrt(ms + EPS) * nw_ref[...]).astype(BF16)
    tm = x.shape[0]

    def proj(c0, width):
        return _dot(hn, w_ref[:, c0:c0 + width])

    step = 512
    for c in range(0, 3 * A_WIDTH, step):
        qkv_ref[:, c:c + step] = proj(C_QKV + c, step).astype(BF16)
    for c in range(0, A_WIDTH, step):
        ga_ref[:, c:c + step] = _silu(proj(C_ZA + c, step)).astype(BF16)

    acc = proj(C_AB, LANES)
    lane = lax.broadcasted_iota(jnp.int32, (tm, LANES), 1)
    g = -jnp.exp(alog_ref[...]) * _softplus(acc + dtb_ref[...])
    val = jnp.where(lane < 2 * A_HEADS, g, jax.nn.sigmoid(acc))
    val = jnp.where(lane < GATE_COLS, val, 0.0)
    pos = lax.broadcasted_iota(jnp.int32, (tm, LANES), 0) % CHUNK
    pre = val
    suf = val
    s = 1
    while s < CHUNK:
        pre = pre + jnp.where(pos >= s, pltpu.roll(pre, s, axis=0), 0.0)
        suf = suf + jnp.where(pos < CHUNK - s, pltpu.roll(suf, tm - s, axis=0), 0.0)
        s *= 2
    res = jnp.where(lane < A_HEADS, pre, jnp.where(lane < 2 * A_HEADS, suf, val))
    ab_ref[...] = res
    abt_ref[...] = res.T

    cos = cos_ref[...]
    sin = sin_ref[...]
    qb = _head_rms(proj(C_QB, B_WIDTH), e64_ref[...], B_HEAD_DIM, qnb_ref[...])
    for c in range(0, B_WIDTH, LANES):
        t = qb[:, c:c + LANES]
        qb_ref[:, c:c + LANES] = (t * cos[:, c:c + LANES] + _rope_partner(t) * sin[:, c:c + LANES]).astype(BF16)
    kb = _head_rms(proj(C_KB, LANES), e64_ref[:LANES, :LANES], B_HEAD_DIM, knb_ref[...])
    kb_ref[...] = (kb * cos[:, :LANES] + _rope_partner(kb) * sin[:, :LANES]).astype(BF16)
    vb_ref[...] = proj(C_VB, LANES).astype(BF16)
    gb_ref[...] = _silu(proj(C_ZB, B_WIDTH)).astype(BF16)
    qc_ref[...] = _head_rms(proj(C_QC, C_WIDTH), e128_ref[...], C_HEAD_DIM, qnc_ref[...]).astype(BF16)
    gc_ref[...] = _silu(proj(C_ZC, C_WIDTH)).astype(BF16)


def _block_diag_ones(width, block):
    idx = np.arange(width) // block
    return jnp.asarray(idx[:, None] == idx[None, :], dtype=BF16)


def _rope_tables(seq):
    d = B_HEAD_DIM
    inv = ROPE_THETA ** (-jnp.arange(0, d, 2, dtype=F32) / d)
    ang = jnp.arange(seq, dtype=F32)[:, None] * inv[None, :]
    cos = jnp.cos(ang)
    sin = jnp.sin(ang)
    cos_h = jnp.concatenate([cos, cos], axis=-1)
    sin_h = jnp.concatenate([-sin, sin], axis=-1)
    return jnp.tile(cos_h, (1, B_Q_HEADS)), jnp.tile(sin_h, (1, B_Q_HEADS))


def _in_proj(x2, norm_w, w_pad, cos, sin, alog_row, dtb_row, qnb, knb, qnc, seq, tm=256):
    t = x2.shape[0]
    tiles_per_seq = seq // tm
    row = lambda i: (i, 0)
    const = lambda i: (0, 0)
    pos = lambda i: (i % tiles_per_seq, 0)
    in_specs = [
        pl.BlockSpec((tm, D_MODEL), row),
        pl.BlockSpec((1, D_MODEL), const),
        pl.BlockSpec((D_MODEL, IN_PAD), const, pipeline_mode=pl.Buffered(1)),
        pl.BlockSpec((tm, B_WIDTH), pos),
        pl.BlockSpec((tm, B_WIDTH), pos),
        pl.BlockSpec((1, LANES), const),
        pl.BlockSpec((1, LANES), const),
        pl.BlockSpec((1, B_WIDTH), const),
        pl.BlockSpec((1, LANES), const),
        pl.BlockSpec((1, C_WIDTH), const),
        pl.BlockSpec((B_WIDTH, B_WIDTH), const),
        pl.BlockSpec((C_WIDTH, C_WIDTH), const),
    ]
    out_shape = [
        jax.ShapeDtypeStruct((t, 3 * A_WIDTH), BF16),
        jax.ShapeDtypeStruct((t, A_WIDTH), BF16),
        jax.ShapeDtypeStruct((t, LANES), F32),
        jax.ShapeDtypeStruct((LANES, t), F32),
        jax.ShapeDtypeStruct((t, B_WIDTH), BF16),
        jax.ShapeDtypeStruct((t, LANES), BF16),
        jax.ShapeDtypeStruct((t, LANES), BF16),
        jax.ShapeDtypeStruct((t, B_WIDTH), BF16),
        jax.ShapeDtypeStruct((t, C_WIDTH), BF16),
        jax.ShapeDtypeStruct((t, C_WIDTH), BF16),
    ]
    out_specs = [
        pl.BlockSpec((tm, 3 * A_WIDTH), row),
        pl.BlockSpec((tm, A_WIDTH), row),
        pl.BlockSpec((tm, LANES), row),
        pl.BlockSpec((LANES, tm), lambda i: (0, i)),
        pl.BlockSpec((tm, B_WIDTH), row),
        pl.BlockSpec((tm, LANES), row),
        pl.BlockSpec((tm, LANES), row),
        pl.BlockSpec((tm, B_WIDTH), row),
        pl.BlockSpec((tm, C_WIDTH), row),
        pl.BlockSpec((tm, C_WIDTH), row),
    ]
    return pl.pallas_call(
        _in_proj_kernel,
        grid=(t // tm,),
        in_specs=in_specs,
        out_specs=out_specs,
        out_shape=out_shape,
        compiler_params=pltpu.CompilerParams(dimension_semantics=("arbitrary",), vmem_limit_bytes=VMEM_LIMIT),
        name="in_proj",
    )(x2, norm_w, w_pad, cos, sin, alog_row, dtb_row, qnb, knb, qnc,
      _block_diag_ones(B_WIDTH, B_HEAD_DIM), _block_diag_ones(C_WIDTH, C_HEAD_DIM))


def _mem_proj_kernel(m_ref, nw_ref, w_ref, knc_ref, e128_ref, km_ref, vm_ref):
    x = m_ref[...]
    ms = jnp.mean(x * x, axis=-1, keepdims=True)
    mn = (x * lax.rsqrt(ms + EPS) * nw_ref[...]).astype(BF16)
    km = _dot(mn, w_ref[:, :C_WIDTH])
    km_ref[...] = _head_rms(km, e128_ref[...], C_HEAD_DIM, knc_ref[...]).astype(BF16)
    vm_ref[...] = _dot(mn, w_ref[:, C_WIDTH:]).astype(BF16)


def _mem_proj(mem2, mem_norm_w, w_kv, knc, tm=256):
    t = mem2.shape[0]
    row = lambda i: (i, 0)
    const = lambda i: (0, 0)
    return pl.pallas_call(
        _mem_proj_kernel,
        grid=(t // tm,),
        in_specs=[
            pl.BlockSpec((tm, D_MODEL), row),
            pl.BlockSpec((1, D_MODEL), const),
            pl.BlockSpec((D_MODEL, 2 * C_WIDTH), const),
            pl.BlockSpec((1, C_WIDTH), const),
            pl.BlockSpec((C_WIDTH, C_WIDTH), const),
        ],
        out_specs=[pl.BlockSpec((tm, C_WIDTH), row), pl.BlockSpec((tm, C_WIDTH), row)],
        out_shape=[jax.ShapeDtypeStruct((t, C_WIDTH), BF16), jax.ShapeDtypeStruct((t, C_WIDTH), BF16)],
        compiler_params=pltpu.CompilerParams(dimension_semantics=("arbitrary",), vmem_limit_bytes=VMEM_LIMIT),
        name="mem_proj",
    )(mem2, mem_norm_w, w_kv, knc, _block_diag_ones(C_WIDTH, C_HEAD_DIM))


PAD_ROWS = 8
PREP_GROUP = 4


def _unit_tri_inverse(n_mat):
    ri = lax.broadcasted_iota(jnp.int32, n_mat.shape, 0)
    ci = lax.broadcasted_iota(jnp.int32, n_mat.shape, 1)
    t = jnp.where(ri == ci, 1.0, 0.0) + n_mat
    p = n_mat
    k = 2
    while k < CHUNK:
        pb = p.astype(BF16)
        p = _dot(pb, pb)
        t = t + _dot(t.astype(BF16), p.astype(BF16))
        k *= 2
    return t


def _deltanet_kernel(q_ref, k_ref, v_ref, wq_ref, wk_ref, wv_ref, ab_ref, gfr_ref, gbr_ref, ga_ref, on_ref,
                     out_ref,
                     pad_s, qs, ks, vs, gcf, gcb, btf, btb, u_f, u_b, wqd_f, wqd_b, qkk_f, qkk_b,
                     cd_f, cd_b, o_f, o_b):
    seq = q_ref.shape[0]
    nc = seq // CHUNK
    h = pl.program_id(1)

    zeros_pad = jnp.zeros((PAD_ROWS, LANES), F32)
    pad_s[0:PAD_ROWS, :] = zeros_pad
    pad_s[PAD_ROWS + seq:PAD_ROWS + seq + PAD_ROWS, :] = zeros_pad
    slab = 256

    def conv_silu(src_ref, w_ref, dst_ref, l2, scale):
        for r in range(0, seq, slab):
            pad_s[PAD_ROWS + r:PAD_ROWS + r + slab, :] = src_ref[r:r + slab, :].astype(F32)
        w = w_ref[...]
        for r in range(0, seq, slab):
            acc = jnp.zeros((slab, LANES), F32)
            for j in range(CONV_K):
                o = PAD_ROWS + r + j - CONV_K // 2
                acc = acc + pad_s[o:o + slab, :] * w[j:j + 1, :]
            y = _silu(acc)
            if l2:
                y = y * lax.rsqrt(jnp.sum(y * y, axis=-1, keepdims=True) + EPS) * scale
            dst_ref[r:r + slab, :] = y

    conv_silu(q_ref, wq_ref, qs, True, A_HEAD_DIM ** -0.5)
    conv_silu(k_ref, wk_ref, ks, True, 1.0)
    conv_silu(v_ref, wv_ref, vs, False, 1.0)

    lane = lax.broadcasted_iota(jnp.int32, (slab, LANES), 1)
    for r in range(0, seq, slab):
        ab = ab_ref[r:r + slab, :]
        for dst, col in ((gcf, h), (gcb, A_HEADS + h), (btf, 2 * A_HEADS + h), (btb, 3 * A_HEADS + h)):
            c = jnp.sum(jnp.where(lane == col, ab, 0.0), axis=-1, keepdims=True)
            dst[r:r + slab, :] = jnp.broadcast_to(c, (slab, LANES))

    ri = lax.broadcasted_iota(jnp.int32, (CHUNK, CHUNK), 0)
    ci = lax.broadcasted_iota(jnp.int32, (CHUNK, CHUNK), 1)

    def prep_chunk(n):
        r0 = pl.multiple_of(n * CHUNK, CHUNK)
        rows = pl.ds(r0, CHUNK)
        kn = ks[rows, :]
        qn = qs[rows, :]
        vn = vs[rows, :]
        kq = jnp.concatenate([kn, qn], axis=0).astype(BF16)
        prod = _dot_nt(kq, kn.astype(BF16))
        kk = prod[:CHUNK]
        qk = prod[CHUNK:]
        for fwd, gc_s, bt_s, gr_ref, u_s, wqd_s, qkk_s, cd_s in (
                (True, gcf, btf, gfr_ref, u_f, wqd_f, qkk_f, cd_f),
                (False, gcb, btb, gbr_ref, u_b, wqd_b, qkk_b, cd_b)):
            gcc = gc_s[rows, :]
            bt = bt_s[rows, :]
            gcr = gr_ref[0, pl.ds(n, 1), :]
            incl = (ri >= ci) if fwd else (ri <= ci)
            strict = (ri > ci) if fwd else (ri < ci)
            decay = jnp.where(incl, jnp.exp(jnp.where(incl, gcc[:, :CHUNK] - gcr, 0.0)), 0.0)
            n_mat = jnp.where(strict, -(kk * decay * bt[:, :CHUNK]), 0.0)
            tinv = _unit_tri_inverse(n_mat)
            eg = jnp.exp(gcc)
            rhs = jnp.concatenate([vn * bt, kn * bt * eg], axis=1).astype(BF16)
            uw = _dot(tinv.astype(BF16), rhs)
            gl = gcc[CHUNK - 1:CHUNK, :] if fwd else gcc[0:1, :]
            kd = kn * jnp.exp(gl - gcc)
            u_s[rows, :] = uw[:, :LANES]
            wqd_s[n, 0:CHUNK, :] = uw[:, LANES:].astype(BF16)
            wqd_s[n, CHUNK:2 * CHUNK, :] = (qn * eg).astype(BF16)
            qkk_s[n, 0:CHUNK, :] = jnp.where(incl, qk * decay, 0.0).astype(BF16)
            qkk_s[n, CHUNK:CHUNK + LANES, :] = kd.T.astype(BF16)
            cd_s[n] = jnp.broadcast_to(jnp.exp(gl), (8, LANES))

    def prep_body(g, carry):
        for j in range(PREP_GROUP):
            prep_chunk(g * PREP_GROUP + j)
        return carry

    lax.fori_loop(0, nc // PREP_GROUP, prep_body, 0)

    def step(n, state, u_s, wqd_s, qkk_s, cd_s, o_s):
        rows = pl.ds(pl.multiple_of(n * CHUNK, CHUNK), CHUNK)
        ws = _dot(wqd_s[n], state.astype(BF16))
        v_new = (u_s[rows, :] - ws[:CHUNK]).astype(BF16)
        r = _dot(qkk_s[n], v_new)
        o_s[rows, :] = ws[CHUNK:] + r[:CHUNK]
        return state * cd_s[n][0:1, :] + r[CHUNK:]

    def seq_body(i, carry):
        sf, sb = carry
        sf = step(i, sf, u_f, wqd_f, qkk_f, cd_f, o_f)
        sb = step(nc - 1 - i, sb, u_b, wqd_b, qkk_b, cd_b, o_b)
        return sf, sb

    s0 = jnp.zeros((A_HEAD_DIM, A_HEAD_DIM), F32)
    lax.fori_loop(0, nc, seq_body, (s0, s0))

    for r in range(0, seq, slab):
        o = o_f[r:r + slab, :] + o_b[r:r + slab, :]
        y = o * lax.rsqrt(jnp.mean(o * o, axis=-1, keepdims=True) + EPS) * on_ref[...]
        out_ref[r:r + slab, :] = (y * ga_ref[r:r + slab, :].astype(F32)).astype(BF16)


def _deltanet(qkv, conv_w, ab, abt3, ga, o_norm, batch, seq):
    nc = seq // CHUNK
    t = batch * seq
    hh = A_HEADS
    seq_f32 = pltpu.VMEM((seq, LANES), F32)
    scratch = [pltpu.VMEM((seq + 2 * PAD_ROWS, LANES), F32)]
    scratch += [seq_f32] * 3
    scratch += [seq_f32] * 4
    scratch += [seq_f32] * 2
    scratch += [pltpu.VMEM((nc, 2 * CHUNK, LANES), BF16)] * 2
    scratch += [pltpu.VMEM((nc, CHUNK + LANES, CHUNK), BF16)] * 2
    scratch += [pltpu.VMEM((nc, 8, LANES), F32)] * 2
    scratch += [seq_f32] * 2
    return pl.pallas_call(
        _deltanet_kernel,
        grid=(batch, hh),
        in_specs=[
            pl.BlockSpec((seq, LANES), lambda b, h: (b, h)),
            pl.BlockSpec((seq, LANES), lambda b, h: (b, hh + h)),
            pl.BlockSpec((seq, LANES), lambda b, h: (b, 2 * hh + h)),
            pl.BlockSpec((CONV_K, LANES), lambda b, h: (0, h)),
            pl.BlockSpec((CONV_K, LANES), lambda b, h: (0, hh + h)),
            pl.BlockSpec((CONV_K, LANES), lambda b, h: (0, 2 * hh + h)),
            pl.BlockSpec((seq, LANES), lambda b, h: (b, 0)),
            pl.BlockSpec((1, nc, CHUNK), lambda b, h: (h, b, 0)),
            pl.BlockSpec((1, nc, CHUNK), lambda b, h: (hh + h, b, 0)),
            pl.BlockSpec((seq, LANES), lambda b, h: (b, h)),
            pl.BlockSpec((1, LANES), lambda b, h: (0, 0)),
        ],
        out_specs=pl.BlockSpec((seq, LANES), lambda b, h: (b, h)),
        out_shape=jax.ShapeDtypeStruct((t, A_WIDTH), BF16),
        scratch_shapes=scratch,
        compiler_params=pltpu.CompilerParams(dimension_semantics=("arbitrary", "arbitrary"),
                                             vmem_limit_bytes=VMEM_LIMIT),
        name="deltanet",
    )(qkv, qkv, qkv, conv_w, conv_w, conv_w, ab, abt3, abt3, ga, o_norm)


def _attn_kernel(sink_ref, qb_ref, kb_ref, vb_ref, gb_ref, qc_ref, km_ref, vm_ref, gc_ref, out_ref):
    n = pl.program_id(1)
    nb = pl.num_programs(1)
    w = WINDOW
    lane = lax.broadcasted_iota(jnp.int32, (w, LANES), 1)
    lower = lane < B_HEAD_DIM
    ri = lax.broadcasted_iota(jnp.int32, (2 * w, w), 0) % w
    ci = lax.broadcasted_iota(jnp.int32, (2 * w, w), 1)
    top = lax.broadcasted_iota(jnp.int32, (2 * w, 1), 0) < w

    blocks = []
    for rel in (-1, 0, 1):
        j = n + rel
        start = pl.multiple_of(jnp.clip(j, 0, nb - 1) * w, w)
        kblk = kb_ref[pl.ds(start, w), :].astype(F32)
        vblk = vb_ref[pl.ds(start, w), :].astype(F32)
        valid = jnp.logical_and(j >= 0, j < nb)
        if rel == -1:
            mask = jnp.logical_and(ci >= ri, valid)
        elif rel == 0:
            mask = None
        else:
            mask = jnp.logical_and(ci <= ri, valid)
        blocks.append((kblk, pltpu.roll(kblk, B_HEAD_DIM, axis=1), vblk, pltpu.roll(vblk, B_HEAD_DIM, axis=1), mask))

    scale = B_HEAD_DIM ** -0.5
    for hk in range(B_KV_HEADS):
        c0 = hk * 2 * LANES
        q2 = jnp.concatenate([qb_ref[:, c0:c0 + LANES], qb_ref[:, c0 + LANES:c0 + 2 * LANES]], axis=0)
        acc = jnp.zeros((2 * w, LANES), F32)
        for half in range(2):
            ks_, vs_ = [], []
            for kblk, ksw, vblk, vsw, _ in blocks:
                in_place = (hk == half)
                keep = lower if half == 0 else jnp.logical_not(lower)
                ks_.append(jnp.where(keep, kblk if in_place else ksw, 0.0).astype(BF16))
                vs_.append(jnp.where(keep, vblk if in_place else vsw, 0.0).astype(BF16))
            sink_col = jnp.where(top, sink_ref[4 * hk + half], sink_ref[4 * hk + 2 + half])
            ss = []
            m = sink_col
            for (_, _, _, _, mask), kk in zip(blocks, ks_):
                s = _dot_nt(q2, kk) * scale
                if mask is not None:
                    s = jnp.where(mask, s, -jnp.inf)
                ss.append(s)
                m = jnp.maximum(m, jnp.max(s, axis=-1, keepdims=True))
            den = jnp.exp(sink_col - m)
            pv = jnp.zeros((2 * w, LANES), F32)
            for s, vv in zip(ss, vs_):
                p = jnp.exp(s - m)
                den = den + jnp.sum(p, axis=-1, keepdims=True)
                pv = pv + _dot(p.astype(BF16), vv)
            acc = acc + pv / den
        out_ref[:, c0:c0 + LANES] = (acc[:w] * gb_ref[:, c0:c0 + LANES].astype(F32)).astype(BF16)
        out_ref[:, c0 + LANES:c0 + 2 * LANES] = (acc[w:] * gb_ref[:, c0 + LANES:c0 + 2 * LANES].astype(F32)).astype(BF16)

    cscale = C_HEAD_DIM ** -0.5
    for hc in range(C_HEADS):
        c0 = hc * C_HEAD_DIM
        s = _dot_nt(qc_ref[:, c0:c0 + C_HEAD_DIM], km_ref[:, c0:c0 + C_HEAD_DIM]) * cscale
        m = jnp.max(s, axis=-1, keepdims=True)
        p = jnp.exp(s - m)
        den = jnp.sum(p, axis=-1, keepdims=True)
        o = _dot(p.astype(BF16), vm_ref[:, c0:c0 + C_HEAD_DIM]) / den
        out_ref[:, B_WIDTH + c0:B_WIDTH + c0 + C_HEAD_DIM] = (
            o * gc_ref[:, c0:c0 + C_HEAD_DIM].astype(F32)).astype(BF16)


def _attention(sink, qb, kb, vb, gb, qc, km, vm, gcg, batch, seq):
    nb = seq // WINDOW
    t = batch * seq
    mlen = km.shape[0] // batch
    qrow = lambda b, n: (b * nb + n, 0)
    per_b = lambda b, n: (b, 0)
    return pl.pallas_call(
        _attn_kernel,
        grid=(batch, nb),
        in_specs=[
            pl.BlockSpec(memory_space=pltpu.SMEM),
            pl.BlockSpec((WINDOW, B_WIDTH), qrow),
            pl.BlockSpec((seq, LANES), per_b),
            pl.BlockSpec((seq, LANES), per_b),
            pl.BlockSpec((WINDOW, B_WIDTH), qrow),
            pl.BlockSpec((WINDOW, C_WIDTH), qrow),
            pl.BlockSpec((mlen, C_WIDTH), per_b),
            pl.BlockSpec((mlen, C_WIDTH), per_b),
            pl.BlockSpec((WINDOW, C_WIDTH), qrow),
        ],
        out_specs=pl.BlockSpec((WINDOW, B_WIDTH + C_WIDTH), qrow),
        out_shape=jax.ShapeDtypeStruct((t, B_WIDTH + C_WIDTH), BF16),
        compiler_params=pltpu.CompilerParams(dimension_semantics=("arbitrary", "arbitrary"),
                                             vmem_limit_bytes=VMEM_LIMIT),
        name="attention",
    )(sink, qb, kb, vb, gb, qc, km, vm, gcg)


def _out_proj_kernel(x_ref, ma_ref, mbc_ref, wa_ref, wbc_ref, out_ref):
    out_ref[...] = x_ref[...] + _dot(ma_ref[...], wa_ref[...]) + _dot(mbc_ref[...], wbc_ref[...])


def _out_proj(x2, mixed_a, mixed_bc, w_a, w_bc, tm=256):
    t = x2.shape[0]
    row = lambda i: (i, 0)
    const = lambda i: (0, 0)
    return pl.pallas_call(
        _out_proj_kernel,
        grid=(t // tm,),
        in_specs=[
            pl.BlockSpec((tm, D_MODEL), row),
            pl.BlockSpec((tm, A_WIDTH), row),
            pl.BlockSpec((tm, B_WIDTH + C_WIDTH), row),
            pl.BlockSpec((A_WIDTH, D_MODEL), const),
            pl.BlockSpec((B_WIDTH + C_WIDTH, D_MODEL), const),
        ],
        out_specs=pl.BlockSpec((tm, D_MODEL), row),
        out_shape=jax.ShapeDtypeStruct((t, D_MODEL), F32),
        compiler_params=pltpu.CompilerParams(dimension_semantics=("arbitrary",), vmem_limit_bytes=VMEM_LIMIT),
        name="out_proj",
    )(x2, mixed_a, mixed_bc, w_a, w_bc)


def _layer(h, mem, norm_w, w_in, conv_w_a, a_log_fwd, a_log_bwd, dt_bias_fwd, dt_bias_bwd, o_norm_a,
           q_norm_b, k_norm_b, sink_b, mem_norm_w, w_mem_kv, q_norm_c, k_norm_c, w_out):
    batch, seq, _ = h.shape
    t = batch * seq
    x2 = h.reshape(t, D_MODEL)

    gate_w = jnp.pad(w_in[:, C_AB:C_AB + GATE_COLS], ((0, 0), (0, LANES - GATE_COLS)))
    w_pad = jnp.concatenate([w_in[:, :C_AB], gate_w, w_in[:, C_AB + GATE_COLS:]], axis=1).astype(BF16)
    zeros = jnp.zeros((LANES - 2 * A_HEADS,), F32)
    alog_row = jnp.concatenate([a_log_fwd, a_log_bwd, zeros]).reshape(1, LANES)
    dtb_row = jnp.concatenate([dt_bias_fwd, dt_bias_bwd, zeros]).reshape(1, LANES)
    cos, sin = _rope_tables(seq)

    qkv, ga, ab, abt, qb, kb, vb, gb, qc, gcg = _in_proj(
        x2, norm_w.reshape(1, D_MODEL), w_pad, cos, sin, alog_row, dtb_row,
        jnp.tile(q_norm_b, B_Q_HEADS).reshape(1, B_WIDTH),
        jnp.tile(k_norm_b, B_KV_HEADS).reshape(1, LANES),
        jnp.tile(q_norm_c, C_HEADS).reshape(1, C_WIDTH), seq)

    mlen = mem.shape[1]
    km, vm = _mem_proj(mem.reshape(batch * mlen, D_MODEL), mem_norm_w.reshape(1, D_MODEL),
                       w_mem_kv.astype(BF16), jnp.tile(k_norm_c, C_HEADS).reshape(1, C_WIDTH))

    mixed_a = _deltanet(qkv, conv_w_a, ab, abt.reshape(LANES, t // CHUNK, CHUNK), ga,
                        o_norm_a.reshape(1, A_HEAD_DIM), batch, seq)
    mixed_bc = _attention(sink_b, qb, kb, vb, gb, qc, km, vm, gcg, batch, seq)

    w_out_b = w_out.astype(BF16)
    out = _out_proj(x2, mixed_a, mixed_bc, w_out_b[:A_WIDTH], w_out_b[A_WIDTH:])
    return out.reshape(batch, seq, D_MODEL)


def kernel(x, mem, norm_w, w_in, conv_w_a, a_log_fwd, a_log_bwd, dt_bias_fwd, dt_bias_bwd, o_norm_a,
           q_norm_b, k_norm_b, sink_b, mem_norm_w, w_mem_kv, q_norm_c, k_norm_c, w_out):
    h = x
    for l in range(norm_w.shape[0]):
        h = _layer(h, mem, norm_w[l], w_in[l], conv_w_a[l], a_log_fwd[l], a_log_bwd[l], dt_bias_fwd[l],
                   dt_bias_bwd[l], o_norm_a[l], q_norm_b[l], k_norm_b[l], sink_b[l], mem_norm_w[l],
                   w_mem_kv[l], q_norm_c[l], k_norm_c[l], w_out[l])
    return h
```

```python
import functools

import numpy as np
import jax
import jax.numpy as jnp
from jax import lax
from jax.experimental import pallas as pl
from jax.experimental.pallas import tpu as pltpu

F32 = jnp.float32
BF16 = jnp.bfloat16

D_MODEL = 2048
A_WIDTH = 1024
A_HEAD_DIM = 128
A_HEADS = 8
CONV_K = 5
CHUNK = 64
B_WIDTH = 512
B_HEAD_DIM = 64
B_Q_HEADS = 8
B_KV_HEADS = 2
WINDOW = 128
C_WIDTH = 512
C_HEADS = 4
C_HEAD_DIM = 128
ROPE_THETA = 10000.0
EPS = 1e-6

LANES = 128
GATE_COLS = 4 * A_HEADS
C_QKV = 0
C_ZA = 3 * A_WIDTH
C_AB = C_ZA + A_WIDTH
C_QB = C_AB + LANES
C_KB = C_QB + B_WIDTH
C_VB = C_KB + B_KV_HEADS * B_HEAD_DIM
C_ZB = C_VB + B_KV_HEADS * B_HEAD_DIM
C_QC = C_ZB + B_WIDTH
C_ZC = C_QC + C_WIDTH
IN_PAD = C_ZC + C_WIDTH

VMEM_LIMIT = 56 * 1024 * 1024

_NT = (((1,), (1,)), ((), ()))
_TN = (((0,), (0,)), ((), ()))


def _dot(a, b):
    return jnp.dot(a, b, preferred_element_type=F32)


def _dot_nt(a, b):
    return lax.dot_general(a, b, _NT, preferred_element_type=F32)


def _silu(z):
    return z * jax.nn.sigmoid(z)


def _softplus(z):
    return jnp.maximum(z, 0.0) + jnp.log1p(jnp.exp(-jnp.abs(z)))


def _rope_partner(x):
    lane = lax.broadcasted_iota(jnp.int32, x.shape, 1)
    first_half = (lane % B_HEAD_DIM) < (B_HEAD_DIM // 2)
    up = pltpu.roll(x, LANES - B_HEAD_DIM // 2, axis=1)
    down = pltpu.roll(x, B_HEAD_DIM // 2, axis=1)
    return jnp.where(first_half, up, down)


def _head_rms(x, ones_blockdiag, head_dim, w):
    ss = _dot((x * x).astype(BF16), ones_blockdiag)
    return x * lax.rsqrt(ss * (1.0 / head_dim) + EPS) * w


def _in_proj_kernel(x_ref, nw_ref, w_ref, cos_ref, sin_ref, alog_ref, dtb_ref, qnb_ref, knb_ref,
                    qnc_ref, e64_ref, e128_ref,
                    qkv_ref, ga_ref, ab_ref, abt_ref, qb_ref, kb_ref, vb_ref, gb_ref, qc_ref, gc_ref):
    x = x_ref[...]
    ms = jnp.mean(x * x, axis=-1, keepdims=True)
    hn = (x * lax.rsqrt(ms + EPS) * nw_ref[...]).astype(BF16)
    tm = x.shape[0]

    def proj(c0, width):
        return _dot(hn, w_ref[:, c0:c0 + width])

    step = 512
    for c in range(0, 3 * A_WIDTH, step):
        qkv_ref[:, c:c + step] = proj(C_QKV + c, step).astype(BF16)
    for c in range(0, A_WIDTH, step):
        ga_ref[:, c:c + step] = _silu(proj(C_ZA + c, step)).astype(BF16)

    acc = proj(C_AB, LANES)
    lane = lax.broadcasted_iota(jnp.int32, (tm, LANES), 1)
    g = -jnp.exp(alog_ref[...]) * _softplus(acc + dtb_ref[...])
    val = jnp.where(lane < 2 * A_HEADS, g, jax.nn.sigmoid(acc))
    val = jnp.where(lane < GATE_COLS, val, 0.0)
    pos = lax.broadcasted_iota(jnp.int32, (tm, LANES), 0) % CHUNK
    pre = val
    suf = val
    s = 1
    while s < CHUNK:
        pre = pre + jnp.where(pos >= s, pltpu.roll(pre, s, axis=0), 0.0)
        suf = suf + jnp.where(pos < CHUNK - s, pltpu.roll(suf, tm - s, axis=0), 0.0)
        s *= 2
    res = jnp.where(lane < A_HEADS, pre, jnp.where(lane < 2 * A_HEADS, suf, val))
    ab_ref[...] = res
    abt_ref[0] = res.T

    cos = cos_ref[...]
    sin = sin_ref[...]
    qb = _head_rms(proj(C_QB, B_WIDTH), e64_ref[...], B_HEAD_DIM, qnb_ref[...])
    for c in range(0, B_WIDTH, LANES):
        t = qb[:, c:c + LANES]
        qb_ref[:, c:c + LANES] = (t * cos[:, c:c + LANES] + _rope_partner(t) * sin[:, c:c + LANES]).astype(BF16)
    kb = _head_rms(proj(C_KB, LANES), e64_ref[:LANES, :LANES], B_HEAD_DIM, knb_ref[...])
    kb_ref[...] = (kb * cos[:, :LANES] + _rope_partner(kb) * sin[:, :LANES]).astype(BF16)
    vb_ref[...] = proj(C_VB, LANES).astype(BF16)
    gb_ref[...] = _silu(proj(C_ZB, B_WIDTH)).astype(BF16)
    qc_ref[...] = _head_rms(proj(C_QC, C_WIDTH), e128_ref[...], C_HEAD_DIM, qnc_ref[...]).astype(BF16)
    gc_ref[...] = _silu(proj(C_ZC, C_WIDTH)).astype(BF16)


def _block_diag_ones(width, block):
    idx = np.arange(width) // block
    return jnp.asarray(idx[:, None] == idx[None, :], dtype=BF16)


def _rope_tables(seq):
    d = B_HEAD_DIM
    inv = ROPE_THETA ** (-jnp.arange(0, d, 2, dtype=F32) / d)
    ang = jnp.arange(seq, dtype=F32)[:, None] * inv[None, :]
    cos = jnp.cos(ang)
    sin = jnp.sin(ang)
    cos_h = jnp.concatenate([cos, cos], axis=-1)
    sin_h = jnp.concatenate([-sin, sin], axis=-1)
    return jnp.tile(cos_h, (1, B_Q_HEADS)), jnp.tile(sin_h, (1, B_Q_HEADS))


def _in_proj(x2, norm_w, w_pad, cos, sin, alog_row, dtb_row, qnb, knb, qnc, seq, tm=256):
    t = x2.shape[0]
    tiles_per_seq = seq // tm
    row = lambda i: (i, 0)
    const = lambda i: (0, 0)
    pos = lambda i: (i % tiles_per_seq, 0)
    in_specs = [
        pl.BlockSpec((tm, D_MODEL), row),
        pl.BlockSpec((1, D_MODEL), const),
        pl.BlockSpec((D_MODEL, IN_PAD), const, pipeline_mode=pl.Buffered(1)),
        pl.BlockSpec((tm, B_WIDTH), pos),
        pl.BlockSpec((tm, B_WIDTH), pos),
        pl.BlockSpec((1, LANES), const),
        pl.BlockSpec((1, LANES), const),
        pl.BlockSpec((1, B_WIDTH), const),
        pl.BlockSpec((1, LANES), const),
        pl.BlockSpec((1, C_WIDTH), const),
        pl.BlockSpec((B_WIDTH, B_WIDTH), const),
        pl.BlockSpec((C_WIDTH, C_WIDTH), const),
    ]
    out_shape = [
        jax.ShapeDtypeStruct((t, 3 * A_WIDTH), BF16),
        jax.ShapeDtypeStruct((t, A_WIDTH), BF16),
        jax.ShapeDtypeStruct((t, LANES), F32),
        jax.ShapeDtypeStruct((t // tm, LANES, tm), F32),
        jax.ShapeDtypeStruct((t, B_WIDTH), BF16),
        jax.ShapeDtypeStruct((t, LANES), BF16),
        jax.ShapeDtypeStruct((t, LANES), BF16),
        jax.ShapeDtypeStruct((t, B_WIDTH), BF16),
        jax.ShapeDtypeStruct((t, C_WIDTH), BF16),
        jax.ShapeDtypeStruct((t, C_WIDTH), BF16),
    ]
    out_specs = [
        pl.BlockSpec((tm, 3 * A_WIDTH), row),
        pl.BlockSpec((tm, A_WIDTH), row),
        pl.BlockSpec((tm, LANES), row),
        pl.BlockSpec((1, LANES, tm), lambda i: (i, 0, 0)),
        pl.BlockSpec((tm, B_WIDTH), row),
        pl.BlockSpec((tm, LANES), row),
        pl.BlockSpec((tm, LANES), row),
        pl.BlockSpec((tm, B_WIDTH), row),
        pl.BlockSpec((tm, C_WIDTH), row),
        pl.BlockSpec((tm, C_WIDTH), row),
    ]
    return pl.pallas_call(
        _in_proj_kernel,
        grid=(t // tm,),
        in_specs=in_specs,
        out_specs=out_specs,
        out_shape=out_shape,
        compiler_params=pltpu.CompilerParams(dimension_semantics=("arbitrary",), vmem_limit_bytes=VMEM_LIMIT),
        name="in_proj",
    )(x2, norm_w, w_pad, cos, sin, alog_row, dtb_row, qnb, knb, qnc,
      _block_diag_ones(B_WIDTH, B_HEAD_DIM), _block_diag_ones(C_WIDTH, C_HEAD_DIM))


def _mem_proj_kernel(m_ref, nw_ref, w_ref, knc_ref, e128_ref, km_ref, vm_ref):
    x = m_ref[...]
    ms = jnp.mean(x * x, axis=-1, keepdims=True)
    mn = (x * lax.rsqrt(ms + EPS) * nw_ref[...]).astype(BF16)
    km = _dot(mn, w_ref[:, :C_WIDTH])
    km_ref[...] = _head_rms(km, e128_ref[...], C_HEAD_DIM, knc_ref[...]).astype(BF16)
    vm_ref[...] = _dot(mn, w_ref[:, C_WIDTH:]).astype(BF16)


def _mem_proj(mem2, mem_norm_w, w_kv, knc, tm=256):
    t = mem2.shape[0]
    row = lambda i: (i, 0)
    const = lambda i: (0, 0)
    return pl.pallas_call(
        _mem_proj_kernel,
        grid=(t // tm,),
        in_specs=[
            pl.BlockSpec((tm, D_MODEL), row),
            pl.BlockSpec((1, D_MODEL), const),
            pl.BlockSpec((D_MODEL, 2 * C_WIDTH), const),
            pl.BlockSpec((1, C_WIDTH), const),
            pl.BlockSpec((C_WIDTH, C_WIDTH), const),
        ],
        out_specs=[pl.BlockSpec((tm, C_WIDTH), row), pl.BlockSpec((tm, C_WIDTH), row)],
        out_shape=[jax.ShapeDtypeStruct((t, C_WIDTH), BF16), jax.ShapeDtypeStruct((t, C_WIDTH), BF16)],
        compiler_params=pltpu.CompilerParams(dimension_semantics=("arbitrary",), vmem_limit_bytes=VMEM_LIMIT),
        name="mem_proj",
    )(mem2, mem_norm_w, w_kv, knc, _block_diag_ones(C_WIDTH, C_HEAD_DIM))


PAD_ROWS = 8
HEADS_PER_STEP = 2
PACK = 4
PACK_ROWS = PACK * CHUNK
SQUARINGS = 5


def _deltanet_kernel(q_ref, k_ref, v_ref, wq_ref, wk_ref, wv_ref, ab_ref, gfr_ref, gbr_ref, ga_ref, on_ref,
                     out_ref,
                     pad_s, qs, ks, vs, gcol, beta, u_s, wqd_s, qkk_s, cd_s, o_s):
    seq = q_ref.shape[0]
    nc = seq // CHUNK
    hb = HEADS_PER_STEP
    h0 = pl.program_id(1) * hb

    zeros_pad = jnp.zeros((PAD_ROWS, LANES), F32)
    pad_s[0:PAD_ROWS, :] = zeros_pad
    pad_s[PAD_ROWS + seq:PAD_ROWS + seq + PAD_ROWS, :] = zeros_pad
    slab = 256

    def slab_rows(i):
        return pl.ds(pl.multiple_of(i * slab, slab), slab)

    def conv_silu(src_ref, w_ref, dst_ref, hs, l2, scale):
        cols = slice(hs * LANES, (hs + 1) * LANES)

        def stage(i, carry):
            pad_s[pl.ds(pl.multiple_of(PAD_ROWS + i * slab, PAD_ROWS), slab), :] = (
                src_ref[slab_rows(i), cols].astype(F32))
            return carry

        lax.fori_loop(0, seq // slab, stage, 0)
        w = w_ref[:, cols]

        def conv(i, carry):
            acc = jnp.zeros((slab, LANES), F32)
            for j in range(CONV_K):
                o = PAD_ROWS + j - CONV_K // 2
                acc = acc + pad_s[pl.ds(i * slab + o, slab), :] * w[j:j + 1, :]
            y = _silu(acc)
            if l2:
                y = y * lax.rsqrt(jnp.sum(y * y, axis=-1, keepdims=True) + EPS) * scale
            dst_ref[hs, slab_rows(i), :] = y
            return carry

        lax.fori_loop(0, seq // slab, conv, 0)

    for hs in range(hb):
        conv_silu(q_ref, wq_ref, qs, hs, True, A_HEAD_DIM ** -0.5)
        conv_silu(k_ref, wk_ref, ks, hs, True, 1.0)
        conv_silu(v_ref, wv_ref, vs, hs, False, 1.0)

    lane = lax.broadcasted_iota(jnp.int32, (slab, LANES), 1)

    def gate_cols(i, carry):
        ab = ab_ref[slab_rows(i), :]
        for hs in range(hb):
            for d in range(2):
                for dst, base in ((gcol, 0), (beta, 2 * A_HEADS)):
                    col = base + d * A_HEADS + h0 + hs
                    c = jnp.sum(jnp.where(lane == col, ab, 0.0), axis=-1, keepdims=True)
                    dst[hs, d, slab_rows(i), :] = jnp.broadcast_to(c, (slab, LANES))
        return carry

    lax.fori_loop(0, seq // slab, gate_cols, 0)

    ii = lax.broadcasted_iota(jnp.int32, (CHUNK, PACK_ROWS), 0)
    jl = lax.broadcasted_iota(jnp.int32, (CHUNK, PACK_ROWS), 1)
    lb = jl // CHUNK
    jj = jl % CHUNK
    block_diag = (lax.broadcasted_iota(jnp.int32, (PACK_ROWS, PACK_ROWS), 0) // CHUNK
                  == lax.broadcasted_iota(jnp.int32, (PACK_ROWS, PACK_ROWS), 1) // CHUNK)

    def pack_diag(full):
        out = full[0:CHUNK]
        for c in range(1, PACK):
            out = jnp.where(lb == c, full[c * CHUNK:(c + 1) * CHUNK], out)
        return out

    def pack_cols(col):
        out = None
        for c in range(PACK):
            blk = col[c * CHUNK:(c + 1) * CHUNK]
            blk = jnp.concatenate([blk, blk], axis=1)
            out = blk if out is None else jnp.where(lb == c, blk, out)
        return out

    def to_block_diag(packed):
        return jnp.where(block_diag, jnp.concatenate([packed] * PACK, axis=0), 0.0).astype(BF16)

    def prep_body(p, carry):
        rows = pl.ds(pl.multiple_of(p * PACK_ROWS, PACK_ROWS), PACK_ROWS)
        members = []
        for hs in range(hb):
            k4 = ks[hs, rows, :]
            q4 = qs[hs, rows, :]
            v4 = vs[hs, rows, :]
            k4b = k4.astype(BF16)
            kk_p = pack_diag(_dot_nt(k4b, k4b))
            qk_p = pack_diag(_dot_nt(q4.astype(BF16), k4b))
            for d in range(2):
                fwd = d == 0
                gc4 = gcol[hs, d, rows, :]
                bt4 = beta[hs, d, rows, :]
                gcr = (gfr_ref if fwd else gbr_ref)[p, pl.ds(h0 + hs, 1), :]
                incl = (ii >= jj) if fwd else (ii <= jj)
                strict = (ii > jj) if fwd else (ii < jj)
                decay = jnp.where(incl, jnp.exp(jnp.where(incl, pack_cols(gc4) - gcr, 0.0)), 0.0)
                n_cat = jnp.where(strict, -(kk_p * decay * pack_cols(bt4)), 0.0)
                qkm = jnp.where(incl, qk_p * decay, 0.0)
                members.append((hs, d, k4, q4, v4, gc4, bt4, n_cat, qkm))

        eye = jnp.where(ii == jj, 1.0, 0.0)
        tinv = [eye + m[7] for m in members]
        pw = []
        for m in members:
            nb = m[7].astype(BF16)
            pw.append(_dot(nb, to_block_diag(m[7])))
        for level in range(1, SQUARINGS + 1):
            last = level == SQUARINGS
            for i in range(len(members)):
                rhs = to_block_diag(pw[i])
                if last:
                    tinv[i] = tinv[i] + _dot(tinv[i].astype(BF16), rhs)
                else:
                    both = _dot(jnp.concatenate([pw[i], tinv[i]], axis=0).astype(BF16), rhs)
                    pw[i] = both[:CHUNK]
                    tinv[i] = tinv[i] + both[CHUNK:]

        for (hs, d, k4, q4, v4, gc4, bt4, _, qkm), t_p in zip(members, tinv):
            fwd = d == 0
            eg = jnp.exp(gc4)
            rhs = jnp.concatenate([v4 * bt4, k4 * bt4 * eg], axis=1).astype(BF16)
            uw = _dot(to_block_diag(t_p), rhs)
            u_s[hs, d, rows, :] = uw[:, :LANES]
            w = uw[:, LANES:].astype(BF16)
            qd = (q4 * eg).astype(BF16)
            for c in range(PACK):
                n = p * PACK + c
                rs = slice(c * CHUNK, (c + 1) * CHUNK)
                last_row = (c + 1) * CHUNK - 1 if fwd else c * CHUNK
                gl = gc4[last_row:last_row + 1, :]
                kd = k4[rs] * jnp.exp(gl - gc4[rs])
                wqd_s[hs, d, n, 0:CHUNK, :] = w[rs]
                wqd_s[hs, d, n, CHUNK:2 * CHUNK, :] = qd[rs]
                qkk_s[hs, d, n, 0:CHUNK, :] = qkm[:, c * CHUNK:(c + 1) * CHUNK].astype(BF16)
                qkk_s[hs, d, n, CHUNK:CHUNK + LANES, :] = kd.T.astype(BF16)
                cd_s[hs, d, n] = jnp.broadcast_to(jnp.exp(gl), (8, LANES))
        return carry

    lax.fori_loop(0, seq // PACK_ROWS, prep_body, 0)

    chains = [(hs, d) for hs in range(hb) for d in range(2)]

    def seq_body(i, states):
        ns = (i, nc - 1 - i)
        rows = [pl.ds(pl.multiple_of(n * CHUNK, CHUNK), CHUNK) for n in ns]
        ws = [_dot(wqd_s[hs, d, ns[d]], st.astype(BF16)) for (hs, d), st in zip(chains, states)]
        v_new = [(u_s[hs, d, rows[d], :] - w_[:CHUNK]).astype(BF16) for (hs, d), w_ in zip(chains, ws)]
        r = [_dot(qkk_s[hs, d, ns[d]], vn) for (hs, d), vn in zip(chains, v_new)]
        new_states = []
        for (hs, d), st, w_, r_ in zip(chains, states, ws, r):
            o_s[hs, d, rows[d], :] = w_[CHUNK:] + r_[:CHUNK]
            new_states.append(st * cd_s[hs, d, ns[d]][0:1, :] + r_[CHUNK:])
        return tuple(new_states)

    s0 = jnp.zeros((A_HEAD_DIM, A_HEAD_DIM), F32)
    lax.fori_loop(0, nc, seq_body, tuple(s0 for _ in chains))

    def finish(i, carry):
        for hs in range(hb):
            cols = slice(hs * LANES, (hs + 1) * LANES)
            o = o_s[hs, 0, slab_rows(i), :] + o_s[hs, 1, slab_rows(i), :]
            y = o * lax.rsqrt(jnp.mean(o * o, axis=-1, keepdims=True) + EPS) * on_ref[...]
            out_ref[slab_rows(i), cols] = (y * ga_ref[slab_rows(i), cols].astype(F32)).astype(BF16)
        return carry

    lax.fori_loop(0, seq // slab, finish, 0)


def _deltanet(qkv, conv_w, ab, abt3, ga, o_norm, batch, seq):
    nc = seq // CHUNK
    npk = seq // PACK_ROWS
    t = batch * seq
    hb = HEADS_PER_STEP
    hw = hb * LANES
    groups = A_HEADS // hb
    scratch = [
        pltpu.VMEM((seq + 2 * PAD_ROWS, LANES), F32),
        pltpu.VMEM((hb, seq, LANES), F32),
        pltpu.VMEM((hb, seq, LANES), F32),
        pltpu.VMEM((hb, seq, LANES), F32),
        pltpu.VMEM((hb, 2, seq, LANES), F32),
        pltpu.VMEM((hb, 2, seq, LANES), F32),
        pltpu.VMEM((hb, 2, seq, LANES), F32),
        pltpu.VMEM((hb, 2, nc, 2 * CHUNK, LANES), BF16),
        pltpu.VMEM((hb, 2, nc, CHUNK + LANES, CHUNK), BF16),
        pltpu.VMEM((hb, 2, nc, 8, LANES), F32),
        pltpu.VMEM((hb, 2, seq, LANES), F32),
    ]
    return pl.pallas_call(
        _deltanet_kernel,
        grid=(batch, groups),
        in_specs=[
            pl.BlockSpec((seq, hw), lambda b, g: (b, g)),
            pl.BlockSpec((seq, hw), lambda b, g: (b, groups + g)),
            pl.BlockSpec((seq, hw), lambda b, g: (b, 2 * groups + g)),
            pl.BlockSpec((CONV_K, hw), lambda b, g: (0, g)),
            pl.BlockSpec((CONV_K, hw), lambda b, g: (0, groups + g)),
            pl.BlockSpec((CONV_K, hw), lambda b, g: (0, 2 * groups + g)),
            pl.BlockSpec((seq, LANES), lambda b, g: (b, 0)),
            pl.BlockSpec((npk, A_HEADS, PACK_ROWS), lambda b, g: (b, 0, 0)),
            pl.BlockSpec((npk, A_HEADS, PACK_ROWS), lambda b, g: (b, 1, 0)),
            pl.BlockSpec((seq, hw), lambda b, g: (b, g)),
            pl.BlockSpec((1, LANES), lambda b, g: (0, 0)),
        ],
        out_specs=pl.BlockSpec((seq, hw), lambda b, g: (b, g)),
        out_shape=jax.ShapeDtypeStruct((t, A_WIDTH), BF16),
        scratch_shapes=scratch,
        compiler_params=pltpu.CompilerParams(dimension_semantics=("arbitrary", "arbitrary"),
                                             vmem_limit_bytes=VMEM_LIMIT),
        name="deltanet",
    )(qkv, qkv, qkv, conv_w, conv_w, conv_w, ab, abt3, abt3, ga, o_norm)


def _attn_kernel(sink_ref, qb_ref, kb_ref, vb_ref, gb_ref, qc_ref, km_ref, vm_ref, gc_ref, out_ref):
    n = pl.program_id(1)
    nb = pl.num_programs(1)
    w = WINDOW
    lane = lax.broadcasted_iota(jnp.int32, (w, LANES), 1)
    lower = lane < B_HEAD_DIM
    ri = lax.broadcasted_iota(jnp.int32, (2 * w, w), 0) % w
    ci = lax.broadcasted_iota(jnp.int32, (2 * w, w), 1)
    top = lax.broadcasted_iota(jnp.int32, (2 * w, 1), 0) < w

    blocks = []
    for rel in (-1, 0, 1):
        j = n + rel
        start = pl.multiple_of(jnp.clip(j, 0, nb - 1) * w, w)
        kblk = kb_ref[pl.ds(start, w), :].astype(F32)
        vblk = vb_ref[pl.ds(start, w), :].astype(F32)
        valid = jnp.logical_and(j >= 0, j < nb)
        if rel == -1:
            mask = jnp.logical_and(ci >= ri, valid)
        elif rel == 0:
            mask = None
        else:
            mask = jnp.logical_and(ci <= ri, valid)
        blocks.append((kblk, pltpu.roll(kblk, B_HEAD_DIM, axis=1), vblk, pltpu.roll(vblk, B_HEAD_DIM, axis=1), mask))

    scale = B_HEAD_DIM ** -0.5
    for hk in range(B_KV_HEADS):
        c0 = hk * 2 * LANES
        q2 = jnp.concatenate([qb_ref[:, c0:c0 + LANES], qb_ref[:, c0 + LANES:c0 + 2 * LANES]], axis=0)
        acc = jnp.zeros((2 * w, LANES), F32)
        for half in range(2):
            ks_, vs_ = [], []
            for kblk, ksw, vblk, vsw, _ in blocks:
                in_place = (hk == half)
                keep = lower if half == 0 else jnp.logical_not(lower)
                ks_.append(jnp.where(keep, kblk if in_place else ksw, 0.0).astype(BF16))
                vs_.append(jnp.where(keep, vblk if in_place else vsw, 0.0).astype(BF16))
            sink_col = jnp.where(top, sink_ref[4 * hk + half], sink_ref[4 * hk + 2 + half])
            ss = []
            m = sink_col
            for (_, _, _, _, mask), kk in zip(blocks, ks_):
                s = _dot_nt(q2, kk) * scale
                if mask is not None:
                    s = jnp.where(mask, s, -jnp.inf)
                ss.append(s)
                m = jnp.maximum(m, jnp.max(s, axis=-1, keepdims=True))
            den = jnp.exp(sink_col - m)
            pv = jnp.zeros((2 * w, LANES), F32)
            for s, vv in zip(ss, vs_):
                p = jnp.exp(s - m)
                den = den + jnp.sum(p, axis=-1, keepdims=True)
                pv = pv + _dot(p.astype(BF16), vv)
            acc = acc + pv / den
        out_ref[:, c0:c0 + LANES] = (acc[:w] * gb_ref[:, c0:c0 + LANES].astype(F32)).astype(BF16)
        out_ref[:, c0 + LANES:c0 + 2 * LANES] = (acc[w:] * gb_ref[:, c0 + LANES:c0 + 2 * LANES].astype(F32)).astype(BF16)

    cscale = C_HEAD_DIM ** -0.5
    for hc in range(C_HEADS):
        c0 = hc * C_HEAD_DIM
        s = _dot_nt(qc_ref[:, c0:c0 + C_HEAD_DIM], km_ref[:, c0:c0 + C_HEAD_DIM]) * cscale
        m = jnp.max(s, axis=-1, keepdims=True)
        p = jnp.exp(s - m)
        den = jnp.sum(p, axis=-1, keepdims=True)
        o = _dot(p.astype(BF16), vm_ref[:, c0:c0 + C_HEAD_DIM]) / den
        out_ref[:, B_WIDTH + c0:B_WIDTH + c0 + C_HEAD_DIM] = (
            o * gc_ref[:, c0:c0 + C_HEAD_DIM].astype(F32)).astype(BF16)


def _attention(sink, qb, kb, vb, gb, qc, km, vm, gcg, batch, seq):
    nb = seq // WINDOW
    t = batch * seq
    mlen = km.shape[0] // batch
    qrow = lambda b, n: (b * nb + n, 0)
    per_b = lambda b, n: (b, 0)
    return pl.pallas_call(
        _attn_kernel,
        grid=(batch, nb),
        in_specs=[
            pl.BlockSpec(memory_space=pltpu.SMEM),
            pl.BlockSpec((WINDOW, B_WIDTH), qrow),
            pl.BlockSpec((seq, LANES), per_b),
            pl.BlockSpec((seq, LANES), per_b),
            pl.BlockSpec((WINDOW, B_WIDTH), qrow),
            pl.BlockSpec((WINDOW, C_WIDTH), qrow),
            pl.BlockSpec((mlen, C_WIDTH), per_b),
            pl.BlockSpec((mlen, C_WIDTH), per_b),
            pl.BlockSpec((WINDOW, C_WIDTH), qrow),
        ],
        out_specs=pl.BlockSpec((WINDOW, B_WIDTH + C_WIDTH), qrow),
        out_shape=jax.ShapeDtypeStruct((t, B_WIDTH + C_WIDTH), BF16),
        compiler_params=pltpu.CompilerParams(dimension_semantics=("arbitrary", "arbitrary"),
                                             vmem_limit_bytes=VMEM_LIMIT),
        name="attention",
    )(sink, qb, kb, vb, gb, qc, km, vm, gcg)


def _out_proj_kernel(x_ref, ma_ref, mbc_ref, wa_ref, wbc_ref, out_ref):
    out_ref[...] = x_ref[...] + _dot(ma_ref[...], wa_ref[...]) + _dot(mbc_ref[...], wbc_ref[...])


def _out_proj(x2, mixed_a, mixed_bc, w_a, w_bc, tm=256):
    t = x2.shape[0]
    row = lambda i: (i, 0)
    const = lambda i: (0, 0)
    return pl.pallas_call(
        _out_proj_kernel,
        grid=(t // tm,),
        in_specs=[
            pl.BlockSpec((tm, D_MODEL), row),
            pl.BlockSpec((tm, A_WIDTH), row),
            pl.BlockSpec((tm, B_WIDTH + C_WIDTH), row),
            pl.BlockSpec((A_WIDTH, D_MODEL), const),
            pl.BlockSpec((B_WIDTH + C_WIDTH, D_MODEL), const),
        ],
        out_specs=pl.BlockSpec((tm, D_MODEL), row),
        out_shape=jax.ShapeDtypeStruct((t, D_MODEL), F32),
        compiler_params=pltpu.CompilerParams(dimension_semantics=("arbitrary",), vmem_limit_bytes=VMEM_LIMIT),
        name="out_proj",
    )(x2, mixed_a, mixed_bc, w_a, w_bc)


def _layer(h, mem, norm_w, w_in, conv_w_a, a_log_fwd, a_log_bwd, dt_bias_fwd, dt_bias_bwd, o_norm_a,
           q_norm_b, k_norm_b, sink_b, mem_norm_w, w_mem_kv, q_norm_c, k_norm_c, w_out):
    batch, seq, _ = h.shape
    t = batch * seq
    x2 = h.reshape(t, D_MODEL)

    gate_w = jnp.pad(w_in[:, C_AB:C_AB + GATE_COLS], ((0, 0), (0, LANES - GATE_COLS)))
    w_pad = jnp.concatenate([w_in[:, :C_AB], gate_w, w_in[:, C_AB + GATE_COLS:]], axis=1).astype(BF16)
    zeros = jnp.zeros((LANES - 2 * A_HEADS,), F32)
    alog_row = jnp.concatenate([a_log_fwd, a_log_bwd, zeros]).reshape(1, LANES)
    dtb_row = jnp.concatenate([dt_bias_fwd, dt_bias_bwd, zeros]).reshape(1, LANES)
    cos, sin = _rope_tables(seq)

    qkv, ga, ab, abt, qb, kb, vb, gb, qc, gcg = _in_proj(
        x2, norm_w.reshape(1, D_MODEL), w_pad, cos, sin, alog_row, dtb_row,
        jnp.tile(q_norm_b, B_Q_HEADS).reshape(1, B_WIDTH),
        jnp.tile(k_norm_b, B_KV_HEADS).reshape(1, LANES),
        jnp.tile(q_norm_c, C_HEADS).reshape(1, C_WIDTH), seq)

    mlen = mem.shape[1]
    km, vm = _mem_proj(mem.reshape(batch * mlen, D_MODEL), mem_norm_w.reshape(1, D_MODEL),
                       w_mem_kv.astype(BF16), jnp.tile(k_norm_c, C_HEADS).reshape(1, C_WIDTH))

    mixed_a = _deltanet(qkv, conv_w_a, ab, abt, ga,
                        o_norm_a.reshape(1, A_HEAD_DIM), batch, seq)
    mixed_bc = _attention(sink_b, qb, kb, vb, gb, qc, km, vm, gcg, batch, seq)

    w_out_b = w_out.astype(BF16)
    out = _out_proj(x2, mixed_a, mixed_bc, w_out_b[:A_WIDTH], w_out_b[A_WIDTH:])
    return out.reshape(batch, seq, D_MODEL)


def kernel(x, mem, norm_w, w_in, conv_w_a, a_log_fwd, a_log_bwd, dt_bias_fwd, dt_bias_bwd, o_norm_a,
           q_norm_b, k_norm_b, sink_b, mem_norm_w, w_mem_kv, q_norm_c, k_norm_c, w_out):
    h = x
    for l in range(norm_w.shape[0]):
        h = _layer(h, mem, norm_w[l], w_in[l], conv_w_a[l], a_log_fwd[l], a_log_bwd[l], dt_bias_fwd[l],
                   dt_bias_bwd[l], o_norm_a[l], q_norm_b[l], k_norm_b[l], sink_b[l], mem_norm_w[l],
                   w_mem_kv[l], q_norm_c[l], k_norm_c[l], w_out[l])
    return h
```

```python
import functools

import numpy as np
import jax
import jax.numpy as jnp
from jax import lax
from jax.experimental import pallas as pl
from jax.experimental.pallas import tpu as pltpu

F32 = jnp.float32
BF16 = jnp.bfloat16

D_MODEL = 2048
A_WIDTH = 1024
A_HEAD_DIM = 128
A_HEADS = 8
CONV_K = 5
CHUNK = 64
B_WIDTH = 512
B_HEAD_DIM = 64
B_Q_HEADS = 8
B_KV_HEADS = 2
WINDOW = 128
C_WIDTH = 512
C_HEADS = 4
C_HEAD_DIM = 128
ROPE_THETA = 10000.0
EPS = 1e-6

LANES = 128
GATE_COLS = 4 * A_HEADS
A_COLS = 4 * A_WIDTH
T_QB = 0
T_KB = T_QB + B_WIDTH
T_VB = T_KB + B_KV_HEADS * B_HEAD_DIM
T_ZB = T_VB + B_KV_HEADS * B_HEAD_DIM
T_QC = T_ZB + B_WIDTH
T_ZC = T_QC + C_WIDTH
T_COLS = T_ZC + C_WIDTH

VMEM_LIMIT = 56 * 1024 * 1024

_NT = (((1,), (1,)), ((), ()))
_TN = (((0,), (0,)), ((), ()))


def _dot(a, b):
    return jnp.dot(a, b, preferred_element_type=F32)


def _dot_nt(a, b):
    return lax.dot_general(a, b, _NT, preferred_element_type=F32)


def _silu(z):
    return z * jax.nn.sigmoid(z)


def _softplus(z):
    return jnp.maximum(z, 0.0) + jnp.log1p(jnp.exp(-jnp.abs(z)))


def _rope_partner(x):
    lane = lax.broadcasted_iota(jnp.int32, x.shape, 1)
    first_half = (lane % B_HEAD_DIM) < (B_HEAD_DIM // 2)
    up = pltpu.roll(x, LANES - B_HEAD_DIM // 2, axis=1)
    down = pltpu.roll(x, B_HEAD_DIM // 2, axis=1)
    return jnp.where(first_half, up, down)


def _head_rms(x, ones_blockdiag, head_dim, w):
    ss = _dot((x * x).astype(BF16), ones_blockdiag)
    return x * lax.rsqrt(ss * (1.0 / head_dim) + EPS) * w


def _in_proj_kernel(x_ref, nw_ref, wa_ref, wg_ref, wt_ref, cos_ref, sin_ref, alog_ref, dtb_ref, qnb_ref, knb_ref,
                    qnc_ref, e64_ref, e128_ref,
                    qkv_ref, ga_ref, ab_ref, abt_ref, qb_ref, kb_ref, vb_ref, gb_ref, qc_ref, gc_ref):
    x = x_ref[...]
    ms = jnp.mean(x * x, axis=-1, keepdims=True)
    hn = (x * lax.rsqrt(ms + EPS) * nw_ref[...]).astype(BF16)
    tm = x.shape[0]

    def proj(w_ref, c0, width):
        return _dot(hn, w_ref[:, c0:c0 + width])

    step = 512
    for c in range(0, 3 * A_WIDTH, step):
        qkv_ref[:, c:c + step] = proj(wa_ref, c, step).astype(BF16)
    for c in range(0, A_WIDTH, step):
        ga_ref[:, c:c + step] = _silu(proj(wa_ref, 3 * A_WIDTH + c, step)).astype(BF16)

    acc = _dot(hn, wg_ref[...])
    lane = lax.broadcasted_iota(jnp.int32, (tm, LANES), 1)
    g = -jnp.exp(alog_ref[...]) * _softplus(acc + dtb_ref[...])
    val = jnp.where(lane < 2 * A_HEADS, g, jax.nn.sigmoid(acc))
    val = jnp.where(lane < GATE_COLS, val, 0.0)
    pos = lax.broadcasted_iota(jnp.int32, (tm, LANES), 0) % CHUNK
    pre = val
    suf = val
    s = 1
    while s < CHUNK:
        pre = pre + jnp.where(pos >= s, pltpu.roll(pre, s, axis=0), 0.0)
        suf = suf + jnp.where(pos < CHUNK - s, pltpu.roll(suf, tm - s, axis=0), 0.0)
        s *= 2
    res = jnp.where(lane < A_HEADS, pre, jnp.where(lane < 2 * A_HEADS, suf, val))
    ab_ref[...] = res
    abt_ref[0] = res.T

    cos = cos_ref[...]
    sin = sin_ref[...]
    qb = _head_rms(proj(wt_ref, T_QB, B_WIDTH), e64_ref[...], B_HEAD_DIM, qnb_ref[...])
    for c in range(0, B_WIDTH, LANES):
        t = qb[:, c:c + LANES]
        qb_ref[:, c:c + LANES] = (t * cos + _rope_partner(t) * sin).astype(BF16)
    kvb = proj(wt_ref, T_KB, 2 * LANES)
    kb = _head_rms(kvb[:, :LANES], e64_ref[:LANES, :LANES], B_HEAD_DIM, knb_ref[...])
    kb_ref[...] = (kb * cos + _rope_partner(kb) * sin).astype(BF16)
    vb_ref[...] = kvb[:, LANES:].astype(BF16)
    gb_ref[...] = _silu(proj(wt_ref, T_ZB, B_WIDTH)).astype(BF16)
    qc_ref[...] = _head_rms(proj(wt_ref, T_QC, C_WIDTH), e128_ref[...], C_HEAD_DIM, qnc_ref[...]).astype(BF16)
    gc_ref[...] = _silu(proj(wt_ref, T_ZC, C_WIDTH)).astype(BF16)


def _block_diag_ones(width, block):
    idx = np.arange(width) // block
    return jnp.asarray(idx[:, None] == idx[None, :], dtype=BF16)


def _rope_tables(seq):
    d = B_HEAD_DIM
    inv = ROPE_THETA ** (-jnp.arange(0, d, 2, dtype=F32) / d)
    ang = jnp.arange(seq, dtype=F32)[:, None] * inv[None, :]
    cos = jnp.cos(ang)
    sin = jnp.sin(ang)
    cos_h = jnp.concatenate([cos, cos], axis=-1)
    sin_h = jnp.concatenate([-sin, sin], axis=-1)
    reps = LANES // d
    return jnp.tile(cos_h, (1, reps)), jnp.tile(sin_h, (1, reps))


def _in_proj(x2, norm_w, w_a, w_g, w_t, cos, sin, alog_row, dtb_row, qnb, knb, qnc, seq, tm=256):
    t = x2.shape[0]
    tiles_per_seq = seq // tm
    row = lambda i: (i, 0)
    const = lambda i: (0, 0)
    pos = lambda i: (i % tiles_per_seq, 0)
    in_specs = [
        pl.BlockSpec((tm, D_MODEL), row),
        pl.BlockSpec((1, D_MODEL), const),
        pl.BlockSpec((D_MODEL, A_COLS), const, pipeline_mode=pl.Buffered(1)),
        pl.BlockSpec((D_MODEL, LANES), const, pipeline_mode=pl.Buffered(1)),
        pl.BlockSpec((D_MODEL, T_COLS), const, pipeline_mode=pl.Buffered(1)),
        pl.BlockSpec((tm, LANES), pos),
        pl.BlockSpec((tm, LANES), pos),
        pl.BlockSpec((1, LANES), const),
        pl.BlockSpec((1, LANES), const),
        pl.BlockSpec((1, B_WIDTH), const),
        pl.BlockSpec((1, LANES), const),
        pl.BlockSpec((1, C_WIDTH), const),
        pl.BlockSpec((B_WIDTH, B_WIDTH), const),
        pl.BlockSpec((C_WIDTH, C_WIDTH), const),
    ]
    out_shape = [
        jax.ShapeDtypeStruct((t, 3 * A_WIDTH), BF16),
        jax.ShapeDtypeStruct((t, A_WIDTH), BF16),
        jax.ShapeDtypeStruct((t, LANES), F32),
        jax.ShapeDtypeStruct((t // tm, LANES, tm), F32),
        jax.ShapeDtypeStruct((t, B_WIDTH), BF16),
        jax.ShapeDtypeStruct((t, LANES), BF16),
        jax.ShapeDtypeStruct((t, LANES), BF16),
        jax.ShapeDtypeStruct((t, B_WIDTH), BF16),
        jax.ShapeDtypeStruct((t, C_WIDTH), BF16),
        jax.ShapeDtypeStruct((t, C_WIDTH), BF16),
    ]
    out_specs = [
        pl.BlockSpec((tm, 3 * A_WIDTH), row),
        pl.BlockSpec((tm, A_WIDTH), row),
        pl.BlockSpec((tm, LANES), row),
        pl.BlockSpec((1, LANES, tm), lambda i: (i, 0, 0)),
        pl.BlockSpec((tm, B_WIDTH), row),
        pl.BlockSpec((tm, LANES), row),
        pl.BlockSpec((tm, LANES), row),
        pl.BlockSpec((tm, B_WIDTH), row),
        pl.BlockSpec((tm, C_WIDTH), row),
        pl.BlockSpec((tm, C_WIDTH), row),
    ]
    return pl.pallas_call(
        _in_proj_kernel,
        grid=(t // tm,),
        in_specs=in_specs,
        out_specs=out_specs,
        out_shape=out_shape,
        compiler_params=pltpu.CompilerParams(dimension_semantics=("arbitrary",), vmem_limit_bytes=VMEM_LIMIT),
        name="in_proj",
    )(x2, norm_w, w_a, w_g, w_t, cos, sin, alog_row, dtb_row, qnb, knb, qnc,
      _block_diag_ones(B_WIDTH, B_HEAD_DIM), _block_diag_ones(C_WIDTH, C_HEAD_DIM))


def _mem_proj_kernel(m_ref, nw_ref, w_ref, knc_ref, e128_ref, km_ref, vm_ref):
    x = m_ref[...]
    ms = jnp.mean(x * x, axis=-1, keepdims=True)
    mn = (x * lax.rsqrt(ms + EPS) * nw_ref[...]).astype(BF16)
    km = _dot(mn, w_ref[:, :C_WIDTH])
    km_ref[...] = _head_rms(km, e128_ref[...], C_HEAD_DIM, knc_ref[...]).astype(BF16)
    vm_ref[...] = _dot(mn, w_ref[:, C_WIDTH:]).astype(BF16)


def _mem_proj(mem2, mem_norm_w, w_kv, knc, tm=256):
    t = mem2.shape[0]
    row = lambda i: (i, 0)
    const = lambda i: (0, 0)
    return pl.pallas_call(
        _mem_proj_kernel,
        grid=(t // tm,),
        in_specs=[
            pl.BlockSpec((tm, D_MODEL), row),
            pl.BlockSpec((1, D_MODEL), const),
            pl.BlockSpec((D_MODEL, 2 * C_WIDTH), const),
            pl.BlockSpec((1, C_WIDTH), const),
            pl.BlockSpec((C_WIDTH, C_WIDTH), const),
        ],
        out_specs=[pl.BlockSpec((tm, C_WIDTH), row), pl.BlockSpec((tm, C_WIDTH), row)],
        out_shape=[jax.ShapeDtypeStruct((t, C_WIDTH), BF16), jax.ShapeDtypeStruct((t, C_WIDTH), BF16)],
        compiler_params=pltpu.CompilerParams(dimension_semantics=("arbitrary",), vmem_limit_bytes=VMEM_LIMIT),
        name="mem_proj",
    )(mem2, mem_norm_w, w_kv, knc, _block_diag_ones(C_WIDTH, C_HEAD_DIM))


PAD_ROWS = 8
HEADS_PER_STEP = 2
PACK = 4
PACK_ROWS = PACK * CHUNK
SQUARINGS = 5
PREP_PACKS = 2


def _deltanet_kernel(q_ref, k_ref, v_ref, wq_ref, wk_ref, wv_ref, ab_ref, gfr_ref, gbr_ref, ga_ref, on_ref,
                     out_ref,
                     pad_s, qs, ks, vs, u_s, wqd_s, qkk_s, cd_s, o_s):
    seq = q_ref.shape[0]
    nc = seq // CHUNK
    hb = HEADS_PER_STEP
    h0 = pl.program_id(1) * hb

    zeros_pad = jnp.zeros((PAD_ROWS, LANES), F32)
    pad_s[0:PAD_ROWS, :] = zeros_pad
    pad_s[PAD_ROWS + seq:PAD_ROWS + seq + PAD_ROWS, :] = zeros_pad
    slab = 256

    def slab_rows(i):
        return pl.ds(pl.multiple_of(i * slab, slab), slab)

    def conv_silu(src_ref, w_ref, dst_ref, hs, l2, scale):
        cols = slice(hs * LANES, (hs + 1) * LANES)

        def stage(i, carry):
            pad_s[pl.ds(pl.multiple_of(PAD_ROWS + i * slab, PAD_ROWS), slab), :] = (
                src_ref[slab_rows(i), cols].astype(F32))
            return carry

        lax.fori_loop(0, seq // slab, stage, 0)
        w = w_ref[:, cols]

        def conv(i, carry):
            acc = jnp.zeros((slab, LANES), F32)
            for j in range(CONV_K):
                o = PAD_ROWS + j - CONV_K // 2
                acc = acc + pad_s[pl.ds(i * slab + o, slab), :] * w[j:j + 1, :]
            y = _silu(acc)
            if l2:
                y = y * lax.rsqrt(jnp.sum(y * y, axis=-1, keepdims=True) + EPS) * scale
            dst_ref[hs, slab_rows(i), :] = y
            return carry

        lax.fori_loop(0, seq // slab, conv, 0, unroll=2)

    for hs in range(hb):
        conv_silu(q_ref, wq_ref, qs, hs, True, A_HEAD_DIM ** -0.5)
        conv_silu(k_ref, wk_ref, ks, hs, True, 1.0)
        conv_silu(v_ref, wv_ref, vs, hs, False, 1.0)

    ii = lax.broadcasted_iota(jnp.int32, (CHUNK, PACK_ROWS), 0)
    jl = lax.broadcasted_iota(jnp.int32, (CHUNK, PACK_ROWS), 1)
    lb = jl // CHUNK
    jj = jl % CHUNK
    block_diag = (lax.broadcasted_iota(jnp.int32, (PACK_ROWS, PACK_ROWS), 0) // CHUNK
                  == lax.broadcasted_iota(jnp.int32, (PACK_ROWS, PACK_ROWS), 1) // CHUNK)

    def pack_diag(full):
        out = full[0:CHUNK]
        for c in range(1, PACK):
            out = jnp.where(lb == c, full[c * CHUNK:(c + 1) * CHUNK], out)
        return out

    def pack_cols(col):
        out = None
        for c in range(PACK):
            blk = col[c * CHUNK:(c + 1) * CHUNK]
            blk = jnp.concatenate([blk, blk], axis=1)
            out = blk if out is None else jnp.where(lb == c, blk, out)
        return out

    def to_block_diag(packed):
        return jnp.where(block_diag, jnp.concatenate([packed] * PACK, axis=0), 0.0).astype(BF16)

    lane4 = lax.broadcasted_iota(jnp.int32, (PACK_ROWS, LANES), 1)

    def prep_body(it, carry):
        members = []
        for pk in range(PREP_PACKS):
            p = it * PREP_PACKS + pk
            rows = pl.ds(pl.multiple_of(p * PACK_ROWS, PACK_ROWS), PACK_ROWS)
            ab4 = ab_ref[rows, :]

            def gate_col(col):
                c = jnp.sum(jnp.where(lane4 == col, ab4, 0.0), axis=-1, keepdims=True)
                return jnp.broadcast_to(c, (PACK_ROWS, LANES))

            for hs in range(hb):
                k4 = ks[hs, rows, :]
                q4 = qs[hs, rows, :]
                v4 = vs[hs, rows, :]
                k4b = k4.astype(BF16)
                kk_p = pack_diag(_dot_nt(k4b, k4b))
                qk_p = pack_diag(_dot_nt(q4.astype(BF16), k4b))
                for d in range(2):
                    fwd = d == 0
                    gc4 = gate_col(d * A_HEADS + h0 + hs)
                    bt4 = gate_col((2 + d) * A_HEADS + h0 + hs)
                    gcr = (gfr_ref if fwd else gbr_ref)[p, pl.ds(h0 + hs, 1), :]
                    incl = (ii >= jj) if fwd else (ii <= jj)
                    strict = (ii > jj) if fwd else (ii < jj)
                    decay = jnp.where(incl, jnp.exp(jnp.where(incl, pack_cols(gc4) - gcr, 0.0)), 0.0)
                    n_cat = jnp.where(strict, -(kk_p * decay * pack_cols(bt4)), 0.0)
                    qkm = jnp.where(incl, qk_p * decay, 0.0)
                    members.append((hs, d, k4, q4, v4, gc4, bt4, n_cat, qkm, p, rows))

        eye = jnp.where(ii == jj, 1.0, 0.0)
        tinv = [eye + m[7] for m in members]
        pw = []
        for m in members:
            nb = m[7].astype(BF16)
            pw.append(_dot(nb, to_block_diag(m[7])))
        for level in range(1, SQUARINGS + 1):
            last = level == SQUARINGS
            for i in range(len(members)):
                rhs = to_block_diag(pw[i])
                if last:
                    tinv[i] = tinv[i] + _dot(tinv[i].astype(BF16), rhs)
                else:
                    both = _dot(jnp.concatenate([pw[i], tinv[i]], axis=0).astype(BF16), rhs)
                    pw[i] = both[:CHUNK]
                    tinv[i] = tinv[i] + both[CHUNK:]

        for (hs, d, k4, q4, v4, gc4, bt4, _, qkm, p, rows), t_p in zip(members, tinv):
            fwd = d == 0
            eg = jnp.exp(gc4)
            rhs = jnp.concatenate([v4 * bt4, k4 * bt4 * eg], axis=1).astype(BF16)
            uw = _dot(to_block_diag(t_p), rhs)
            u_s[hs, d, rows, :] = uw[:, :LANES]
            w = uw[:, LANES:].astype(BF16)
            qd = (q4 * eg).astype(BF16)
            for c in range(PACK):
                n = p * PACK + c
                rs = slice(c * CHUNK, (c + 1) * CHUNK)
                last_row = (c + 1) * CHUNK - 1 if fwd else c * CHUNK
                gl = gc4[last_row:last_row + 1, :]
                kd = k4[rs] * jnp.exp(gl - gc4[rs])
                wqd_s[hs, d, n, 0:CHUNK, :] = w[rs]
                wqd_s[hs, d, n, CHUNK:2 * CHUNK, :] = qd[rs]
                qkk_s[hs, d, n, 0:CHUNK, :] = qkm[:, c * CHUNK:(c + 1) * CHUNK].astype(BF16)
                qkk_s[hs, d, n, CHUNK:CHUNK + LANES, :] = kd.T.astype(BF16)
                cd_s[hs, d, n] = jnp.broadcast_to(jnp.exp(gl), (8, LANES))
        return carry

    lax.fori_loop(0, seq // (PACK_ROWS * PREP_PACKS), prep_body, 0)

    chains = [(hs, d) for hs in range(hb) for d in range(2)]

    def seq_body(i, states):
        ns = (i, nc - 1 - i)
        rows = [pl.ds(pl.multiple_of(n * CHUNK, CHUNK), CHUNK) for n in ns]
        ws = [_dot(wqd_s[hs, d, ns[d]], st.astype(BF16)) for (hs, d), st in zip(chains, states)]
        v_new = [(u_s[hs, d, rows[d], :] - w_[:CHUNK]).astype(BF16) for (hs, d), w_ in zip(chains, ws)]
        r = [_dot(qkk_s[hs, d, ns[d]], vn) for (hs, d), vn in zip(chains, v_new)]
        new_states = []
        for (hs, d), st, w_, r_ in zip(chains, states, ws, r):
            o_s[hs, d, rows[d], :] = w_[CHUNK:] + r_[:CHUNK]
            new_states.append(st * cd_s[hs, d, ns[d]][0:1, :] + r_[CHUNK:])
        return tuple(new_states)

    s0 = jnp.zeros((A_HEAD_DIM, A_HEAD_DIM), F32)
    lax.fori_loop(0, nc, seq_body, tuple(s0 for _ in chains))

    def finish(i, carry):
        for hs in range(hb):
            cols = slice(hs * LANES, (hs + 1) * LANES)
            o = o_s[hs, 0, slab_rows(i), :] + o_s[hs, 1, slab_rows(i), :]
            y = o * lax.rsqrt(jnp.mean(o * o, axis=-1, keepdims=True) + EPS) * on_ref[...]
            out_ref[slab_rows(i), cols] = (y * ga_ref[slab_rows(i), cols].astype(F32)).astype(BF16)
        return carry

    lax.fori_loop(0, seq // slab, finish, 0)


def _deltanet(qkv, conv_w, ab, abt3, ga, o_norm, batch, seq):
    nc = seq // CHUNK
    npk = seq // PACK_ROWS
    t = batch * seq
    hb = HEADS_PER_STEP
    hw = hb * LANES
    groups = A_HEADS // hb
    scratch = [
        pltpu.VMEM((seq + 2 * PAD_ROWS, LANES), F32),
        pltpu.VMEM((hb, seq, LANES), F32),
        pltpu.VMEM((hb, seq, LANES), F32),
        pltpu.VMEM((hb, seq, LANES), F32),
        pltpu.VMEM((hb, 2, seq, LANES), F32),
        pltpu.VMEM((hb, 2, nc, 2 * CHUNK, LANES), BF16),
        pltpu.VMEM((hb, 2, nc, CHUNK + LANES, CHUNK), BF16),
        pltpu.VMEM((hb, 2, nc, 8, LANES), F32),
        pltpu.VMEM((hb, 2, seq, LANES), F32),
    ]
    return pl.pallas_call(
        _deltanet_kernel,
        grid=(batch, groups),
        in_specs=[
            pl.BlockSpec((seq, hw), lambda b, g: (b, g)),
            pl.BlockSpec((seq, hw), lambda b, g: (b, groups + g)),
            pl.BlockSpec((seq, hw), lambda b, g: (b, 2 * groups + g)),
            pl.BlockSpec((CONV_K, hw), lambda b, g: (0, g)),
            pl.BlockSpec((CONV_K, hw), lambda b, g: (0, groups + g)),
            pl.BlockSpec((CONV_K, hw), lambda b, g: (0, 2 * groups + g)),
            pl.BlockSpec((seq, LANES), lambda b, g: (b, 0)),
            pl.BlockSpec((npk, A_HEADS, PACK_ROWS), lambda b, g: (b, 0, 0)),
            pl.BlockSpec((npk, A_HEADS, PACK_ROWS), lambda b, g: (b, 1, 0)),
            pl.BlockSpec((seq, hw), lambda b, g: (b, g)),
            pl.BlockSpec((1, LANES), lambda b, g: (0, 0)),
        ],
        out_specs=pl.BlockSpec((seq, hw), lambda b, g: (b, g)),
        out_shape=jax.ShapeDtypeStruct((t, A_WIDTH), BF16),
        scratch_shapes=scratch,
        compiler_params=pltpu.CompilerParams(dimension_semantics=("arbitrary", "arbitrary"),
                                             vmem_limit_bytes=VMEM_LIMIT),
        name="deltanet",
    )(qkv, qkv, qkv, conv_w, conv_w, conv_w, ab, abt3, abt3, ga, o_norm)


def _attn_kernel(sink_ref, qb_ref, kb_ref, vb_ref, gb_ref, qc_ref, km_ref, vm_ref, gc_ref, out_ref):
    n = pl.program_id(1)
    nb = pl.num_programs(1)
    w = WINDOW
    lane = lax.broadcasted_iota(jnp.int32, (w, LANES), 1)
    lower = lane < B_HEAD_DIM
    ri = lax.broadcasted_iota(jnp.int32, (2 * w, w), 0) % w
    ci = lax.broadcasted_iota(jnp.int32, (2 * w, w), 1)
    top = lax.broadcasted_iota(jnp.int32, (2 * w, 1), 0) < w

    blocks = []
    for rel in (-1, 0, 1):
        j = n + rel
        start = pl.multiple_of(jnp.clip(j, 0, nb - 1) * w, w)
        kblk = kb_ref[pl.ds(start, w), :].astype(F32)
        vblk = vb_ref[pl.ds(start, w), :].astype(F32)
        valid = jnp.logical_and(j >= 0, j < nb)
        if rel == -1:
            mask = jnp.logical_and(ci >= ri, valid)
        elif rel == 0:
            mask = None
        else:
            mask = jnp.logical_and(ci <= ri, valid)
        blocks.append((kblk, pltpu.roll(kblk, B_HEAD_DIM, axis=1), vblk, pltpu.roll(vblk, B_HEAD_DIM, axis=1), mask))

    scale = B_HEAD_DIM ** -0.5
    for hk in range(B_KV_HEADS):
        c0 = hk * 2 * LANES
        q2 = jnp.concatenate([qb_ref[:, c0:c0 + LANES], qb_ref[:, c0 + LANES:c0 + 2 * LANES]], axis=0)
        acc = jnp.zeros((2 * w, LANES), F32)
        for half in range(2):
            ks_, vs_ = [], []
            for kblk, ksw, vblk, vsw, _ in blocks:
                in_place = (hk == half)
                keep = lower if half == 0 else jnp.logical_not(lower)
                ks_.append(jnp.where(keep, kblk if in_place else ksw, 0.0).astype(BF16))
                vs_.append(jnp.where(keep, vblk if in_place else vsw, 0.0).astype(BF16))
            sink_col = jnp.where(top, sink_ref[4 * hk + half], sink_ref[4 * hk + 2 + half])
            ss = []
            for (_, _, _, _, mask), kk in zip(blocks, ks_):
                s = _dot_nt(q2, kk) * scale
                if mask is not None:
                    s = jnp.where(mask, s, -jnp.inf)
                ss.append(s)
            m = jnp.maximum(sink_col, jnp.max(jnp.maximum(jnp.maximum(ss[0], ss[1]), ss[2]),
                                              axis=-1, keepdims=True))
            ps = [jnp.exp(s - m) for s in ss]
            den = jnp.exp(sink_col - m) + jnp.sum(ps[0] + ps[1] + ps[2], axis=-1, keepdims=True)
            pv = jnp.zeros((2 * w, LANES), F32)
            for p, vv in zip(ps, vs_):
                pv = pv + _dot(p.astype(BF16), vv)
            acc = acc + pv / den
        out_ref[:, c0:c0 + LANES] = (acc[:w] * gb_ref[:, c0:c0 + LANES].astype(F32)).astype(BF16)
        out_ref[:, c0 + LANES:c0 + 2 * LANES] = (acc[w:] * gb_ref[:, c0 + LANES:c0 + 2 * LANES].astype(F32)).astype(BF16)

    cscale = C_HEAD_DIM ** -0.5
    for hc in range(C_HEADS):
        c0 = hc * C_HEAD_DIM
        s = _dot_nt(qc_ref[:, c0:c0 + C_HEAD_DIM], km_ref[:, c0:c0 + C_HEAD_DIM]) * cscale
        m = jnp.max(s, axis=-1, keepdims=True)
        p = jnp.exp(s - m)
        den = jnp.sum(p, axis=-1, keepdims=True)
        o = _dot(p.astype(BF16), vm_ref[:, c0:c0 + C_HEAD_DIM]) / den
        out_ref[:, B_WIDTH + c0:B_WIDTH + c0 + C_HEAD_DIM] = (
            o * gc_ref[:, c0:c0 + C_HEAD_DIM].astype(F32)).astype(BF16)


def _attention(sink, qb, kb, vb, gb, qc, km, vm, gcg, batch, seq):
    nb = seq // WINDOW
    t = batch * seq
    mlen = km.shape[0] // batch
    qrow = lambda b, n: (b * nb + n, 0)
    per_b = lambda b, n: (b, 0)
    return pl.pallas_call(
        _attn_kernel,
        grid=(batch, nb),
        in_specs=[
            pl.BlockSpec(memory_space=pltpu.SMEM),
            pl.BlockSpec((WINDOW, B_WIDTH), qrow),
            pl.BlockSpec((seq, LANES), per_b),
            pl.BlockSpec((seq, LANES), per_b),
            pl.BlockSpec((WINDOW, B_WIDTH), qrow),
            pl.BlockSpec((WINDOW, C_WIDTH), qrow),
            pl.BlockSpec((mlen, C_WIDTH), per_b),
            pl.BlockSpec((mlen, C_WIDTH), per_b),
            pl.BlockSpec((WINDOW, C_WIDTH), qrow),
        ],
        out_specs=pl.BlockSpec((WINDOW, B_WIDTH + C_WIDTH), qrow),
        out_shape=jax.ShapeDtypeStruct((t, B_WIDTH + C_WIDTH), BF16),
        compiler_params=pltpu.CompilerParams(dimension_semantics=("arbitrary", "arbitrary"),
                                             vmem_limit_bytes=VMEM_LIMIT),
        name="attention",
    )(sink, qb, kb, vb, gb, qc, km, vm, gcg)


def _out_proj_kernel(x_ref, ma_ref, mbc_ref, wa_ref, wbc_ref, out_ref):
    out_ref[...] = x_ref[...] + _dot(ma_ref[...], wa_ref[...]) + _dot(mbc_ref[...], wbc_ref[...])


def _out_proj(x2, mixed_a, mixed_bc, w_out, tm=256):
    t = x2.shape[0]
    row = lambda i: (i, 0)
    const = lambda i: (0, 0)
    return pl.pallas_call(
        _out_proj_kernel,
        grid=(t // tm,),
        in_specs=[
            pl.BlockSpec((tm, D_MODEL), row),
            pl.BlockSpec((tm, A_WIDTH), row),
            pl.BlockSpec((tm, B_WIDTH + C_WIDTH), row),
            pl.BlockSpec((A_WIDTH, D_MODEL), const),
            pl.BlockSpec((B_WIDTH + C_WIDTH, D_MODEL), lambda i: (1, 0)),
        ],
        out_specs=pl.BlockSpec((tm, D_MODEL), row),
        out_shape=jax.ShapeDtypeStruct((t, D_MODEL), F32),
        compiler_params=pltpu.CompilerParams(dimension_semantics=("arbitrary",), vmem_limit_bytes=VMEM_LIMIT),
        name="out_proj",
    )(x2, mixed_a, mixed_bc, w_out, w_out)


def _layer(h, mem, norm_w, w_in, conv_w_a, a_log_fwd, a_log_bwd, dt_bias_fwd, dt_bias_bwd, o_norm_a,
           q_norm_b, k_norm_b, sink_b, mem_norm_w, w_mem_kv, q_norm_c, k_norm_c, w_out):
    batch, seq, _ = h.shape
    t = batch * seq
    x2 = h.reshape(t, D_MODEL)

    w_a = w_in[:, :A_COLS].astype(BF16)
    w_g = jnp.pad(w_in[:, A_COLS:A_COLS + GATE_COLS], ((0, 0), (0, LANES - GATE_COLS))).astype(BF16)
    w_t = w_in[:, A_COLS + GATE_COLS:].astype(BF16)
    zeros = jnp.zeros((LANES - 2 * A_HEADS,), F32)
    alog_row = jnp.concatenate([a_log_fwd, a_log_bwd, zeros]).reshape(1, LANES)
    dtb_row = jnp.concatenate([dt_bias_fwd, dt_bias_bwd, zeros]).reshape(1, LANES)
    cos, sin = _rope_tables(seq)

    qkv, ga, ab, abt, qb, kb, vb, gb, qc, gcg = _in_proj(
        x2, norm_w.reshape(1, D_MODEL), w_a, w_g, w_t, cos, sin, alog_row, dtb_row,
        jnp.tile(q_norm_b, B_Q_HEADS).reshape(1, B_WIDTH),
        jnp.tile(k_norm_b, B_KV_HEADS).reshape(1, LANES),
        jnp.tile(q_norm_c, C_HEADS).reshape(1, C_WIDTH), seq)

    mlen = mem.shape[1]
    km, vm = _mem_proj(mem.reshape(batch * mlen, D_MODEL), mem_norm_w.reshape(1, D_MODEL),
                       w_mem_kv.astype(BF16), jnp.tile(k_norm_c, C_HEADS).reshape(1, C_WIDTH))

    mixed_a = _deltanet(qkv, conv_w_a, ab, abt, ga,
                        o_norm_a.reshape(1, A_HEAD_DIM), batch, seq)
    mixed_bc = _attention(sink_b, qb, kb, vb, gb, qc, km, vm, gcg, batch, seq)

    out = _out_proj(x2, mixed_a, mixed_bc, w_out.astype(BF16))
    return out.reshape(batch, seq, D_MODEL)


def kernel(x, mem, norm_w, w_in, conv_w_a, a_log_fwd, a_log_bwd, dt_bias_fwd, dt_bias_bwd, o_norm_a,
           q_norm_b, k_norm_b, sink_b, mem_norm_w, w_mem_kv, q_norm_c, k_norm_c, w_out):
    h = x
    for l in range(norm_w.shape[0]):
        h = _layer(h, mem, norm_w[l], w_in[l], conv_w_a[l], a_log_fwd[l], a_log_bwd[l], dt_bias_fwd[l],
                   dt_bias_bwd[l], o_norm_a[l], q_norm_b[l], k_norm_b[l], sink_b[l], mem_norm_w[l],
                   w_mem_kv[l], q_norm_c[l], k_norm_c[l], w_out[l])
    return h
```

```python
import functools

import numpy as np
import jax
import jax.numpy as jnp
from jax import lax
from jax.experimental import pallas as pl
from jax.experimental.pallas import tpu as pltpu

F32 = jnp.float32
BF16 = jnp.bfloat16

D_MODEL = 2048
A_WIDTH = 1024
A_HEAD_DIM = 128
A_HEADS = 8
CONV_K = 5
CHUNK = 64
B_WIDTH = 512
B_HEAD_DIM = 64
B_Q_HEADS = 8
B_KV_HEADS = 2
WINDOW = 128
C_WIDTH = 512
C_HEADS = 4
C_HEAD_DIM = 128
ROPE_THETA = 10000.0
EPS = 1e-6

LANES = 128
GATE_COLS = 4 * A_HEADS
A_COLS = 4 * A_WIDTH
T_QB = 0
T_KB = T_QB + B_WIDTH
T_VB = T_KB + B_KV_HEADS * B_HEAD_DIM
T_ZB = T_VB + B_KV_HEADS * B_HEAD_DIM
T_QC = T_ZB + B_WIDTH
T_ZC = T_QC + C_WIDTH
T_COLS = T_ZC + C_WIDTH

VMEM_LIMIT = 56 * 1024 * 1024
CONV_SHIFT = 16
CONV_HALO = 24
CONV_COLS = 256

_NT = (((1,), (1,)), ((), ()))
_TN = (((0,), (0,)), ((), ()))


def _dot(a, b):
    return jnp.dot(a, b, preferred_element_type=F32)


def _dot_nt(a, b):
    return lax.dot_general(a, b, _NT, preferred_element_type=F32)


def _silu(z):
    return z * jax.nn.sigmoid(z)


def _softplus(z):
    return jnp.maximum(z, 0.0) + jnp.log1p(jnp.exp(-jnp.abs(z)))


def _rope_partner(x):
    lane = lax.broadcasted_iota(jnp.int32, x.shape, 1)
    first_half = (lane % B_HEAD_DIM) < (B_HEAD_DIM // 2)
    up = pltpu.roll(x, LANES - B_HEAD_DIM // 2, axis=1)
    down = pltpu.roll(x, B_HEAD_DIM // 2, axis=1)
    return jnp.where(first_half, up, down)


def _head_rms(x, ones_blockdiag, head_dim, w):
    ss = _dot((x * x).astype(BF16), ones_blockdiag)
    return x * lax.rsqrt(ss * (1.0 / head_dim) + EPS) * w


def _in_proj_kernel(tiles_per_seq, x_ref, nw_ref, wa_ref, wg_ref, wt_ref, cw_ref, cos_ref, sin_ref, alog_ref,
                    dtb_ref, qnb_ref, knb_ref, qnc_ref, e64_ref, e128_ref,
                    qkv_ref, ga_ref, ab_ref, abt_ref, qb_ref, kb_ref, vb_ref, gb_ref, qc_ref, gc_ref,
                    ext_s):
    step_id = pl.program_id(0)

    @pl.when(step_id == 0)
    def _():
        ext_s[0:CONV_HALO, :] = jnp.zeros((CONV_HALO, ext_s.shape[1]), F32)

    seq_start = (step_id % tiles_per_seq) == 0
    x = x_ref[...]
    ms = jnp.mean(x * x, axis=-1, keepdims=True)
    hn = (x * lax.rsqrt(ms + EPS) * nw_ref[...]).astype(BF16)
    tm = x.shape[0]

    def proj(w_ref, c0, width):
        return _dot(hn, w_ref[:, c0:c0 + width])

    fix_rows = lax.broadcasted_iota(jnp.int32, (2 * 8, 1), 0) + (CONV_SHIFT - 8)
    first = CONV_HALO - CONV_SHIFT - CONV_K // 2
    lo = CONV_SHIFT - 8

    def conv_head(c):
        lanes = slice(c, c + A_HEAD_DIM)
        w = cw_ref[:, lanes]
        taps = [w[j:j + 1, :] for j in range(CONV_K)]
        y = ext_s[first:first + tm, lanes] * taps[0]
        for j in range(1, CONV_K):
            y = y + ext_s[first + j:first + j + tm, lanes] * taps[j]
        cur0 = ext_s[CONV_HALO:CONV_HALO + 1, lanes]
        cur1 = ext_s[CONV_HALO + 1:CONV_HALO + 2, lanes]
        old0 = ext_s[CONV_HALO - 2:CONV_HALO - 1, lanes]
        old1 = ext_s[CONV_HALO - 1:CONV_HALO, lanes]
        cross = (jnp.where(fix_rows == CONV_SHIFT - 2, taps[4] * cur0, 0.0)
                 + jnp.where(fix_rows == CONV_SHIFT - 1, taps[3] * cur0 + taps[4] * cur1, 0.0)
                 + jnp.where(fix_rows == CONV_SHIFT, taps[0] * old0 + taps[1] * old1, 0.0)
                 + jnp.where(fix_rows == CONV_SHIFT + 1, taps[0] * old1, 0.0))
        fixed = y[lo:lo + 16] - jnp.where(seq_start, cross, 0.0)
        y = _silu(jnp.concatenate([y[:lo], fixed, y[lo + 16:]], axis=0))
        if c < 2 * A_WIDTH:
            scale = A_HEAD_DIM ** -0.5 if c < A_WIDTH else 1.0
            y = y * (lax.rsqrt(jnp.sum(y * y, axis=-1, keepdims=True) + EPS) * scale)
        qkv_ref[:, lanes] = y.astype(BF16)

    for c in range(0, 3 * A_WIDTH, CONV_COLS):
        ext_s[CONV_HALO:CONV_HALO + tm, c:c + CONV_COLS] = proj(wa_ref, c, CONV_COLS)
        for hc in range(c, c + CONV_COLS, A_HEAD_DIM):
            conv_head(hc)
    ext_s[0:CONV_HALO, :] = ext_s[tm:tm + CONV_HALO, :]

    step = 512
    for c in range(0, A_WIDTH, step):
        ga_ref[:, c:c + step] = _silu(proj(wa_ref, 3 * A_WIDTH + c, step)).astype(BF16)

    acc = _dot(hn, wg_ref[...])
    lane = lax.broadcasted_iota(jnp.int32, (tm, LANES), 1)
    g = -jnp.exp(alog_ref[...]) * _softplus(acc + dtb_ref[...])
    val = jnp.where(lane < 2 * A_HEADS, g, jax.nn.sigmoid(acc))
    val = jnp.where(lane < GATE_COLS, val, 0.0)
    pos = lax.broadcasted_iota(jnp.int32, (tm, LANES), 0) % CHUNK
    pre = val
    suf = val
    s = 1
    while s < CHUNK:
        pre = pre + jnp.where(pos >= s, pltpu.roll(pre, s, axis=0), 0.0)
        suf = suf + jnp.where(pos < CHUNK - s, pltpu.roll(suf, tm - s, axis=0), 0.0)
        s *= 2
    res = jnp.where(lane < A_HEADS, pre, jnp.where(lane < 2 * A_HEADS, suf, val))
    ab_ref[...] = res
    abt_ref[0] = res.T

    cos = cos_ref[...]
    sin = sin_ref[...]
    qb = _head_rms(proj(wt_ref, T_QB, B_WIDTH), e64_ref[...], B_HEAD_DIM, qnb_ref[...])
    for c in range(0, B_WIDTH, LANES):
        t = qb[:, c:c + LANES]
        qb_ref[:, c:c + LANES] = (t * cos + _rope_partner(t) * sin).astype(BF16)
    kvb = proj(wt_ref, T_KB, 2 * LANES)
    kb = _head_rms(kvb[:, :LANES], e64_ref[:LANES, :LANES], B_HEAD_DIM, knb_ref[...])
    kb_ref[...] = (kb * cos + _rope_partner(kb) * sin).astype(BF16)
    vb_ref[...] = kvb[:, LANES:].astype(BF16)
    gb_ref[...] = _silu(proj(wt_ref, T_ZB, B_WIDTH)).astype(BF16)
    qc_ref[...] = _head_rms(proj(wt_ref, T_QC, C_WIDTH), e128_ref[...], C_HEAD_DIM, qnc_ref[...]).astype(BF16)
    gc_ref[...] = _silu(proj(wt_ref, T_ZC, C_WIDTH)).astype(BF16)


def _block_diag_ones(width, block):
    idx = np.arange(width) // block
    return jnp.asarray(idx[:, None] == idx[None, :], dtype=BF16)


def _rope_tables(seq):
    d = B_HEAD_DIM
    inv = ROPE_THETA ** (-jnp.arange(0, d, 2, dtype=F32) / d)
    ang = jnp.arange(seq, dtype=F32)[:, None] * inv[None, :]
    cos = jnp.cos(ang)
    sin = jnp.sin(ang)
    cos_h = jnp.concatenate([cos, cos], axis=-1)
    sin_h = jnp.concatenate([-sin, sin], axis=-1)
    reps = LANES // d
    return jnp.tile(cos_h, (1, reps)), jnp.tile(sin_h, (1, reps))


def _in_proj(x2, norm_w, w_a, w_g, w_t, conv_w, cos, sin, alog_row, dtb_row, qnb, knb, qnc, seq, tm=256):
    t = x2.shape[0]
    tiles_per_seq = seq // tm
    nt = t // tm
    row = lambda i: (jnp.minimum(i, nt - 1), 0)
    const = lambda i: (0, 0)
    pos = lambda i: (jnp.minimum(i, nt - 1) % tiles_per_seq, 0)
    in_specs = [
        pl.BlockSpec((tm, D_MODEL), row),
        pl.BlockSpec((1, D_MODEL), const),
        pl.BlockSpec((D_MODEL, A_COLS), const, pipeline_mode=pl.Buffered(1)),
        pl.BlockSpec((D_MODEL, LANES), const, pipeline_mode=pl.Buffered(1)),
        pl.BlockSpec((D_MODEL, T_COLS), const, pipeline_mode=pl.Buffered(1)),
        pl.BlockSpec((CONV_K, 3 * A_WIDTH), const),
        pl.BlockSpec((tm, LANES), pos),
        pl.BlockSpec((tm, LANES), pos),
        pl.BlockSpec((1, LANES), const),
        pl.BlockSpec((1, LANES), const),
        pl.BlockSpec((1, B_WIDTH), const),
        pl.BlockSpec((1, LANES), const),
        pl.BlockSpec((1, C_WIDTH), const),
        pl.BlockSpec((B_WIDTH, B_WIDTH), const),
        pl.BlockSpec((C_WIDTH, C_WIDTH), const),
    ]
    out_shape = [
        jax.ShapeDtypeStruct((t + tm, 3 * A_WIDTH), BF16),
        jax.ShapeDtypeStruct((t, A_WIDTH), BF16),
        jax.ShapeDtypeStruct((t, LANES), F32),
        jax.ShapeDtypeStruct((t // tm, LANES, tm), F32),
        jax.ShapeDtypeStruct((t, B_WIDTH), BF16),
        jax.ShapeDtypeStruct((t, LANES), BF16),
        jax.ShapeDtypeStruct((t, LANES), BF16),
        jax.ShapeDtypeStruct((t, B_WIDTH), BF16),
        jax.ShapeDtypeStruct((t, C_WIDTH), BF16),
        jax.ShapeDtypeStruct((t, C_WIDTH), BF16),
    ]
    out_specs = [
        pl.BlockSpec((tm, 3 * A_WIDTH), lambda i: (i, 0)),
        pl.BlockSpec((tm, A_WIDTH), row),
        pl.BlockSpec((tm, LANES), row),
        pl.BlockSpec((1, LANES, tm), lambda i: (jnp.minimum(i, nt - 1), 0, 0)),
        pl.BlockSpec((tm, B_WIDTH), row),
        pl.BlockSpec((tm, LANES), row),
        pl.BlockSpec((tm, LANES), row),
        pl.BlockSpec((tm, B_WIDTH), row),
        pl.BlockSpec((tm, C_WIDTH), row),
        pl.BlockSpec((tm, C_WIDTH), row),
    ]
    return pl.pallas_call(
        functools.partial(_in_proj_kernel, tiles_per_seq),
        grid=(nt + 1,),
        in_specs=in_specs,
        out_specs=out_specs,
        out_shape=out_shape,
        scratch_shapes=[pltpu.VMEM((CONV_HALO + tm, 3 * A_WIDTH), F32)],
        compiler_params=pltpu.CompilerParams(dimension_semantics=("arbitrary",), vmem_limit_bytes=VMEM_LIMIT),
        name="in_proj",
    )(x2, norm_w, w_a, w_g, w_t, conv_w, cos, sin, alog_row, dtb_row, qnb, knb, qnc,
      _block_diag_ones(B_WIDTH, B_HEAD_DIM), _block_diag_ones(C_WIDTH, C_HEAD_DIM))


def _mem_proj_kernel(m_ref, nw_ref, w_ref, knc_ref, e128_ref, km_ref, vm_ref):
    x = m_ref[...]
    ms = jnp.mean(x * x, axis=-1, keepdims=True)
    mn = (x * lax.rsqrt(ms + EPS) * nw_ref[...]).astype(BF16)
    km = _dot(mn, w_ref[:, :C_WIDTH])
    km_ref[...] = _head_rms(km, e128_ref[...], C_HEAD_DIM, knc_ref[...]).astype(BF16)
    vm_ref[...] = _dot(mn, w_ref[:, C_WIDTH:]).astype(BF16)


def _mem_proj(mem2, mem_norm_w, w_kv, knc, tm=256):
    t = mem2.shape[0]
    row = lambda i: (i, 0)
    const = lambda i: (0, 0)
    return pl.pallas_call(
        _mem_proj_kernel,
        grid=(t // tm,),
        in_specs=[
            pl.BlockSpec((tm, D_MODEL), row),
            pl.BlockSpec((1, D_MODEL), const),
            pl.BlockSpec((D_MODEL, 2 * C_WIDTH), const),
            pl.BlockSpec((1, C_WIDTH), const),
            pl.BlockSpec((C_WIDTH, C_WIDTH), const),
        ],
        out_specs=[pl.BlockSpec((tm, C_WIDTH), row), pl.BlockSpec((tm, C_WIDTH), row)],
        out_shape=[jax.ShapeDtypeStruct((t, C_WIDTH), BF16), jax.ShapeDtypeStruct((t, C_WIDTH), BF16)],
        compiler_params=pltpu.CompilerParams(dimension_semantics=("arbitrary",), vmem_limit_bytes=VMEM_LIMIT),
        name="mem_proj",
    )(mem2, mem_norm_w, w_kv, knc, _block_diag_ones(C_WIDTH, C_HEAD_DIM))


HEADS_PER_STEP = 2
PACK = 4
PACK_ROWS = PACK * CHUNK
SQUARINGS = 5
PREP_PACKS = 2


def _deltanet_kernel(q_ref, k_ref, v_ref, qt_ref, kt_ref, vt_ref, ab_ref, gfr_ref, gbr_ref, ga_ref, on_ref,
                     out_ref,
                     qs, ks, vs, u_s, wqd_s, qkk_s, cd_s, o_s):
    seq = q_ref.shape[0]
    nc = seq // CHUNK
    hb = HEADS_PER_STEP
    h0 = pl.program_id(1) * hb
    slab = 256

    def slab_rows(i):
        return pl.ds(pl.multiple_of(i * slab, slab), slab)

    def stage(src_ref, tail_ref, dst_ref, hs):
        cols = slice(hs * LANES, (hs + 1) * LANES)

        def copy(i, carry):
            src_rows = pl.ds(pl.multiple_of(CONV_SHIFT + i * slab, CONV_SHIFT), slab)
            dst_ref[hs, slab_rows(i), :] = src_ref[src_rows, cols].astype(F32)
            return carry

        lax.fori_loop(0, seq // slab - 1, copy, 0)
        last = seq - slab
        dst_ref[hs, last:seq - CONV_SHIFT, :] = src_ref[last + CONV_SHIFT:seq, cols].astype(F32)
        dst_ref[hs, seq - CONV_SHIFT:seq, :] = tail_ref[:, cols].astype(F32)

    for hs in range(hb):
        stage(q_ref, qt_ref, qs, hs)
        stage(k_ref, kt_ref, ks, hs)
        stage(v_ref, vt_ref, vs, hs)

    ii = lax.broadcasted_iota(jnp.int32, (CHUNK, PACK_ROWS), 0)
    jl = lax.broadcasted_iota(jnp.int32, (CHUNK, PACK_ROWS), 1)
    lb = jl // CHUNK
    jj = jl % CHUNK
    block_diag = (lax.broadcasted_iota(jnp.int32, (PACK_ROWS, PACK_ROWS), 0) // CHUNK
                  == lax.broadcasted_iota(jnp.int32, (PACK_ROWS, PACK_ROWS), 1) // CHUNK)

    def pack_diag(full):
        out = full[0:CHUNK]
        for c in range(1, PACK):
            out = jnp.where(lb == c, full[c * CHUNK:(c + 1) * CHUNK], out)
        return out

    def pack_cols(col):
        out = None
        for c in range(PACK):
            blk = col[c * CHUNK:(c + 1) * CHUNK]
            blk = jnp.concatenate([blk, blk], axis=1)
            out = blk if out is None else jnp.where(lb == c, blk, out)
        return out

    def to_block_diag(packed):
        return jnp.where(block_diag, jnp.concatenate([packed] * PACK, axis=0), 0.0).astype(BF16)

    lane4 = lax.broadcasted_iota(jnp.int32, (PACK_ROWS, LANES), 1)
    npk = seq // PACK_ROWS
    chains = [(hs, d) for hs in range(hb) for d in range(2)]

    def prep_gen(it):
        members = []
        for pk in range(PREP_PACKS):
            for d in range(2):
                fwd = d == 0
                p = it * PREP_PACKS + pk
                if not fwd:
                    p = npk - 1 - p
                rows = pl.ds(pl.multiple_of(p * PACK_ROWS, PACK_ROWS), PACK_ROWS)
                ab4 = ab_ref[rows, :]

                def gate_col(col, ab4=ab4):
                    c = jnp.sum(jnp.where(lane4 == col, ab4, 0.0), axis=-1, keepdims=True)
                    return jnp.broadcast_to(c, (PACK_ROWS, LANES))

                incl = (ii >= jj) if fwd else (ii <= jj)
                strict = (ii > jj) if fwd else (ii < jj)
                for hs in range(hb):
                    k4 = ks[hs, rows, :]
                    q4 = qs[hs, rows, :]
                    v4 = vs[hs, rows, :]
                    k4b = k4.astype(BF16)
                    kk_p = pack_diag(_dot_nt(k4b, k4b))
                    qk_p = pack_diag(_dot_nt(q4.astype(BF16), k4b))
                    gc4 = gate_col(d * A_HEADS + h0 + hs)
                    bt4 = gate_col((2 + d) * A_HEADS + h0 + hs)
                    gcr = (gfr_ref if fwd else gbr_ref)[p, pl.ds(h0 + hs, 1), :]
                    decay = jnp.where(incl, jnp.exp(jnp.where(incl, pack_cols(gc4) - gcr, 0.0)), 0.0)
                    n_cat = jnp.where(strict, -(kk_p * decay * pack_cols(bt4)), 0.0)
                    qkm = jnp.where(incl, qk_p * decay, 0.0)
                    members.append((hs, d, k4, q4, v4, gc4, bt4, n_cat, qkm, p, rows))
        yield

        eye = jnp.where(ii == jj, 1.0, 0.0)
        tinv = [eye + m[7] for m in members]
        pw = [_dot(m[7].astype(BF16), to_block_diag(m[7])) for m in members]
        yield
        for level in range(1, SQUARINGS + 1):
            last = level == SQUARINGS
            for i in range(len(members)):
                rhs = to_block_diag(pw[i])
                if last:
                    tinv[i] = tinv[i] + _dot(tinv[i].astype(BF16), rhs)
                else:
                    both = _dot(jnp.concatenate([pw[i], tinv[i]], axis=0).astype(BF16), rhs)
                    pw[i] = both[:CHUNK]
                    tinv[i] = tinv[i] + both[CHUNK:]
            yield

        uws = []
        for (hs, d, k4, q4, v4, gc4, bt4, _, qkm, p, rows), t_p in zip(members, tinv):
            eg = jnp.exp(gc4)
            rhs = jnp.concatenate([v4 * bt4, k4 * bt4 * eg], axis=1).astype(BF16)
            uws.append((_dot(to_block_diag(t_p), rhs), eg))
        yield

        for (hs, d, k4, q4, v4, gc4, bt4, _, qkm, p, rows), (uw, eg) in zip(members, uws):
            fwd = d == 0
            u_s[hs, d, rows, :] = uw[:, :LANES]
            w = uw[:, LANES:].astype(BF16)
            qd = (q4 * eg).astype(BF16)
            for c in range(PACK):
                n = p * PACK + c
                rs = slice(c * CHUNK, (c + 1) * CHUNK)
                last_row = (c + 1) * CHUNK - 1 if fwd else c * CHUNK
                gl = gc4[last_row:last_row + 1, :]
                kd = k4[rs] * jnp.exp(gl - gc4[rs])
                wqd_s[hs, d, n, 0:CHUNK, :] = w[rs]
                wqd_s[hs, d, n, CHUNK:2 * CHUNK, :] = qd[rs]
                qkk_s[hs, d, n, 0:CHUNK, :] = qkm[:, c * CHUNK:(c + 1) * CHUNK].astype(BF16)
                qkk_s[hs, d, n, CHUNK:CHUNK + LANES, :] = kd.T.astype(BF16)
                cd_s[hs, d, n] = jnp.broadcast_to(jnp.exp(gl), (8, LANES))

    steps_per_stage = PREP_PACKS * PACK

    def seq_gen(it, states, result):
        states = list(states)
        for k in range(steps_per_stage):
            i = it * steps_per_stage + k
            ns = (i, nc - 1 - i)
            rows = [pl.ds(pl.multiple_of(n * CHUNK, CHUNK), CHUNK) for n in ns]
            ws = [_dot(wqd_s[hs, d, ns[d]], st.astype(BF16)) for (hs, d), st in zip(chains, states)]
            yield
            v_new = [(u_s[hs, d, rows[d], :] - w_[:CHUNK]).astype(BF16) for (hs, d), w_ in zip(chains, ws)]
            r = [_dot(qkk_s[hs, d, ns[d]], vn) for (hs, d), vn in zip(chains, v_new)]
            for c, ((hs, d), w_, r_) in enumerate(zip(chains, ws, r)):
                o_s[hs, d, rows[d], :] = w_[CHUNK:] + r_[:CHUNK]
                states[c] = states[c] * cd_s[hs, d, ns[d]][0:1, :] + r_[CHUNK:]
            yield
        result.append(tuple(states))

    def run(gen):
        for _ in gen:
            pass

    def fused_body(j, states):
        result = []
        pg = prep_gen(j + 1)
        sg = seq_gen(j, states, result)
        for _ in range(steps_per_stage):
            next(pg)
            next(sg)
            next(sg)
        run(sg)
        run(pg)
        return result[0]

    n_stages = npk // PREP_PACKS
    s0 = jnp.zeros((A_HEAD_DIM, A_HEAD_DIM), F32)
    run(prep_gen(0))
    states = lax.fori_loop(0, n_stages - 1, fused_body, tuple(s0 for _ in chains))
    run(seq_gen(n_stages - 1, states, []))

    def finish(i, carry):
        for hs in range(hb):
            cols = slice(hs * LANES, (hs + 1) * LANES)
            o = o_s[hs, 0, slab_rows(i), :] + o_s[hs, 1, slab_rows(i), :]
            y = o * lax.rsqrt(jnp.mean(o * o, axis=-1, keepdims=True) + EPS) * on_ref[...]
            out_ref[slab_rows(i), cols] = (y * ga_ref[slab_rows(i), cols].astype(F32)).astype(BF16)
        return carry

    lax.fori_loop(0, seq // slab, finish, 0)


def _deltanet(qkv, ab, abt3, ga, o_norm, batch, seq):
    nc = seq // CHUNK
    npk = seq // PACK_ROWS
    t = batch * seq
    hb = HEADS_PER_STEP
    hw = hb * LANES
    groups = A_HEADS // hb
    tail_blocks = seq // CONV_SHIFT
    scratch = [
        pltpu.VMEM((hb, seq, LANES), F32),
        pltpu.VMEM((hb, seq, LANES), F32),
        pltpu.VMEM((hb, seq, LANES), F32),
        pltpu.VMEM((hb, 2, seq, LANES), F32),
        pltpu.VMEM((hb, 2, nc, 2 * CHUNK, LANES), BF16),
        pltpu.VMEM((hb, 2, nc, CHUNK + LANES, CHUNK), BF16),
        pltpu.VMEM((hb, 2, nc, 8, LANES), F32),
        pltpu.VMEM((hb, 2, seq, LANES), F32),
    ]
    return pl.pallas_call(
        _deltanet_kernel,
        grid=(batch, groups),
        in_specs=[
            pl.BlockSpec((seq, hw), lambda b, g: (b, g)),
            pl.BlockSpec((seq, hw), lambda b, g: (b, groups + g)),
            pl.BlockSpec((seq, hw), lambda b, g: (b, 2 * groups + g)),
            pl.BlockSpec((CONV_SHIFT, hw), lambda b, g: ((b + 1) * tail_blocks, g)),
            pl.BlockSpec((CONV_SHIFT, hw), lambda b, g: ((b + 1) * tail_blocks, groups + g)),
            pl.BlockSpec((CONV_SHIFT, hw), lambda b, g: ((b + 1) * tail_blocks, 2 * groups + g)),
            pl.BlockSpec((seq, LANES), lambda b, g: (b, 0)),
            pl.BlockSpec((npk, A_HEADS, PACK_ROWS), lambda b, g: (b, 0, 0)),
            pl.BlockSpec((npk, A_HEADS, PACK_ROWS), lambda b, g: (b, 1, 0)),
            pl.BlockSpec((seq, hw), lambda b, g: (b, g)),
            pl.BlockSpec((1, LANES), lambda b, g: (0, 0)),
        ],
        out_specs=pl.BlockSpec((seq, hw), lambda b, g: (b, g)),
        out_shape=jax.ShapeDtypeStruct((t, A_WIDTH), BF16),
        scratch_shapes=scratch,
        compiler_params=pltpu.CompilerParams(dimension_semantics=("arbitrary", "arbitrary"),
                                             vmem_limit_bytes=VMEM_LIMIT),
        name="deltanet",
    )(qkv, qkv, qkv, qkv, qkv, qkv, ab, abt3, abt3, ga, o_norm)


def _attn_kernel(sink_ref, qb_ref, kb_ref, vb_ref, gb_ref, qc_ref, km_ref, vm_ref, gc_ref, out_ref):
    n = pl.program_id(1)
    nb = pl.num_programs(1)
    w = WINDOW
    lane = lax.broadcasted_iota(jnp.int32, (w, LANES), 1)
    lower = lane < B_HEAD_DIM
    ri = lax.broadcasted_iota(jnp.int32, (2 * w, w), 0) % w
    ci = lax.broadcasted_iota(jnp.int32, (2 * w, w), 1)
    top = lax.broadcasted_iota(jnp.int32, (2 * w, 1), 0) < w

    blocks = []
    for rel in (-1, 0, 1):
        j = n + rel
        start = pl.multiple_of(jnp.clip(j, 0, nb - 1) * w, w)
        kblk = kb_ref[pl.ds(start, w), :].astype(F32)
        vblk = vb_ref[pl.ds(start, w), :].astype(F32)
        valid = jnp.logical_and(j >= 0, j < nb)
        if rel == -1:
            mask = jnp.logical_and(ci >= ri, valid)
        elif rel == 0:
            mask = None
        else:
            mask = jnp.logical_and(ci <= ri, valid)
        blocks.append((kblk, pltpu.roll(kblk, B_HEAD_DIM, axis=1), vblk, pltpu.roll(vblk, B_HEAD_DIM, axis=1), mask))

    scale = B_HEAD_DIM ** -0.5
    for hk in range(B_KV_HEADS):
        c0 = hk * 2 * LANES
        q2 = jnp.concatenate([qb_ref[:, c0:c0 + LANES], qb_ref[:, c0 + LANES:c0 + 2 * LANES]], axis=0)
        acc = jnp.zeros((2 * w, LANES), F32)
        for half in range(2):
            ks_, vs_ = [], []
            for kblk, ksw, vblk, vsw, _ in blocks:
                in_place = (hk == half)
                keep = lower if half == 0 else jnp.logical_not(lower)
                ks_.append(jnp.where(keep, kblk if in_place else ksw, 0.0).astype(BF16))
                vs_.append(jnp.where(keep, vblk if in_place else vsw, 0.0).astype(BF16))
            sink_col = jnp.where(top, sink_ref[4 * hk + half], sink_ref[4 * hk + 2 + half])
            ss = []
            for (_, _, _, _, mask), kk in zip(blocks, ks_):
                s = _dot_nt(q2, kk) * scale
                if mask is not None:
                    s = jnp.where(mask, s, -jnp.inf)
                ss.append(s)
            m = jnp.maximum(sink_col, jnp.max(jnp.maximum(jnp.maximum(ss[0], ss[1]), ss[2]),
                                              axis=-1, keepdims=True))
            ps = [jnp.exp(s - m) for s in ss]
            den = jnp.exp(sink_col - m) + jnp.sum(ps[0] + ps[1] + ps[2], axis=-1, keepdims=True)
            pv = jnp.zeros((2 * w, LANES), F32)
            for p, vv in zip(ps, vs_):
                pv = pv + _dot(p.astype(BF16), vv)
            acc = acc + pv / den
        out_ref[:, c0:c0 + LANES] = (acc[:w] * gb_ref[:, c0:c0 + LANES].astype(F32)).astype(BF16)
        out_ref[:, c0 + LANES:c0 + 2 * LANES] = (acc[w:] * gb_ref[:, c0 + LANES:c0 + 2 * LANES].astype(F32)).astype(BF16)

    cscale = C_HEAD_DIM ** -0.5
    for hc in range(C_HEADS):
        c0 = hc * C_HEAD_DIM
        s = _dot_nt(qc_ref[:, c0:c0 + C_HEAD_DIM], km_ref[:, c0:c0 + C_HEAD_DIM]) * cscale
        m = jnp.max(s, axis=-1, keepdims=True)
        p = jnp.exp(s - m)
        den = jnp.sum(p, axis=-1, keepdims=True)
        o = _dot(p.astype(BF16), vm_ref[:, c0:c0 + C_HEAD_DIM]) / den
        out_ref[:, B_WIDTH + c0:B_WIDTH + c0 + C_HEAD_DIM] = (
            o * gc_ref[:, c0:c0 + C_HEAD_DIM].astype(F32)).astype(BF16)


def _attention(sink, qb, kb, vb, gb, qc, km, vm, gcg, batch, seq):
    nb = seq // WINDOW
    t = batch * seq
    mlen = km.shape[0] // batch
    qrow = lambda b, n: (b * nb + n, 0)
    per_b = lambda b, n: (b, 0)
    return pl.pallas_call(
        _attn_kernel,
        grid=(batch, nb),
        in_specs=[
            pl.BlockSpec(memory_space=pltpu.SMEM),
            pl.BlockSpec((WINDOW, B_WIDTH), qrow),
            pl.BlockSpec((seq, LANES), per_b),
            pl.BlockSpec((seq, LANES), per_b),
            pl.BlockSpec((WINDOW, B_WIDTH), qrow),
            pl.BlockSpec((WINDOW, C_WIDTH), qrow),
            pl.BlockSpec((mlen, C_WIDTH), per_b),
            pl.BlockSpec((mlen, C_WIDTH), per_b),
            pl.BlockSpec((WINDOW, C_WIDTH), qrow),
        ],
        out_specs=pl.BlockSpec((WINDOW, B_WIDTH + C_WIDTH), qrow),
        out_shape=jax.ShapeDtypeStruct((t, B_WIDTH + C_WIDTH), BF16),
        compiler_params=pltpu.CompilerParams(dimension_semantics=("arbitrary", "arbitrary"),
                                             vmem_limit_bytes=VMEM_LIMIT),
        name="attention",
    )(sink, qb, kb, vb, gb, qc, km, vm, gcg)


def _out_proj_kernel(x_ref, ma_ref, mbc_ref, wa_ref, wbc_ref, out_ref):
    out_ref[...] = x_ref[...] + _dot(ma_ref[...], wa_ref[...]) + _dot(mbc_ref[...], wbc_ref[...])


def _out_proj(x2, mixed_a, mixed_bc, w_out, tm=256):
    t = x2.shape[0]
    row = lambda i: (i, 0)
    const = lambda i: (0, 0)
    return pl.pallas_call(
        _out_proj_kernel,
        grid=(t // tm,),
        in_specs=[
            pl.BlockSpec((tm, D_MODEL), row),
            pl.BlockSpec((tm, A_WIDTH), row),
            pl.BlockSpec((tm, B_WIDTH + C_WIDTH), row),
            pl.BlockSpec((A_WIDTH, D_MODEL), const),
            pl.BlockSpec((B_WIDTH + C_WIDTH, D_MODEL), lambda i: (1, 0)),
        ],
        out_specs=pl.BlockSpec((tm, D_MODEL), row),
        out_shape=jax.ShapeDtypeStruct((t, D_MODEL), F32),
        compiler_params=pltpu.CompilerParams(dimension_semantics=("arbitrary",), vmem_limit_bytes=VMEM_LIMIT),
        name="out_proj",
    )(x2, mixed_a, mixed_bc, w_out, w_out)


def _layer(h, mem, norm_w, w_in, conv_w_a, a_log_fwd, a_log_bwd, dt_bias_fwd, dt_bias_bwd, o_norm_a,
           q_norm_b, k_norm_b, sink_b, mem_norm_w, w_mem_kv, q_norm_c, k_norm_c, w_out):
    batch, seq, _ = h.shape
    t = batch * seq
    x2 = h.reshape(t, D_MODEL)

    w_a = w_in[:, :A_COLS].astype(BF16)
    w_g = jnp.pad(w_in[:, A_COLS:A_COLS + GATE_COLS], ((0, 0), (0, LANES - GATE_COLS))).astype(BF16)
    w_t = w_in[:, A_COLS + GATE_COLS:].astype(BF16)
    zeros = jnp.zeros((LANES - 2 * A_HEADS,), F32)
    alog_row = jnp.concatenate([a_log_fwd, a_log_bwd, zeros]).reshape(1, LANES)
    dtb_row = jnp.concatenate([dt_bias_fwd, dt_bias_bwd, zeros]).reshape(1, LANES)
    cos, sin = _rope_tables(seq)

    qkv, ga, ab, abt, qb, kb, vb, gb, qc, gcg = _in_proj(
        x2, norm_w.reshape(1, D_MODEL), w_a, w_g, w_t, conv_w_a, cos, sin, alog_row, dtb_row,
        jnp.tile(q_norm_b, B_Q_HEADS).reshape(1, B_WIDTH),
        jnp.tile(k_norm_b, B_KV_HEADS).reshape(1, LANES),
        jnp.tile(q_norm_c, C_HEADS).reshape(1, C_WIDTH), seq)

    mlen = mem.shape[1]
    km, vm = _mem_proj(mem.reshape(batch * mlen, D_MODEL), mem_norm_w.reshape(1, D_MODEL),
                       w_mem_kv.astype(BF16), jnp.tile(k_norm_c, C_HEADS).reshape(1, C_WIDTH))

    mixed_a = _deltanet(qkv, ab, abt, ga,
                        o_norm_a.reshape(1, A_HEAD_DIM), batch, seq)
    mixed_bc = _attention(sink_b, qb, kb, vb, gb, qc, km, vm, gcg, batch, seq)

    out = _out_proj(x2, mixed_a, mixed_bc, w_out.astype(BF16))
    return out.reshape(batch, seq, D_MODEL)


def kernel(x, mem, norm_w, w_in, conv_w_a, a_log_fwd, a_log_bwd, dt_bias_fwd, dt_bias_bwd, o_norm_a,
           q_norm_b, k_norm_b, sink_b, mem_norm_w, w_mem_kv, q_norm_c, k_norm_c, w_out):
    h = x
    for l in range(norm_w.shape[0]):
        h = _layer(h, mem, norm_w[l], w_in[l], conv_w_a[l], a_log_fwd[l], a_log_bwd[l], dt_bias_fwd[l],
                   dt_bias_bwd[l], o_norm_a[l], q_norm_b[l], k_norm_b[l], sink_b[l], mem_norm_w[l],
                   w_mem_kv[l], q_norm_c[l], k_norm_c[l], w_out[l])
    return h
```

```python
import functools

import numpy as np
import jax
import jax.numpy as jnp
from jax import lax
from jax.experimental import pallas as pl
from jax.experimental.pallas import tpu as pltpu

F32 = jnp.float32
BF16 = jnp.bfloat16

D_MODEL = 2048
A_WIDTH = 1024
A_HEAD_DIM = 128
A_HEADS = 8
CONV_K = 5
CHUNK = 64
B_WIDTH = 512
B_HEAD_DIM = 64
B_Q_HEADS = 8
B_KV_HEADS = 2
WINDOW = 128
C_WIDTH = 512
C_HEADS = 4
C_HEAD_DIM = 128
ROPE_THETA = 10000.0
EPS = 1e-6

LANES = 128
GATE_COLS = 4 * A_HEADS
A_COLS = 4 * A_WIDTH
T_QB = 0
T_KB = T_QB + B_WIDTH
T_VB = T_KB + B_KV_HEADS * B_HEAD_DIM
T_ZB = T_VB + B_KV_HEADS * B_HEAD_DIM
T_QC = T_ZB + B_WIDTH
T_ZC = T_QC + C_WIDTH
T_COLS = T_ZC + C_WIDTH

VMEM_LIMIT = 56 * 1024 * 1024
CONV_SHIFT = 16
CONV_HALO = 24
CONV_COLS = 256
CONV_ROWS = 128

_NT = (((1,), (1,)), ((), ()))
_TN = (((0,), (0,)), ((), ()))


def _dot(a, b):
    return jnp.dot(a, b, preferred_element_type=F32)


def _dot_nt(a, b):
    return lax.dot_general(a, b, _NT, preferred_element_type=F32)


def _silu(z):
    return z * jax.nn.sigmoid(z)


def _softplus(z):
    return jnp.maximum(z, 0.0) + jnp.log1p(jnp.exp(-jnp.abs(z)))


def _rope_partner(x):
    lane = lax.broadcasted_iota(jnp.int32, x.shape, 1)
    first_half = (lane % B_HEAD_DIM) < (B_HEAD_DIM // 2)
    up = pltpu.roll(x, LANES - B_HEAD_DIM // 2, axis=1)
    down = pltpu.roll(x, B_HEAD_DIM // 2, axis=1)
    return jnp.where(first_half, up, down)


def _head_rms(x, ones_blockdiag, head_dim, w):
    ss = _dot((x * x).astype(BF16), ones_blockdiag)
    return x * lax.rsqrt(ss * (1.0 / head_dim) + EPS) * w


def _in_proj_kernel(tiles_per_seq, x_ref, nw_ref, wa_ref, wg_ref, wt_ref, cw_ref, cos_ref, sin_ref, alog_ref,
                    dtb_ref, qnb_ref, knb_ref, qnc_ref, e64_ref, e128_ref,
                    qkv_ref, ga_ref, ab_ref, abt_ref, qb_ref, kb_ref, vb_ref, gb_ref, qc_ref, gc_ref,
                    ext_s):
    step_id = pl.program_id(0)

    @pl.when(step_id == 0)
    def _():
        ext_s[:, 0:CONV_HALO, :] = jnp.zeros((ext_s.shape[0], CONV_HALO, A_HEAD_DIM), F32)

    seq_start = (step_id % tiles_per_seq) == 0
    zero = jnp.minimum(step_id, 0)
    x = x_ref[...]
    ms = jnp.mean(x * x, axis=-1, keepdims=True)
    hn = (x * lax.rsqrt(ms + EPS) * nw_ref[...]).astype(BF16)
    tm = x.shape[0]

    def proj(w_ref, c0, width):
        return _dot(hn, w_ref[:, c0:c0 + width])

    fix_rows = lax.broadcasted_iota(jnp.int32, (2 * 8, 1), 0) + (CONV_SHIFT - 8)
    first = CONV_HALO - CONV_SHIFT - CONV_K // 2
    lo = CONV_SHIFT - 8

    def conv_head(c):
        lanes = slice(c, c + A_HEAD_DIM)
        head = c // A_HEAD_DIM
        w = cw_ref[:, lanes]
        taps = [w[j:j + 1, :] for j in range(CONV_K)]
        scale = A_HEAD_DIM ** -0.5 if c < A_WIDTH else 1.0
        for r in range(0, tm, CONV_ROWS):
            y = ext_s[head, pl.ds(zero + (first + r), CONV_ROWS), :] * taps[0]
            for j in range(1, CONV_K):
                y = y + ext_s[head, pl.ds(zero + (first + r + j), CONV_ROWS), :] * taps[j]
            if r == 0:
                cur0 = ext_s[head, CONV_HALO:CONV_HALO + 1, :]
                cur1 = ext_s[head, CONV_HALO + 1:CONV_HALO + 2, :]
                old0 = ext_s[head, CONV_HALO - 2:CONV_HALO - 1, :]
                old1 = ext_s[head, CONV_HALO - 1:CONV_HALO, :]
                cross = (jnp.where(fix_rows == CONV_SHIFT - 2, taps[4] * cur0, 0.0)
                         + jnp.where(fix_rows == CONV_SHIFT - 1, taps[3] * cur0 + taps[4] * cur1, 0.0)
                         + jnp.where(fix_rows == CONV_SHIFT, taps[0] * old0 + taps[1] * old1, 0.0)
                         + jnp.where(fix_rows == CONV_SHIFT + 1, taps[0] * old1, 0.0))
                fixed = y[lo:lo + 16] - jnp.where(seq_start, cross, 0.0)
                y = jnp.concatenate([y[:lo], fixed, y[lo + 16:]], axis=0)
            y = _silu(y)
            if c < 2 * A_WIDTH:
                y = y * (lax.rsqrt(jnp.sum(y * y, axis=-1, keepdims=True) + EPS) * scale)
            qkv_ref[r:r + CONV_ROWS, lanes] = y.astype(BF16)

    for c in range(0, 3 * A_WIDTH, CONV_COLS):
        acc = proj(wa_ref, c, CONV_COLS)
        for hc in range(0, CONV_COLS, A_HEAD_DIM):
            ext_s[(c + hc) // A_HEAD_DIM, CONV_HALO:CONV_HALO + tm, :] = acc[:, hc:hc + A_HEAD_DIM]
        for hc in range(c, c + CONV_COLS, A_HEAD_DIM):
            conv_head(hc)
    ext_s[:, 0:CONV_HALO, :] = ext_s[:, tm:tm + CONV_HALO, :]

    step = 512
    for c in range(0, A_WIDTH, step):
        ga_ref[:, c:c + step] = _silu(proj(wa_ref, 3 * A_WIDTH + c, step)).astype(BF16)

    acc = _dot(hn, wg_ref[...])
    lane = lax.broadcasted_iota(jnp.int32, (tm, LANES), 1)
    g = -jnp.exp(alog_ref[...]) * _softplus(acc + dtb_ref[...])
    val = jnp.where(lane < 2 * A_HEADS, g, jax.nn.sigmoid(acc))
    val = jnp.where(lane < GATE_COLS, val, 0.0)
    pos = lax.broadcasted_iota(jnp.int32, (tm, LANES), 0) % CHUNK
    pre = val
    suf = val
    s = 1
    while s < CHUNK:
        pre = pre + jnp.where(pos >= s, pltpu.roll(pre, s, axis=0), 0.0)
        suf = suf + jnp.where(pos < CHUNK - s, pltpu.roll(suf, tm - s, axis=0), 0.0)
        s *= 2
    res = jnp.where(lane < A_HEADS, pre, jnp.where(lane < 2 * A_HEADS, suf, val))
    ab_ref[...] = res
    abt_ref[0] = res.T

    cos = cos_ref[...]
    sin = sin_ref[...]
    qb = _head_rms(proj(wt_ref, T_QB, B_WIDTH), e64_ref[...], B_HEAD_DIM, qnb_ref[...])
    for c in range(0, B_WIDTH, LANES):
        t = qb[:, c:c + LANES]
        qb_ref[:, c:c + LANES] = (t * cos + _rope_partner(t) * sin).astype(BF16)
    kvb = proj(wt_ref, T_KB, 2 * LANES)
    kb = _head_rms(kvb[:, :LANES], e64_ref[:LANES, :LANES], B_HEAD_DIM, knb_ref[...])
    kb_ref[...] = (kb * cos + _rope_partner(kb) * sin).astype(BF16)
    vb_ref[...] = kvb[:, LANES:].astype(BF16)
    gb_ref[...] = _silu(proj(wt_ref, T_ZB, B_WIDTH)).astype(BF16)
    qc_ref[...] = _head_rms(proj(wt_ref, T_QC, C_WIDTH), e128_ref[...], C_HEAD_DIM, qnc_ref[...]).astype(BF16)
    gc_ref[...] = _silu(proj(wt_ref, T_ZC, C_WIDTH)).astype(BF16)


def _block_diag_ones(width, block):
    idx = np.arange(width) // block
    return jnp.asarray(idx[:, None] == idx[None, :], dtype=BF16)


def _rope_tables(seq):
    d = B_HEAD_DIM
    inv = ROPE_THETA ** (-jnp.arange(0, d, 2, dtype=F32) / d)
    ang = jnp.arange(seq, dtype=F32)[:, None] * inv[None, :]
    cos = jnp.cos(ang)
    sin = jnp.sin(ang)
    cos_h = jnp.concatenate([cos, cos], axis=-1)
    sin_h = jnp.concatenate([-sin, sin], axis=-1)
    reps = LANES // d
    return jnp.tile(cos_h, (1, reps)), jnp.tile(sin_h, (1, reps))


def _in_proj(x2, norm_w, w_a, w_g, w_t, conv_w, cos, sin, alog_row, dtb_row, qnb, knb, qnc, seq, tm=256):
    t = x2.shape[0]
    tiles_per_seq = seq // tm
    nt = t // tm
    row = lambda i: (jnp.minimum(i, nt - 1), 0)
    const = lambda i: (0, 0)
    pos = lambda i: (jnp.minimum(i, nt - 1) % tiles_per_seq, 0)
    in_specs = [
        pl.BlockSpec((tm, D_MODEL), row),
        pl.BlockSpec((1, D_MODEL), const),
        pl.BlockSpec((D_MODEL, A_COLS), const, pipeline_mode=pl.Buffered(1)),
        pl.BlockSpec((D_MODEL, LANES), const, pipeline_mode=pl.Buffered(1)),
        pl.BlockSpec((D_MODEL, T_COLS), const, pipeline_mode=pl.Buffered(1)),
        pl.BlockSpec((CONV_K, 3 * A_WIDTH), const),
        pl.BlockSpec((tm, LANES), pos),
        pl.BlockSpec((tm, LANES), pos),
        pl.BlockSpec((1, LANES), const),
        pl.BlockSpec((1, LANES), const),
        pl.BlockSpec((1, B_WIDTH), const),
        pl.BlockSpec((1, LANES), const),
        pl.BlockSpec((1, C_WIDTH), const),
        pl.BlockSpec((B_WIDTH, B_WIDTH), const),
        pl.BlockSpec((C_WIDTH, C_WIDTH), const),
    ]
    out_shape = [
        jax.ShapeDtypeStruct((t + tm, 3 * A_WIDTH), BF16),
        jax.ShapeDtypeStruct((t, A_WIDTH), BF16),
        jax.ShapeDtypeStruct((t, LANES), F32),
        jax.ShapeDtypeStruct((t // tm, LANES, tm), F32),
        jax.ShapeDtypeStruct((t, B_WIDTH), BF16),
        jax.ShapeDtypeStruct((t, LANES), BF16),
        jax.ShapeDtypeStruct((t, LANES), BF16),
        jax.ShapeDtypeStruct((t, B_WIDTH), BF16),
        jax.ShapeDtypeStruct((t, C_WIDTH), BF16),
        jax.ShapeDtypeStruct((t, C_WIDTH), BF16),
    ]
    out_specs = [
        pl.BlockSpec((tm, 3 * A_WIDTH), lambda i: (i, 0)),
        pl.BlockSpec((tm, A_WIDTH), row),
        pl.BlockSpec((tm, LANES), row),
        pl.BlockSpec((1, LANES, tm), lambda i: (jnp.minimum(i, nt - 1), 0, 0)),
        pl.BlockSpec((tm, B_WIDTH), row),
        pl.BlockSpec((tm, LANES), row),
        pl.BlockSpec((tm, LANES), row),
        pl.BlockSpec((tm, B_WIDTH), row),
        pl.BlockSpec((tm, C_WIDTH), row),
        pl.BlockSpec((tm, C_WIDTH), row),
    ]
    return pl.pallas_call(
        functools.partial(_in_proj_kernel, tiles_per_seq),
        grid=(nt + 1,),
        in_specs=in_specs,
        out_specs=out_specs,
        out_shape=out_shape,
        scratch_shapes=[pltpu.VMEM((3 * A_HEADS, CONV_HALO + tm, A_HEAD_DIM), F32)],
        compiler_params=pltpu.CompilerParams(dimension_semantics=("arbitrary",), vmem_limit_bytes=VMEM_LIMIT),
        name="in_proj",
    )(x2, norm_w, w_a, w_g, w_t, conv_w, cos, sin, alog_row, dtb_row, qnb, knb, qnc,
      _block_diag_ones(B_WIDTH, B_HEAD_DIM), _block_diag_ones(C_WIDTH, C_HEAD_DIM))


def _mem_proj_kernel(m_ref, nw_ref, w_ref, knc_ref, e128_ref, km_ref, vm_ref):
    x = m_ref[...]
    ms = jnp.mean(x * x, axis=-1, keepdims=True)
    mn = (x * lax.rsqrt(ms + EPS) * nw_ref[...]).astype(BF16)
    km = _dot(mn, w_ref[:, :C_WIDTH])
    km_ref[...] = _head_rms(km, e128_ref[...], C_HEAD_DIM, knc_ref[...]).astype(BF16)
    vm_ref[...] = _dot(mn, w_ref[:, C_WIDTH:]).astype(BF16)


def _mem_proj(mem2, mem_norm_w, w_kv, knc, tm=256):
    t = mem2.shape[0]
    row = lambda i: (i, 0)
    const = lambda i: (0, 0)
    return pl.pallas_call(
        _mem_proj_kernel,
        grid=(t // tm,),
        in_specs=[
            pl.BlockSpec((tm, D_MODEL), row),
            pl.BlockSpec((1, D_MODEL), const),
            pl.BlockSpec((D_MODEL, 2 * C_WIDTH), const),
            pl.BlockSpec((1, C_WIDTH), const),
            pl.BlockSpec((C_WIDTH, C_WIDTH), const),
        ],
        out_specs=[pl.BlockSpec((tm, C_WIDTH), row), pl.BlockSpec((tm, C_WIDTH), row)],
        out_shape=[jax.ShapeDtypeStruct((t, C_WIDTH), BF16), jax.ShapeDtypeStruct((t, C_WIDTH), BF16)],
        compiler_params=pltpu.CompilerParams(dimension_semantics=("arbitrary",), vmem_limit_bytes=VMEM_LIMIT),
        name="mem_proj",
    )(mem2, mem_norm_w, w_kv, knc, _block_diag_ones(C_WIDTH, C_HEAD_DIM))


HEADS_PER_STEP = 2
PACK = 4
PACK_ROWS = PACK * CHUNK
SQUARINGS = 5
PREP_PACKS = 2


def _deltanet_kernel(q_ref, k_ref, v_ref, qt_ref, kt_ref, vt_ref, ab_ref, gfr_ref, gbr_ref, ga_ref, on_ref,
                     out_ref,
                     qs, ks, vs, kq_s, b_s, cd_s, o_s):
    seq = q_ref.shape[0]
    nc = seq // CHUNK
    hb = HEADS_PER_STEP
    h0 = pl.program_id(1) * hb
    slab = 256

    def slab_rows(i):
        return pl.ds(pl.multiple_of(i * slab, slab), slab)

    def stage(src_ref, tail_ref, dst_ref, hs):
        cols = slice(hs * LANES, (hs + 1) * LANES)

        def copy(i, carry):
            src_rows = pl.ds(pl.multiple_of(CONV_SHIFT + i * slab, CONV_SHIFT), slab)
            dst_ref[hs, slab_rows(i), :] = src_ref[src_rows, cols].astype(F32)
            return carry

        lax.fori_loop(0, seq // slab - 1, copy, 0)
        last = seq - slab
        dst_ref[hs, last:seq - CONV_SHIFT, :] = src_ref[last + CONV_SHIFT:seq, cols].astype(F32)
        dst_ref[hs, seq - CONV_SHIFT:seq, :] = tail_ref[:, cols].astype(F32)

    for hs in range(hb):
        stage(q_ref, qt_ref, qs, hs)
        stage(k_ref, kt_ref, ks, hs)
        stage(v_ref, vt_ref, vs, hs)

    ii = lax.broadcasted_iota(jnp.int32, (CHUNK, PACK_ROWS), 0)
    jl = lax.broadcasted_iota(jnp.int32, (CHUNK, PACK_ROWS), 1)
    lb = jl // CHUNK
    jj = jl % CHUNK
    block_diag = (lax.broadcasted_iota(jnp.int32, (PACK_ROWS, PACK_ROWS), 0) // CHUNK
                  == lax.broadcasted_iota(jnp.int32, (PACK_ROWS, PACK_ROWS), 1) // CHUNK)

    def pack_diag(full):
        out = full[0:CHUNK]
        for c in range(1, PACK):
            out = jnp.where(lb == c, full[c * CHUNK:(c + 1) * CHUNK], out)
        return out

    def pack_cols(col):
        out = None
        for c in range(PACK):
            blk = col[c * CHUNK:(c + 1) * CHUNK]
            blk = jnp.concatenate([blk, blk], axis=1)
            out = blk if out is None else jnp.where(lb == c, blk, out)
        return out

    def to_block_diag(packed):
        return jnp.where(block_diag, jnp.concatenate([packed] * PACK, axis=0), 0.0).astype(BF16)

    lane4 = lax.broadcasted_iota(jnp.int32, (PACK_ROWS, LANES), 1)
    npk = seq // PACK_ROWS
    chains = [(hs, d) for hs in range(hb) for d in range(2)]

    def prep_gen(it):
        members = []
        for pk in range(PREP_PACKS):
            for d in range(2):
                fwd = d == 0
                p = it * PREP_PACKS + pk
                if not fwd:
                    p = npk - 1 - p
                rows = pl.ds(pl.multiple_of(p * PACK_ROWS, PACK_ROWS), PACK_ROWS)
                ab4 = ab_ref[rows, :]

                def gate_col(col, ab4=ab4):
                    c = jnp.sum(jnp.where(lane4 == col, ab4, 0.0), axis=-1, keepdims=True)
                    return jnp.broadcast_to(c, (PACK_ROWS, LANES))

                incl = (ii >= jj) if fwd else (ii <= jj)
                strict = (ii > jj) if fwd else (ii < jj)
                for hs in range(hb):
                    k4 = ks[hs, rows, :]
                    q4 = qs[hs, rows, :]
                    v4 = vs[hs, rows, :]
                    k4b = k4.astype(BF16)
                    kk_p = pack_diag(_dot_nt(k4b, k4b))
                    qk_p = pack_diag(_dot_nt(q4.astype(BF16), k4b))
                    gc4 = gate_col(d * A_HEADS + h0 + hs)
                    bt4 = gate_col((2 + d) * A_HEADS + h0 + hs)
                    gcr = (gfr_ref if fwd else gbr_ref)[p, pl.ds(h0 + hs, 1), :]
                    decay = jnp.where(incl, jnp.exp(jnp.where(incl, pack_cols(gc4) - gcr, 0.0)), 0.0)
                    n_cat = jnp.where(strict, -(kk_p * decay * pack_cols(bt4)), 0.0)
                    qkm = jnp.where(incl, qk_p * decay, 0.0)
                    members.append((hs, d, k4, q4, v4, gc4, bt4, n_cat, qkm, p, rows))
        yield

        eye = jnp.where(ii == jj, 1.0, 0.0)
        tinv = [eye + m[7] for m in members]
        pw = [_dot(m[7].astype(BF16), to_block_diag(m[7])) for m in members]
        yield
        for level in range(1, SQUARINGS + 1):
            last = level == SQUARINGS
            for i in range(len(members)):
                rhs = to_block_diag(pw[i])
                if last:
                    tinv[i] = tinv[i] + _dot(tinv[i].astype(BF16), rhs)
                else:
                    both = _dot(jnp.concatenate([pw[i], tinv[i]], axis=0).astype(BF16), rhs)
                    pw[i] = both[:CHUNK]
                    tinv[i] = tinv[i] + both[CHUNK:]
            yield

        uws = []
        for (hs, d, k4, q4, v4, gc4, bt4, _, qkm, p, rows), t_p in zip(members, tinv):
            eg = jnp.exp(gc4)
            rhs = jnp.concatenate([v4 * bt4, k4 * bt4 * eg], axis=1).astype(BF16)
            uws.append((_dot(to_block_diag(t_p), rhs), eg))
        yield

        folded = []
        for (hs, d, k4, q4, v4, gc4, bt4, _, qkm, p, rows), (uw, eg) in zip(members, uws):
            fwd = d == 0
            uwb = uw.astype(BF16)
            per_chunk = []
            for c in range(PACK):
                rs = slice(c * CHUNK, (c + 1) * CHUNK)
                last_row = (c + 1) * CHUNK - 1 if fwd else c * CHUNK
                gl = gc4[last_row:last_row + 1, :]
                kd = k4[rs] * jnp.exp(gl - gc4[rs])
                lhs = jnp.concatenate([qkm[:, c * CHUNK:(c + 1) * CHUNK], kd.T], axis=0).astype(BF16)
                per_chunk.append((_dot(lhs, uwb[rs]), gl))
            folded.append(per_chunk)
        yield

        for (hs, d, k4, q4, v4, gc4, bt4, _, qkm, p, rows), (uw, eg), per_chunk in zip(members, uws, folded):
            qd = q4 * eg
            for c, (res, gl) in enumerate(per_chunk):
                n = p * PACK + c
                rs = slice(c * CHUNK, (c + 1) * CHUNK)
                o_s[hs, d, pl.ds(pl.multiple_of(n * CHUNK, CHUNK), CHUNK), :] = res[:CHUNK, :LANES]
                kq_s[hs, d, n, 0:LANES, :] = res[CHUNK:, LANES:].astype(BF16)
                kq_s[hs, d, n, LANES:LANES + CHUNK, :] = (qd[rs] - res[:CHUNK, LANES:]).astype(BF16)
                b_s[hs, d, n] = res[CHUNK:, :LANES]
                cd_s[hs, d, n] = jnp.broadcast_to(jnp.exp(gl), (8, LANES))

    steps_per_stage = PREP_PACKS * PACK

    def seq_gen(it, states, result):
        states = list(states)
        for k in range(steps_per_stage):
            i = it * steps_per_stage + k
            ns = (i, nc - 1 - i)
            rows = [pl.ds(pl.multiple_of(n * CHUNK, CHUNK), CHUNK) for n in ns]
            outs = [_dot(kq_s[hs, d, ns[d]], st.astype(BF16)) for (hs, d), st in zip(chains, states)]
            for c, ((hs, d), out) in enumerate(zip(chains, outs)):
                o_s[hs, d, rows[d], :] = o_s[hs, d, rows[d], :] + out[LANES:]
                states[c] = states[c] * cd_s[hs, d, ns[d]][0:1, :] - out[:LANES] + b_s[hs, d, ns[d]]
            yield
        result.append(tuple(states))

    def run(gen):
        for _ in gen:
            pass

    def fused_body(j, states):
        result = []
        pg = prep_gen(j + 1)
        sg = seq_gen(j, states, result)
        next(pg)
        for _ in range(steps_per_stage):
            next(pg)
            next(sg)
        run(sg)
        run(pg)
        return result[0]

    n_stages = npk // PREP_PACKS
    s0 = jnp.zeros((A_HEAD_DIM, A_HEAD_DIM), F32)
    run(prep_gen(0))
    states = lax.fori_loop(0, n_stages - 1, fused_body, tuple(s0 for _ in chains))
    run(seq_gen(n_stages - 1, states, []))

    def finish(i, carry):
        for hs in range(hb):
            cols = slice(hs * LANES, (hs + 1) * LANES)
            o = o_s[hs, 0, slab_rows(i), :] + o_s[hs, 1, slab_rows(i), :]
            y = o * lax.rsqrt(jnp.mean(o * o, axis=-1, keepdims=True) + EPS) * on_ref[...]
            out_ref[slab_rows(i), cols] = (y * ga_ref[slab_rows(i), cols].astype(F32)).astype(BF16)
        return carry

    lax.fori_loop(0, seq // slab, finish, 0)


def _deltanet(qkv, ab, abt3, ga, o_norm, batch, seq):
    nc = seq // CHUNK
    npk = seq // PACK_ROWS
    t = batch * seq
    hb = HEADS_PER_STEP
    hw = hb * LANES
    groups = A_HEADS // hb
    tail_blocks = seq // CONV_SHIFT
    scratch = [
        pltpu.VMEM((hb, seq, LANES), F32),
        pltpu.VMEM((hb, seq, LANES), F32),
        pltpu.VMEM((hb, seq, LANES), F32),
        pltpu.VMEM((hb, 2, nc, LANES + CHUNK, LANES), BF16),
        pltpu.VMEM((hb, 2, nc, LANES, LANES), F32),
        pltpu.VMEM((hb, 2, nc, 8, LANES), F32),
        pltpu.VMEM((hb, 2, seq, LANES), F32),
    ]
    return pl.pallas_call(
        _deltanet_kernel,
        grid=(batch, groups),
        in_specs=[
            pl.BlockSpec((seq, hw), lambda b, g: (b, g)),
            pl.BlockSpec((seq, hw), lambda b, g: (b, groups + g)),
            pl.BlockSpec((seq, hw), lambda b, g: (b, 2 * groups + g)),
            pl.BlockSpec((CONV_SHIFT, hw), lambda b, g: ((b + 1) * tail_blocks, g)),
            pl.BlockSpec((CONV_SHIFT, hw), lambda b, g: ((b + 1) * tail_blocks, groups + g)),
            pl.BlockSpec((CONV_SHIFT, hw), lambda b, g: ((b + 1) * tail_blocks, 2 * groups + g)),
            pl.BlockSpec((seq, LANES), lambda b, g: (b, 0)),
            pl.BlockSpec((npk, A_HEADS, PACK_ROWS), lambda b, g: (b, 0, 0)),
            pl.BlockSpec((npk, A_HEADS, PACK_ROWS), lambda b, g: (b, 1, 0)),
            pl.BlockSpec((seq, hw), lambda b, g: (b, g)),
            pl.BlockSpec((1, LANES), lambda b, g: (0, 0)),
        ],
        out_specs=pl.BlockSpec((seq, hw), lambda b, g: (b, g)),
        out_shape=jax.ShapeDtypeStruct((t, A_WIDTH), BF16),
        scratch_shapes=scratch,
        compiler_params=pltpu.CompilerParams(dimension_semantics=("arbitrary", "arbitrary"),
                                             vmem_limit_bytes=VMEM_LIMIT),
        name="deltanet",
    )(qkv, qkv, qkv, qkv, qkv, qkv, ab, abt3, abt3, ga, o_norm)


def _attn_kernel(sink_ref, qb_ref, kb_ref, vb_ref, gb_ref, qc_ref, km_ref, vm_ref, gc_ref, out_ref):
    n = pl.program_id(1)
    nb = pl.num_programs(1)
    w = WINDOW
    lane = lax.broadcasted_iota(jnp.int32, (w, LANES), 1)
    lower = lane < B_HEAD_DIM
    ri = lax.broadcasted_iota(jnp.int32, (2 * w, w), 0) % w
    ci = lax.broadcasted_iota(jnp.int32, (2 * w, w), 1)
    top = lax.broadcasted_iota(jnp.int32, (2 * w, 1), 0) < w

    blocks = []
    for rel in (-1, 0, 1):
        j = n + rel
        start = pl.multiple_of(jnp.clip(j, 0, nb - 1) * w, w)
        kblk = kb_ref[pl.ds(start, w), :].astype(F32)
        vblk = vb_ref[pl.ds(start, w), :].astype(F32)
        valid = jnp.logical_and(j >= 0, j < nb)
        if rel == -1:
            mask = jnp.logical_and(ci >= ri, valid)
        elif rel == 0:
            mask = None
        else:
            mask = jnp.logical_and(ci <= ri, valid)
        blocks.append((kblk, pltpu.roll(kblk, B_HEAD_DIM, axis=1), vblk, pltpu.roll(vblk, B_HEAD_DIM, axis=1), mask))

    for hk in range(B_KV_HEADS):
        c0 = hk * 2 * LANES
        q2 = jnp.concatenate([qb_ref[:, c0:c0 + LANES], qb_ref[:, c0 + LANES:c0 + 2 * LANES]], axis=0)
        acc = jnp.zeros((2 * w, LANES), F32)
        for half in range(2):
            ks_, vs_ = [], []
            for kblk, ksw, vblk, vsw, _ in blocks:
                in_place = (hk == half)
                keep = lower if half == 0 else jnp.logical_not(lower)
                ks_.append(jnp.where(keep, kblk if in_place else ksw, 0.0).astype(BF16))
                vs_.append(jnp.where(keep, vblk if in_place else vsw, 0.0).astype(BF16))
            sink_col = jnp.where(top, sink_ref[4 * hk + half], sink_ref[4 * hk + 2 + half])
            ss = []
            for (_, _, _, _, mask), kk in zip(blocks, ks_):
                s = _dot_nt(q2, kk)
                if mask is not None:
                    s = jnp.where(mask, s, -jnp.inf)
                ss.append(s)
            m = jnp.maximum(sink_col, jnp.max(jnp.maximum(jnp.maximum(ss[0], ss[1]), ss[2]),
                                              axis=-1, keepdims=True))
            ps = [jnp.exp(s - m) for s in ss]
            den = jnp.exp(sink_col - m) + jnp.sum(ps[0] + ps[1] + ps[2], axis=-1, keepdims=True)
            pv = jnp.zeros((2 * w, LANES), F32)
            for p, vv in zip(ps, vs_):
                pv = pv + _dot(p.astype(BF16), vv)
            acc = acc + pv / den
        out_ref[:, c0:c0 + LANES] = (acc[:w] * gb_ref[:, c0:c0 + LANES].astype(F32)).astype(BF16)
        out_ref[:, c0 + LANES:c0 + 2 * LANES] = (acc[w:] * gb_ref[:, c0 + LANES:c0 + 2 * LANES].astype(F32)).astype(BF16)

    for hc in range(C_HEADS):
        c0 = hc * C_HEAD_DIM
        s = _dot_nt(qc_ref[:, c0:c0 + C_HEAD_DIM], km_ref[:, c0:c0 + C_HEAD_DIM])
        m = jnp.max(s, axis=-1, keepdims=True)
        p = jnp.exp(s - m)
        den = jnp.sum(p, axis=-1, keepdims=True)
        o = _dot(p.astype(BF16), vm_ref[:, c0:c0 + C_HEAD_DIM]) / den
        out_ref[:, B_WIDTH + c0:B_WIDTH + c0 + C_HEAD_DIM] = (
            o * gc_ref[:, c0:c0 + C_HEAD_DIM].astype(F32)).astype(BF16)


def _attention(sink, qb, kb, vb, gb, qc, km, vm, gcg, batch, seq):
    nb = seq // WINDOW
    t = batch * seq
    mlen = km.shape[0] // batch
    qrow = lambda b, n: (b * nb + n, 0)
    per_b = lambda b, n: (b, 0)
    return pl.pallas_call(
        _attn_kernel,
        grid=(batch, nb),
        in_specs=[
            pl.BlockSpec(memory_space=pltpu.SMEM),
            pl.BlockSpec((WINDOW, B_WIDTH), qrow),
            pl.BlockSpec((seq, LANES), per_b),
            pl.BlockSpec((seq, LANES), per_b),
            pl.BlockSpec((WINDOW, B_WIDTH), qrow),
            pl.BlockSpec((WINDOW, C_WIDTH), qrow),
            pl.BlockSpec((mlen, C_WIDTH), per_b),
            pl.BlockSpec((mlen, C_WIDTH), per_b),
            pl.BlockSpec((WINDOW, C_WIDTH), qrow),
        ],
        out_specs=pl.BlockSpec((WINDOW, B_WIDTH + C_WIDTH), qrow),
        out_shape=jax.ShapeDtypeStruct((t, B_WIDTH + C_WIDTH), BF16),
        compiler_params=pltpu.CompilerParams(dimension_semantics=("arbitrary", "arbitrary"),
                                             vmem_limit_bytes=VMEM_LIMIT),
        name="attention",
    )(sink, qb, kb, vb, gb, qc, km, vm, gcg)


def _out_proj_kernel(x_ref, ma_ref, mbc_ref, wa_ref, wbc_ref, out_ref):
    out_ref[...] = x_ref[...] + _dot(ma_ref[...], wa_ref[...]) + _dot(mbc_ref[...], wbc_ref[...])


def _out_proj(x2, mixed_a, mixed_bc, w_out, tm=256):
    t = x2.shape[0]
    row = lambda i: (i, 0)
    const = lambda i: (0, 0)
    return pl.pallas_call(
        _out_proj_kernel,
        grid=(t // tm,),
        in_specs=[
            pl.BlockSpec((tm, D_MODEL), row),
            pl.BlockSpec((tm, A_WIDTH), row),
            pl.BlockSpec((tm, B_WIDTH + C_WIDTH), row),
            pl.BlockSpec((A_WIDTH, D_MODEL), const),
            pl.BlockSpec((B_WIDTH + C_WIDTH, D_MODEL), lambda i: (1, 0)),
        ],
        out_specs=pl.BlockSpec((tm, D_MODEL), row),
        out_shape=jax.ShapeDtypeStruct((t, D_MODEL), F32),
        compiler_params=pltpu.CompilerParams(dimension_semantics=("arbitrary",), vmem_limit_bytes=VMEM_LIMIT),
        name="out_proj",
    )(x2, mixed_a, mixed_bc, w_out, w_out)


def _layer(h, mem, norm_w, w_in, conv_w_a, a_log_fwd, a_log_bwd, dt_bias_fwd, dt_bias_bwd, o_norm_a,
           q_norm_b, k_norm_b, sink_b, mem_norm_w, w_mem_kv, q_norm_c, k_norm_c, w_out):
    batch, seq, _ = h.shape
    t = batch * seq
    x2 = h.reshape(t, D_MODEL)

    w_a = w_in[:, :A_COLS].astype(BF16)
    w_g = jnp.pad(w_in[:, A_COLS:A_COLS + GATE_COLS], ((0, 0), (0, LANES - GATE_COLS))).astype(BF16)
    w_t = w_in[:, A_COLS + GATE_COLS:].astype(BF16)
    zeros = jnp.zeros((LANES - 2 * A_HEADS,), F32)
    alog_row = jnp.concatenate([a_log_fwd, a_log_bwd, zeros]).reshape(1, LANES)
    dtb_row = jnp.concatenate([dt_bias_fwd, dt_bias_bwd, zeros]).reshape(1, LANES)
    cos, sin = _rope_tables(seq)

    qkv, ga, ab, abt, qb, kb, vb, gb, qc, gcg = _in_proj(
        x2, norm_w.reshape(1, D_MODEL), w_a, w_g, w_t, conv_w_a, cos, sin, alog_row, dtb_row,
        (jnp.tile(q_norm_b, B_Q_HEADS) * B_HEAD_DIM ** -0.5).reshape(1, B_WIDTH),
        jnp.tile(k_norm_b, B_KV_HEADS).reshape(1, LANES),
        (jnp.tile(q_norm_c, C_HEADS) * C_HEAD_DIM ** -0.5).reshape(1, C_WIDTH), seq)

    mlen = mem.shape[1]
    km, vm = _mem_proj(mem.reshape(batch * mlen, D_MODEL), mem_norm_w.reshape(1, D_MODEL),
                       w_mem_kv.astype(BF16), jnp.tile(k_norm_c, C_HEADS).reshape(1, C_WIDTH))

    mixed_a = _deltanet(qkv, ab, abt, ga,
                        o_norm_a.reshape(1, A_HEAD_DIM), batch, seq)
    mixed_bc = _attention(sink_b, qb, kb, vb, gb, qc, km, vm, gcg, batch, seq)

    out = _out_proj(x2, mixed_a, mixed_bc, w_out.astype(BF16))
    return out.reshape(batch, seq, D_MODEL)


def kernel(x, mem, norm_w, w_in, conv_w_a, a_log_fwd, a_log_bwd, dt_bias_fwd, dt_bias_bwd, o_norm_a,
           q_norm_b, k_norm_b, sink_b, mem_norm_w, w_mem_kv, q_norm_c, k_norm_c, w_out):
    h = x
    for l in range(norm_w.shape[0]):
        h = _layer(h, mem, norm_w[l], w_in[l], conv_w_a[l], a_log_fwd[l], a_log_bwd[l], dt_bias_fwd[l],
                   dt_bias_bwd[l], o_norm_a[l], q_norm_b[l], k_norm_b[l], sink_b[l], mem_norm_w[l],
                   w_mem_kv[l], q_norm_c[l], k_norm_c[l], w_out[l])
    return h
```

```python
import functools

import numpy as np
import jax
import jax.numpy as jnp
from jax import lax
from jax.experimental import pallas as pl
from jax.experimental.pallas import tpu as pltpu

F32 = jnp.float32
BF16 = jnp.bfloat16

D_MODEL = 2048
A_WIDTH = 1024
A_HEAD_DIM = 128
A_HEADS = 8
CONV_K = 5
CHUNK = 64
B_WIDTH = 512
B_HEAD_DIM = 64
B_Q_HEADS = 8
B_KV_HEADS = 2
WINDOW = 128
C_WIDTH = 512
C_HEADS = 4
C_HEAD_DIM = 128
ROPE_THETA = 10000.0
EPS = 1e-6

LANES = 128
GATE_COLS = 4 * A_HEADS
A_COLS = 4 * A_WIDTH
T_QB = 0
T_KB = T_QB + B_WIDTH
T_VB = T_KB + B_KV_HEADS * B_HEAD_DIM
T_ZB = T_VB + B_KV_HEADS * B_HEAD_DIM
T_QC = T_ZB + B_WIDTH
T_ZC = T_QC + C_WIDTH
T_COLS = T_ZC + C_WIDTH

VMEM_LIMIT = 56 * 1024 * 1024
CONV_SHIFT = 16
CONV_HALO = 24
CONV_COLS = 256
CONV_ROWS = 128

_NT = (((1,), (1,)), ((), ()))
_TN = (((0,), (0,)), ((), ()))


def _dot(a, b):
    return jnp.dot(a, b, preferred_element_type=F32)


def _dot_nt(a, b):
    return lax.dot_general(a, b, _NT, preferred_element_type=F32)


def _silu(z):
    return z * jax.nn.sigmoid(z)


def _softplus(z):
    return jnp.maximum(z, 0.0) + jnp.log1p(jnp.exp(-jnp.abs(z)))


def _rope_partner(x):
    lane = lax.broadcasted_iota(jnp.int32, x.shape, 1)
    first_half = (lane % B_HEAD_DIM) < (B_HEAD_DIM // 2)
    up = pltpu.roll(x, LANES - B_HEAD_DIM // 2, axis=1)
    down = pltpu.roll(x, B_HEAD_DIM // 2, axis=1)
    return jnp.where(first_half, up, down)


def _head_rms(x, ones_blockdiag, head_dim, w):
    ss = _dot((x * x).astype(BF16), ones_blockdiag)
    return x * lax.rsqrt(ss * (1.0 / head_dim) + EPS) * w


def _w_prep_kernel(w_ref, wa_ref, wg_ref, wt_ref):
    wa_ref[...] = w_ref[:, :A_COLS].astype(BF16)
    g = w_ref[:, A_COLS:A_COLS + LANES]
    lane = lax.broadcasted_iota(jnp.int32, g.shape, 1)
    wg_ref[...] = jnp.where(lane < GATE_COLS, g, 0.0).astype(BF16)
    wt_ref[...] = w_ref[:, A_COLS + GATE_COLS:A_COLS + GATE_COLS + T_COLS].astype(BF16)


def _w_prep(w_in, tr=256):
    rows, cols = w_in.shape
    row = lambda i: (i, 0)
    return pl.pallas_call(
        _w_prep_kernel,
        grid=(rows // tr,),
        in_specs=[pl.BlockSpec((tr, cols), row)],
        out_specs=[pl.BlockSpec((tr, A_COLS), row), pl.BlockSpec((tr, LANES), row), pl.BlockSpec((tr, T_COLS), row)],
        out_shape=[jax.ShapeDtypeStruct((rows, A_COLS), BF16), jax.ShapeDtypeStruct((rows, LANES), BF16),
                   jax.ShapeDtypeStruct((rows, T_COLS), BF16)],
        compiler_params=pltpu.CompilerParams(dimension_semantics=("arbitrary",), vmem_limit_bytes=VMEM_LIMIT),
        name="w_prep",
    )(w_in)


def _in_proj_kernel(tiles_per_seq, x_ref, nw_ref, wa_ref, wg_ref, wt_ref, cw_ref, cos_ref, sin_ref, alog_ref,
                    dtb_ref, qnb_ref, knb_ref, qnc_ref, e64_ref, e128_ref,
                    qkv_ref, ga_ref, ab_ref, abt_ref, qb_ref, kb_ref, vb_ref, gb_ref, qc_ref, gc_ref,
                    ext_s):
    step_id = pl.program_id(0)

    @pl.when(step_id == 0)
    def _():
        ext_s[:, 0:CONV_HALO, :] = jnp.zeros((ext_s.shape[0], CONV_HALO, A_HEAD_DIM), F32)

    seq_start = (step_id % tiles_per_seq) == 0
    zero = jnp.minimum(step_id, 0)
    x = x_ref[...]
    ms = jnp.mean(x * x, axis=-1, keepdims=True)
    hn = (x * lax.rsqrt(ms + EPS) * nw_ref[...]).astype(BF16)
    tm = x.shape[0]

    def proj(w_ref, c0, width):
        return _dot(hn, w_ref[:, c0:c0 + width])

    fix_rows = lax.broadcasted_iota(jnp.int32, (2 * 8, 1), 0) + (CONV_SHIFT - 8)
    first = CONV_HALO - CONV_SHIFT - CONV_K // 2
    lo = CONV_SHIFT - 8

    def conv_head(c):
        lanes = slice(c, c + A_HEAD_DIM)
        head = c // A_HEAD_DIM
        w = cw_ref[:, lanes]
        taps = [w[j:j + 1, :] for j in range(CONV_K)]
        scale = A_HEAD_DIM ** -0.5 if c < A_WIDTH else 1.0
        for r in range(0, tm, CONV_ROWS):
            y = ext_s[head, pl.ds(zero + (first + r), CONV_ROWS), :] * taps[0]
            for j in range(1, CONV_K):
                y = y + ext_s[head, pl.ds(zero + (first + r + j), CONV_ROWS), :] * taps[j]
            if r == 0:
                cur0 = ext_s[head, CONV_HALO:CONV_HALO + 1, :]
                cur1 = ext_s[head, CONV_HALO + 1:CONV_HALO + 2, :]
                old0 = ext_s[head, CONV_HALO - 2:CONV_HALO - 1, :]
                old1 = ext_s[head, CONV_HALO - 1:CONV_HALO, :]
                cross = (jnp.where(fix_rows == CONV_SHIFT - 2, taps[4] * cur0, 0.0)
                         + jnp.where(fix_rows == CONV_SHIFT - 1, taps[3] * cur0 + taps[4] * cur1, 0.0)
                         + jnp.where(fix_rows == CONV_SHIFT, taps[0] * old0 + taps[1] * old1, 0.0)
                         + jnp.where(fix_rows == CONV_SHIFT + 1, taps[0] * old1, 0.0))
                fixed = y[lo:lo + 16] - jnp.where(seq_start, cross, 0.0)
                y = jnp.concatenate([y[:lo], fixed, y[lo + 16:]], axis=0)
            y = _silu(y)
            if c < 2 * A_WIDTH:
                y = y * (lax.rsqrt(jnp.sum(y * y, axis=-1, keepdims=True) + EPS) * scale)
            qkv_ref[r:r + CONV_ROWS, lanes] = y.astype(BF16)

    for c in range(0, 3 * A_WIDTH, CONV_COLS):
        acc = proj(wa_ref, c, CONV_COLS)
        for hc in range(0, CONV_COLS, A_HEAD_DIM):
            ext_s[(c + hc) // A_HEAD_DIM, CONV_HALO:CONV_HALO + tm, :] = acc[:, hc:hc + A_HEAD_DIM]
        for hc in range(c, c + CONV_COLS, A_HEAD_DIM):
            conv_head(hc)
    ext_s[:, 0:CONV_HALO, :] = ext_s[:, tm:tm + CONV_HALO, :]

    step = 512
    for c in range(0, A_WIDTH, step):
        ga_ref[:, c:c + step] = _silu(proj(wa_ref, 3 * A_WIDTH + c, step)).astype(BF16)

    acc = _dot(hn, wg_ref[...])
    lane = lax.broadcasted_iota(jnp.int32, (tm, LANES), 1)
    g = -jnp.exp(alog_ref[...]) * _softplus(acc + dtb_ref[...])
    val = jnp.where(lane < 2 * A_HEADS, g, jax.nn.sigmoid(acc))
    val = jnp.where(lane < GATE_COLS, val, 0.0)
    pos = lax.broadcasted_iota(jnp.int32, (tm, LANES), 0) % CHUNK
    pre = val
    suf = val
    s = 1
    while s < CHUNK:
        pre = pre + jnp.where(pos >= s, pltpu.roll(pre, s, axis=0), 0.0)
        suf = suf + jnp.where(pos < CHUNK - s, pltpu.roll(suf, tm - s, axis=0), 0.0)
        s *= 2
    res = jnp.where(lane < A_HEADS, pre, jnp.where(lane < 2 * A_HEADS, suf, val))
    ab_ref[...] = res
    abt_ref[0] = res.T

    cos = cos_ref[...]
    sin = sin_ref[...]
    qb = _head_rms(proj(wt_ref, T_QB, B_WIDTH), e64_ref[...], B_HEAD_DIM, qnb_ref[...])
    for c in range(0, B_WIDTH, LANES):
        t = qb[:, c:c + LANES]
        qb_ref[:, c:c + LANES] = (t * cos + _rope_partner(t) * sin).astype(BF16)
    kvb = proj(wt_ref, T_KB, 2 * LANES)
    kb = _head_rms(kvb[:, :LANES], e64_ref[:LANES, :LANES], B_HEAD_DIM, knb_ref[...])
    kb_ref[...] = (kb * cos + _rope_partner(kb) * sin).astype(BF16)
    vb_ref[...] = kvb[:, LANES:].astype(BF16)
    gb_ref[...] = _silu(proj(wt_ref, T_ZB, B_WIDTH)).astype(BF16)
    qc_ref[...] = _head_rms(proj(wt_ref, T_QC, C_WIDTH), e128_ref[...], C_HEAD_DIM, qnc_ref[...]).astype(BF16)
    gc_ref[...] = _silu(proj(wt_ref, T_ZC, C_WIDTH)).astype(BF16)


def _block_diag_ones(width, block):
    idx = np.arange(width) // block
    return jnp.asarray(idx[:, None] == idx[None, :], dtype=BF16)


def _rope_tables(seq):
    d = B_HEAD_DIM
    inv = ROPE_THETA ** (-jnp.arange(0, d, 2, dtype=F32) / d)
    ang = jnp.arange(seq, dtype=F32)[:, None] * inv[None, :]
    cos = jnp.cos(ang)
    sin = jnp.sin(ang)
    cos_h = jnp.concatenate([cos, cos], axis=-1)
    sin_h = jnp.concatenate([-sin, sin], axis=-1)
    reps = LANES // d
    return jnp.tile(cos_h, (1, reps)), jnp.tile(sin_h, (1, reps))


def _in_proj(x2, norm_w, w_a, w_g, w_t, conv_w, cos, sin, alog_row, dtb_row, qnb, knb, qnc, seq, tm=256):
    t = x2.shape[0]
    tiles_per_seq = seq // tm
    nt = t // tm
    row = lambda i: (jnp.minimum(i, nt - 1), 0)
    const = lambda i: (0, 0)
    pos = lambda i: (jnp.minimum(i, nt - 1) % tiles_per_seq, 0)
    in_specs = [
        pl.BlockSpec((tm, D_MODEL), row),
        pl.BlockSpec((1, D_MODEL), const),
        pl.BlockSpec((D_MODEL, A_COLS), const, pipeline_mode=pl.Buffered(1)),
        pl.BlockSpec((D_MODEL, LANES), const, pipeline_mode=pl.Buffered(1)),
        pl.BlockSpec((D_MODEL, T_COLS), const, pipeline_mode=pl.Buffered(1)),
        pl.BlockSpec((CONV_K, 3 * A_WIDTH), const),
        pl.BlockSpec((tm, LANES), pos),
        pl.BlockSpec((tm, LANES), pos),
        pl.BlockSpec((1, LANES), const),
        pl.BlockSpec((1, LANES), const),
        pl.BlockSpec((1, B_WIDTH), const),
        pl.BlockSpec((1, LANES), const),
        pl.BlockSpec((1, C_WIDTH), const),
        pl.BlockSpec((B_WIDTH, B_WIDTH), const),
        pl.BlockSpec((C_WIDTH, C_WIDTH), const),
    ]
    out_shape = [
        jax.ShapeDtypeStruct((t + tm, 3 * A_WIDTH), BF16),
        jax.ShapeDtypeStruct((t, A_WIDTH), BF16),
        jax.ShapeDtypeStruct((t, LANES), F32),
        jax.ShapeDtypeStruct((t // tm, LANES, tm), F32),
        jax.ShapeDtypeStruct((t, B_WIDTH), BF16),
        jax.ShapeDtypeStruct((t, LANES), BF16),
        jax.ShapeDtypeStruct((t, LANES), BF16),
        jax.ShapeDtypeStruct((t, B_WIDTH), BF16),
        jax.ShapeDtypeStruct((t, C_WIDTH), BF16),
        jax.ShapeDtypeStruct((t, C_WIDTH), BF16),
    ]
    out_specs = [
        pl.BlockSpec((tm, 3 * A_WIDTH), lambda i: (i, 0)),
        pl.BlockSpec((tm, A_WIDTH), row),
        pl.BlockSpec((tm, LANES), row),
        pl.BlockSpec((1, LANES, tm), lambda i: (jnp.minimum(i, nt - 1), 0, 0)),
        pl.BlockSpec((tm, B_WIDTH), row),
        pl.BlockSpec((tm, LANES), row),
        pl.BlockSpec((tm, LANES), row),
        pl.BlockSpec((tm, B_WIDTH), row),
        pl.BlockSpec((tm, C_WIDTH), row),
        pl.BlockSpec((tm, C_WIDTH), row),
    ]
    return pl.pallas_call(
        functools.partial(_in_proj_kernel, tiles_per_seq),
        grid=(nt + 1,),
        in_specs=in_specs,
        out_specs=out_specs,
        out_shape=out_shape,
        scratch_shapes=[pltpu.VMEM((3 * A_HEADS, CONV_HALO + tm, A_HEAD_DIM), F32)],
        compiler_params=pltpu.CompilerParams(dimension_semantics=("arbitrary",), vmem_limit_bytes=VMEM_LIMIT),
        name="in_proj",
    )(x2, norm_w, w_a, w_g, w_t, conv_w, cos, sin, alog_row, dtb_row, qnb, knb, qnc,
      _block_diag_ones(B_WIDTH, B_HEAD_DIM), _block_diag_ones(C_WIDTH, C_HEAD_DIM))


def _mem_proj_kernel(m_ref, nw_ref, w_ref, knc_ref, e128_ref, km_ref, vm_ref):
    x = m_ref[...]
    ms = jnp.mean(x * x, axis=-1, keepdims=True)
    mn = (x * lax.rsqrt(ms + EPS) * nw_ref[...]).astype(BF16)
    km = _dot(mn, w_ref[:, :C_WIDTH])
    km_ref[...] = _head_rms(km, e128_ref[...], C_HEAD_DIM, knc_ref[...]).astype(BF16)
    vm_ref[...] = _dot(mn, w_ref[:, C_WIDTH:]).astype(BF16)


def _mem_proj(mem2, mem_norm_w, w_kv, knc, tm=256):
    t = mem2.shape[0]
    row = lambda i: (i, 0)
    const = lambda i: (0, 0)
    return pl.pallas_call(
        _mem_proj_kernel,
        grid=(t // tm,),
        in_specs=[
            pl.BlockSpec((tm, D_MODEL), row),
            pl.BlockSpec((1, D_MODEL), const),
            pl.BlockSpec((D_MODEL, 2 * C_WIDTH), const),
            pl.BlockSpec((1, C_WIDTH), const),
            pl.BlockSpec((C_WIDTH, C_WIDTH), const),
        ],
        out_specs=[pl.BlockSpec((tm, C_WIDTH), row), pl.BlockSpec((tm, C_WIDTH), row)],
        out_shape=[jax.ShapeDtypeStruct((t, C_WIDTH), BF16), jax.ShapeDtypeStruct((t, C_WIDTH), BF16)],
        compiler_params=pltpu.CompilerParams(dimension_semantics=("arbitrary",), vmem_limit_bytes=VMEM_LIMIT),
        name="mem_proj",
    )(mem2, mem_norm_w, w_kv, knc, _block_diag_ones(C_WIDTH, C_HEAD_DIM))


HEADS_PER_STEP = 2
PACK = 4
PACK_ROWS = PACK * CHUNK
SQUARINGS = 5
PREP_PACKS = 2


def _deltanet_kernel(q_ref, k_ref, v_ref, qt_ref, kt_ref, vt_ref, ab_ref, gfr_ref, gbr_ref, ga_ref, on_ref,
                     out_ref,
                     qs, ks, vs, kq_s, b_s, cd_s, o_s):
    seq = q_ref.shape[0]
    nc = seq // CHUNK
    hb = HEADS_PER_STEP
    h0 = pl.program_id(1) * hb
    slab = 256

    def slab_rows(i):
        return pl.ds(pl.multiple_of(i * slab, slab), slab)

    def stage(src_ref, tail_ref, dst_ref, hs):
        cols = slice(hs * LANES, (hs + 1) * LANES)

        def copy(i, carry):
            src_rows = pl.ds(pl.multiple_of(CONV_SHIFT + i * slab, CONV_SHIFT), slab)
            dst_ref[hs, slab_rows(i), :] = src_ref[src_rows, cols].astype(F32)
            return carry

        lax.fori_loop(0, seq // slab - 1, copy, 0)
        last = seq - slab
        dst_ref[hs, last:seq - CONV_SHIFT, :] = src_ref[last + CONV_SHIFT:seq, cols].astype(F32)
        dst_ref[hs, seq - CONV_SHIFT:seq, :] = tail_ref[:, cols].astype(F32)

    for hs in range(hb):
        stage(q_ref, qt_ref, qs, hs)
        stage(k_ref, kt_ref, ks, hs)
        stage(v_ref, vt_ref, vs, hs)

    ii = lax.broadcasted_iota(jnp.int32, (CHUNK, PACK_ROWS), 0)
    jl = lax.broadcasted_iota(jnp.int32, (CHUNK, PACK_ROWS), 1)
    lb = jl // CHUNK
    jj = jl % CHUNK
    block_diag = (lax.broadcasted_iota(jnp.int32, (PACK_ROWS, PACK_ROWS), 0) // CHUNK
                  == lax.broadcasted_iota(jnp.int32, (PACK_ROWS, PACK_ROWS), 1) // CHUNK)

    def pack_diag(full):
        out = full[0:CHUNK]
        for c in range(1, PACK):
            out = jnp.where(lb == c, full[c * CHUNK:(c + 1) * CHUNK], out)
        return out

    def pack_cols(col):
        out = None
        for c in range(PACK):
            blk = col[c * CHUNK:(c + 1) * CHUNK]
            blk = jnp.concatenate([blk, blk], axis=1)
            out = blk if out is None else jnp.where(lb == c, blk, out)
        return out

    def to_block_diag(packed):
        return jnp.where(block_diag, jnp.concatenate([packed] * PACK, axis=0), 0.0).astype(BF16)

    lane4 = lax.broadcasted_iota(jnp.int32, (PACK_ROWS, LANES), 1)
    npk = seq // PACK_ROWS
    chains = [(hs, d) for hs in range(hb) for d in range(2)]

    def prep_gen(it):
        members = []
        for pk in range(PREP_PACKS):
            for d in range(2):
                fwd = d == 0
                p = it * PREP_PACKS + pk
                if not fwd:
                    p = npk - 1 - p
                rows = pl.ds(pl.multiple_of(p * PACK_ROWS, PACK_ROWS), PACK_ROWS)
                ab4 = ab_ref[rows, :]

                def gate_col(col, ab4=ab4):
                    c = jnp.sum(jnp.where(lane4 == col, ab4, 0.0), axis=-1, keepdims=True)
                    return jnp.broadcast_to(c, (PACK_ROWS, LANES))

                incl = (ii >= jj) if fwd else (ii <= jj)
                strict = (ii > jj) if fwd else (ii < jj)
                for hs in range(hb):
                    k4 = ks[hs, rows, :]
                    q4 = qs[hs, rows, :]
                    v4 = vs[hs, rows, :]
                    k4b = k4.astype(BF16)
                    kk_p = pack_diag(_dot_nt(k4b, k4b))
                    qk_p = pack_diag(_dot_nt(q4.astype(BF16), k4b))
                    gc4 = gate_col(d * A_HEADS + h0 + hs)
                    bt4 = gate_col((2 + d) * A_HEADS + h0 + hs)
                    gcr = (gfr_ref if fwd else gbr_ref)[p, pl.ds(h0 + hs, 1), :]
                    decay = jnp.where(incl, jnp.exp(jnp.where(incl, pack_cols(gc4) - gcr, 0.0)), 0.0)
                    n_cat = jnp.where(strict, -(kk_p * decay * pack_cols(bt4)), 0.0)
                    qkm = jnp.where(incl, qk_p * decay, 0.0)
                    members.append((hs, d, k4, q4, v4, gc4, bt4, n_cat, qkm, p, rows))
        yield

        eye = jnp.where(ii == jj, 1.0, 0.0)
        tinv = [eye + m[7] for m in members]
        pw = [_dot(m[7].astype(BF16), to_block_diag(m[7])) for m in members]
        yield
        for level in range(1, SQUARINGS + 1):
            last = level == SQUARINGS
            for i in range(len(members)):
                rhs = to_block_diag(pw[i])
                if last:
                    tinv[i] = tinv[i] + _dot(tinv[i].astype(BF16), rhs)
                else:
                    both = _dot(jnp.concatenate([pw[i], tinv[i]], axis=0).astype(BF16), rhs)
                    pw[i] = both[:CHUNK]
                    tinv[i] = tinv[i] + both[CHUNK:]
            yield

        uws = []
        for (hs, d, k4, q4, v4, gc4, bt4, _, qkm, p, rows), t_p in zip(members, tinv):
            eg = jnp.exp(gc4)
            rhs = jnp.concatenate([v4 * bt4, k4 * bt4 * eg], axis=1).astype(BF16)
            uws.append((_dot(to_block_diag(t_p), rhs), eg))
        yield

        folded = []
        for (hs, d, k4, q4, v4, gc4, bt4, _, qkm, p, rows), (uw, eg) in zip(members, uws):
            fwd = d == 0
            uwb = uw.astype(BF16)
            per_chunk = []
            for c in range(PACK):
                rs = slice(c * CHUNK, (c + 1) * CHUNK)
                last_row = (c + 1) * CHUNK - 1 if fwd else c * CHUNK
                gl = gc4[last_row:last_row + 1, :]
                kd = k4[rs] * jnp.exp(gl - gc4[rs])
                lhs = jnp.concatenate([qkm[:, c * CHUNK:(c + 1) * CHUNK], kd.T], axis=0).astype(BF16)
                per_chunk.append((_dot(lhs, uwb[rs]), gl))
            folded.append(per_chunk)
        yield

        for (hs, d, k4, q4, v4, gc4, bt4, _, qkm, p, rows), (uw, eg), per_chunk in zip(members, uws, folded):
            qd = q4 * eg
            for c, (res, gl) in enumerate(per_chunk):
                n = p * PACK + c
                rs = slice(c * CHUNK, (c + 1) * CHUNK)
                o_s[hs, d, pl.ds(pl.multiple_of(n * CHUNK, CHUNK), CHUNK), :] = res[:CHUNK, :LANES]
                kq_s[hs, d, n, 0:LANES, :] = res[CHUNK:, LANES:].astype(BF16)
                kq_s[hs, d, n, LANES:LANES + CHUNK, :] = (qd[rs] - res[:CHUNK, LANES:]).astype(BF16)
                b_s[hs, d, n] = res[CHUNK:, :LANES]
                cd_s[hs, d, n] = jnp.broadcast_to(jnp.exp(gl), (8, LANES))

    steps_per_stage = PREP_PACKS * PACK

    def seq_gen(it, states, result):
        states = list(states)
        for k in range(steps_per_stage):
            i = it * steps_per_stage + k
            ns = (i, nc - 1 - i)
            rows = [pl.ds(pl.multiple_of(n * CHUNK, CHUNK), CHUNK) for n in ns]
            outs = [_dot(kq_s[hs, d, ns[d]], st.astype(BF16)) for (hs, d), st in zip(chains, states)]
            for c, ((hs, d), out) in enumerate(zip(chains, outs)):
                o_s[hs, d, rows[d], :] = o_s[hs, d, rows[d], :] + out[LANES:]
                states[c] = states[c] * cd_s[hs, d, ns[d]][0:1, :] - out[:LANES] + b_s[hs, d, ns[d]]
            yield
        result.append(tuple(states))

    def run(gen):
        for _ in gen:
            pass

    def fused_body(j, states):
        result = []
        pg = prep_gen(j + 1)
        sg = seq_gen(j, states, result)
        next(pg)
        for _ in range(steps_per_stage):
            next(pg)
            next(sg)
        run(sg)
        run(pg)
        return result[0]

    n_stages = npk // PREP_PACKS
    s0 = jnp.zeros((A_HEAD_DIM, A_HEAD_DIM), F32)
    run(prep_gen(0))
    states = lax.fori_loop(0, n_stages - 1, fused_body, tuple(s0 for _ in chains))
    run(seq_gen(n_stages - 1, states, []))

    def finish(i, carry):
        for hs in range(hb):
            cols = slice(hs * LANES, (hs + 1) * LANES)
            o = o_s[hs, 0, slab_rows(i), :] + o_s[hs, 1, slab_rows(i), :]
            y = o * lax.rsqrt(jnp.mean(o * o, axis=-1, keepdims=True) + EPS) * on_ref[...]
            out_ref[slab_rows(i), cols] = (y * ga_ref[slab_rows(i), cols].astype(F32)).astype(BF16)
        return carry

    lax.fori_loop(0, seq // slab, finish, 0)


def _deltanet(qkv, ab, abt3, ga, o_norm, batch, seq):
    nc = seq // CHUNK
    npk = seq // PACK_ROWS
    t = batch * seq
    hb = HEADS_PER_STEP
    hw = hb * LANES
    groups = A_HEADS // hb
    tail_blocks = seq // CONV_SHIFT
    scratch = [
        pltpu.VMEM((hb, seq, LANES), F32),
        pltpu.VMEM((hb, seq, LANES), F32),
        pltpu.VMEM((hb, seq, LANES), F32),
        pltpu.VMEM((hb, 2, nc, LANES + CHUNK, LANES), BF16),
        pltpu.VMEM((hb, 2, nc, LANES, LANES), F32),
        pltpu.VMEM((hb, 2, nc, 8, LANES), F32),
        pltpu.VMEM((hb, 2, seq, LANES), F32),
    ]
    return pl.pallas_call(
        _deltanet_kernel,
        grid=(batch, groups),
        in_specs=[
            pl.BlockSpec((seq, hw), lambda b, g: (b, g)),
            pl.BlockSpec((seq, hw), lambda b, g: (b, groups + g)),
            pl.BlockSpec((seq, hw), lambda b, g: (b, 2 * groups + g)),
            pl.BlockSpec((CONV_SHIFT, hw), lambda b, g: ((b + 1) * tail_blocks, g)),
            pl.BlockSpec((CONV_SHIFT, hw), lambda b, g: ((b + 1) * tail_blocks, groups + g)),
            pl.BlockSpec((CONV_SHIFT, hw), lambda b, g: ((b + 1) * tail_blocks, 2 * groups + g)),
            pl.BlockSpec((seq, LANES), lambda b, g: (b, 0)),
            pl.BlockSpec((npk, A_HEADS, PACK_ROWS), lambda b, g: (b, 0, 0)),
            pl.BlockSpec((npk, A_HEADS, PACK_ROWS), lambda b, g: (b, 1, 0)),
            pl.BlockSpec((seq, hw), lambda b, g: (b, g)),
            pl.BlockSpec((1, LANES), lambda b, g: (0, 0)),
        ],
        out_specs=pl.BlockSpec((seq, hw), lambda b, g: (b, g)),
        out_shape=jax.ShapeDtypeStruct((t, A_WIDTH), BF16),
        scratch_shapes=scratch,
        compiler_params=pltpu.CompilerParams(dimension_semantics=("arbitrary", "arbitrary"),
                                             vmem_limit_bytes=VMEM_LIMIT),
        name="deltanet",
    )(qkv, qkv, qkv, qkv, qkv, qkv, ab, abt3, abt3, ga, o_norm)


Q_BLOCKS = 2


def _attn_kernel(sink_ref, qb_ref, kb_ref, vb_ref, gb_ref, qc_ref, km_ref, vm_ref, gc_ref, out_ref):
    nb = pl.num_programs(1) * Q_BLOCKS
    w = WINDOW
    lane = lax.broadcasted_iota(jnp.int32, (w, LANES), 1)
    lower = lane < B_HEAD_DIM
    ri = lax.broadcasted_iota(jnp.int32, (2 * w, w), 0) % w
    ci = lax.broadcasted_iota(jnp.int32, (2 * w, w), 1)
    top = lax.broadcasted_iota(jnp.int32, (2 * w, 1), 0) < w

    first_blk = pl.program_id(1) * Q_BLOCKS - 1
    kv = []
    for i in range(Q_BLOCKS + 2):
        j = first_blk + i
        start = pl.multiple_of(jnp.clip(j, 0, nb - 1) * w, w)
        kblk = kb_ref[pl.ds(start, w), :].astype(F32)
        vblk = vb_ref[pl.ds(start, w), :].astype(F32)
        ksw = pltpu.roll(kblk, B_HEAD_DIM, axis=1)
        vsw = pltpu.roll(vblk, B_HEAD_DIM, axis=1)
        variants = {}
        for hk in range(B_KV_HEADS):
            for half in range(2):
                keep = lower if half == 0 else jnp.logical_not(lower)
                in_place = hk == half
                variants[hk, half] = (jnp.where(keep, kblk if in_place else ksw, 0.0).astype(BF16),
                                      jnp.where(keep, vblk if in_place else vsw, 0.0).astype(BF16))
        kv.append((variants, jnp.logical_and(j >= 0, j < nb)))

    for qi in range(Q_BLOCKS):
        rows = slice(qi * w, (qi + 1) * w)
        masks = (jnp.logical_and(ci >= ri, kv[qi][1]), None, jnp.logical_and(ci <= ri, kv[qi + 2][1]))
        for hk in range(B_KV_HEADS):
            c0 = hk * 2 * LANES
            q2 = jnp.concatenate([qb_ref[rows, c0:c0 + LANES], qb_ref[rows, c0 + LANES:c0 + 2 * LANES]], axis=0)
            acc = jnp.zeros((2 * w, LANES), F32)
            for half in range(2):
                sink_col = jnp.where(top, sink_ref[4 * hk + half], sink_ref[4 * hk + 2 + half])
                ss = []
                for rel in range(3):
                    s = _dot_nt(q2, kv[qi + rel][0][hk, half][0])
                    if masks[rel] is not None:
                        s = jnp.where(masks[rel], s, -jnp.inf)
                    ss.append(s)
                m = jnp.maximum(sink_col, jnp.max(jnp.maximum(jnp.maximum(ss[0], ss[1]), ss[2]),
                                                  axis=-1, keepdims=True))
                ps = [jnp.exp(s - m) for s in ss]
                den = jnp.exp(sink_col - m) + jnp.sum(ps[0] + ps[1] + ps[2], axis=-1, keepdims=True)
                pv = jnp.zeros((2 * w, LANES), F32)
                for rel in range(3):
                    pv = pv + _dot(ps[rel].astype(BF16), kv[qi + rel][0][hk, half][1])
                acc = acc + pv / den
            out_ref[rows, c0:c0 + LANES] = (acc[:w] * gb_ref[rows, c0:c0 + LANES].astype(F32)).astype(BF16)
            out_ref[rows, c0 + LANES:c0 + 2 * LANES] = (
                acc[w:] * gb_ref[rows, c0 + LANES:c0 + 2 * LANES].astype(F32)).astype(BF16)

    for hc in range(C_HEADS):
        c0 = hc * C_HEAD_DIM
        s = _dot_nt(qc_ref[:, c0:c0 + C_HEAD_DIM], km_ref[:, c0:c0 + C_HEAD_DIM])
        m = jnp.max(s, axis=-1, keepdims=True)
        p = jnp.exp(s - m)
        den = jnp.sum(p, axis=-1, keepdims=True)
        o = _dot(p.astype(BF16), vm_ref[:, c0:c0 + C_HEAD_DIM]) / den
        out_ref[:, B_WIDTH + c0:B_WIDTH + c0 + C_HEAD_DIM] = (
            o * gc_ref[:, c0:c0 + C_HEAD_DIM].astype(F32)).astype(BF16)


def _attention(sink, qb, kb, vb, gb, qc, km, vm, gcg, batch, seq):
    nb = seq // (WINDOW * Q_BLOCKS)
    qw = WINDOW * Q_BLOCKS
    t = batch * seq
    mlen = km.shape[0] // batch
    qrow = lambda b, n: (b * nb + n, 0)
    per_b = lambda b, n: (b, 0)
    return pl.pallas_call(
        _attn_kernel,
        grid=(batch, nb),
        in_specs=[
            pl.BlockSpec(memory_space=pltpu.SMEM),
            pl.BlockSpec((qw, B_WIDTH), qrow),
            pl.BlockSpec((seq, LANES), per_b),
            pl.BlockSpec((seq, LANES), per_b),
            pl.BlockSpec((qw, B_WIDTH), qrow),
            pl.BlockSpec((qw, C_WIDTH), qrow),
            pl.BlockSpec((mlen, C_WIDTH), per_b),
            pl.BlockSpec((mlen, C_WIDTH), per_b),
            pl.BlockSpec((qw, C_WIDTH), qrow),
        ],
        out_specs=pl.BlockSpec((qw, B_WIDTH + C_WIDTH), qrow),
        out_shape=jax.ShapeDtypeStruct((t, B_WIDTH + C_WIDTH), BF16),
        compiler_params=pltpu.CompilerParams(dimension_semantics=("arbitrary", "arbitrary"),
                                             vmem_limit_bytes=VMEM_LIMIT),
        name="attention",
    )(sink, qb, kb, vb, gb, qc, km, vm, gcg)


def _out_proj_kernel(x_ref, ma_ref, mbc_ref, wa_ref, wbc_ref, out_ref):
    out_ref[...] = x_ref[...] + _dot(ma_ref[...], wa_ref[...]) + _dot(mbc_ref[...], wbc_ref[...])


def _out_proj(x2, mixed_a, mixed_bc, w_out, tm=512):
    t = x2.shape[0]
    row = lambda i: (i, 0)
    const = lambda i: (0, 0)
    return pl.pallas_call(
        _out_proj_kernel,
        grid=(t // tm,),
        in_specs=[
            pl.BlockSpec((tm, D_MODEL), row),
            pl.BlockSpec((tm, A_WIDTH), row),
            pl.BlockSpec((tm, B_WIDTH + C_WIDTH), row),
            pl.BlockSpec((A_WIDTH, D_MODEL), const),
            pl.BlockSpec((B_WIDTH + C_WIDTH, D_MODEL), lambda i: (1, 0)),
        ],
        out_specs=pl.BlockSpec((tm, D_MODEL), row),
        out_shape=jax.ShapeDtypeStruct((t, D_MODEL), F32),
        compiler_params=pltpu.CompilerParams(dimension_semantics=("arbitrary",), vmem_limit_bytes=VMEM_LIMIT),
        name="out_proj",
    )(x2, mixed_a, mixed_bc, w_out, w_out)


def _layer(h, mem, norm_w, w_in, conv_w_a, a_log_fwd, a_log_bwd, dt_bias_fwd, dt_bias_bwd, o_norm_a,
           q_norm_b, k_norm_b, sink_b, mem_norm_w, w_mem_kv, q_norm_c, k_norm_c, w_out):
    batch, seq, _ = h.shape
    t = batch * seq
    x2 = h.reshape(t, D_MODEL)

    w_a, w_g, w_t = _w_prep(w_in)
    zeros = jnp.zeros((LANES - 2 * A_HEADS,), F32)
    alog_row = jnp.concatenate([a_log_fwd, a_log_bwd, zeros]).reshape(1, LANES)
    dtb_row = jnp.concatenate([dt_bias_fwd, dt_bias_bwd, zeros]).reshape(1, LANES)
    cos, sin = _rope_tables(seq)

    qkv, ga, ab, abt, qb, kb, vb, gb, qc, gcg = _in_proj(
        x2, norm_w.reshape(1, D_MODEL), w_a, w_g, w_t, conv_w_a, cos, sin, alog_row, dtb_row,
        (jnp.tile(q_norm_b, B_Q_HEADS) * B_HEAD_DIM ** -0.5).reshape(1, B_WIDTH),
        jnp.tile(k_norm_b, B_KV_HEADS).reshape(1, LANES),
        (jnp.tile(q_norm_c, C_HEADS) * C_HEAD_DIM ** -0.5).reshape(1, C_WIDTH), seq)

    mlen = mem.shape[1]
    km, vm = _mem_proj(mem.reshape(batch * mlen, D_MODEL), mem_norm_w.reshape(1, D_MODEL),
                       w_mem_kv.astype(BF16), jnp.tile(k_norm_c, C_HEADS).reshape(1, C_WIDTH))

    mixed_a = _deltanet(qkv, ab, abt, ga,
                        o_norm_a.reshape(1, A_HEAD_DIM), batch, seq)
    mixed_bc = _attention(sink_b, qb, kb, vb, gb, qc, km, vm, gcg, batch, seq)

    out = _out_proj(x2, mixed_a, mixed_bc, w_out.astype(BF16))
    return out.reshape(batch, seq, D_MODEL)


def kernel(x, mem, norm_w, w_in, conv_w_a, a_log_fwd, a_log_bwd, dt_bias_fwd, dt_bias_bwd, o_norm_a,
           q_norm_b, k_norm_b, sink_b, mem_norm_w, w_mem_kv, q_norm_c, k_norm_c, w_out):
    h = x
    for l in range(norm_w.shape[0]):
        h = _layer(h, mem, norm_w[l], w_in[l], conv_w_a[l], a_log_fwd[l], a_log_bwd[l], dt_bias_fwd[l],
                   dt_bias_bwd[l], o_norm_a[l], q_norm_b[l], k_norm_b[l], sink_b[l], mem_norm_w[l],
                   w_mem_kv[l], q_norm_c[l], k_norm_c[l], w_out[l])
    return h
```

```python
import functools

import numpy as np
import jax
import jax.numpy as jnp
from jax import lax
from jax.experimental import pallas as pl
from jax.experimental.pallas import tpu as pltpu

F32 = jnp.float32
BF16 = jnp.bfloat16

D_MODEL = 2048
A_WIDTH = 1024
A_HEAD_DIM = 128
A_HEADS = 8
CONV_K = 5
CHUNK = 64
B_WIDTH = 512
B_HEAD_DIM = 64
B_Q_HEADS = 8
B_KV_HEADS = 2
WINDOW = 128
C_WIDTH = 512
C_HEADS = 4
C_HEAD_DIM = 128
ROPE_THETA = 10000.0
EPS = 1e-6

LANES = 128
GATE_COLS = 4 * A_HEADS
A_COLS = 4 * A_WIDTH
T_QB = 0
T_KB = T_QB + B_WIDTH
T_VB = T_KB + B_KV_HEADS * B_HEAD_DIM
T_ZB = T_VB + B_KV_HEADS * B_HEAD_DIM
T_QC = T_ZB + B_WIDTH
T_ZC = T_QC + C_WIDTH
T_COLS = T_ZC + C_WIDTH

VMEM_LIMIT = 56 * 1024 * 1024
CONV_SHIFT = 16
CONV_HALO = 24
CONV_COLS = 256
CONV_ROWS = 128

_NT = (((1,), (1,)), ((), ()))
_TN = (((0,), (0,)), ((), ()))


def _dot(a, b):
    return jnp.dot(a, b, preferred_element_type=F32)


def _dot_nt(a, b):
    return lax.dot_general(a, b, _NT, preferred_element_type=F32)


def _silu(z):
    return z * jax.nn.sigmoid(z)


def _softplus(z):
    return jnp.maximum(z, 0.0) + jnp.log1p(jnp.exp(-jnp.abs(z)))


def _rope_partner(x):
    lane = lax.broadcasted_iota(jnp.int32, x.shape, 1)
    first_half = (lane % B_HEAD_DIM) < (B_HEAD_DIM // 2)
    up = pltpu.roll(x, LANES - B_HEAD_DIM // 2, axis=1)
    down = pltpu.roll(x, B_HEAD_DIM // 2, axis=1)
    return jnp.where(first_half, up, down)


def _head_rms(x, ones_blockdiag, head_dim, w):
    ss = _dot((x * x).astype(BF16), ones_blockdiag)
    return x * lax.rsqrt(ss * (1.0 / head_dim) + EPS) * w


def _w_prep_kernel(w_ref, wa_ref, wg_ref, wt_ref):
    wa_ref[...] = w_ref[:, :A_COLS].astype(BF16)
    g = w_ref[:, A_COLS:A_COLS + LANES]
    lane = lax.broadcasted_iota(jnp.int32, g.shape, 1)
    wg_ref[...] = jnp.where(lane < GATE_COLS, g, 0.0).astype(BF16)
    wt_ref[...] = w_ref[:, A_COLS + GATE_COLS:A_COLS + GATE_COLS + T_COLS].astype(BF16)


def _w_prep(w_in, tr=256):
    rows, cols = w_in.shape
    row = lambda i: (i, 0)
    return pl.pallas_call(
        _w_prep_kernel,
        grid=(rows // tr,),
        in_specs=[pl.BlockSpec((tr, cols), row)],
        out_specs=[pl.BlockSpec((tr, A_COLS), row), pl.BlockSpec((tr, LANES), row), pl.BlockSpec((tr, T_COLS), row)],
        out_shape=[jax.ShapeDtypeStruct((rows, A_COLS), BF16), jax.ShapeDtypeStruct((rows, LANES), BF16),
                   jax.ShapeDtypeStruct((rows, T_COLS), BF16)],
        compiler_params=pltpu.CompilerParams(dimension_semantics=("arbitrary",), vmem_limit_bytes=VMEM_LIMIT),
        name="w_prep",
    )(w_in)


def _in_proj_kernel(tiles_per_seq, x_ref, nw_ref, wa_ref, wg_ref, wt_ref, cw_ref, cos_ref, sin_ref, alog_ref,
                    dtb_ref, qnb_ref, knb_ref, qnc_ref, e64_ref, e128_ref,
                    qkv_ref, ga_ref, ab_ref, abt_ref, qb_ref, kb_ref, vb_ref, gb_ref, qc_ref, gc_ref,
                    ext_s):
    step_id = pl.program_id(0)

    @pl.when(step_id == 0)
    def _():
        ext_s[:, 0:CONV_HALO, :] = jnp.zeros((ext_s.shape[0], CONV_HALO, A_HEAD_DIM), F32)

    seq_start = (step_id % tiles_per_seq) == 0
    zero = jnp.minimum(step_id, 0)
    x = x_ref[...]
    ms = jnp.mean(x * x, axis=-1, keepdims=True)
    hn = (x * lax.rsqrt(ms + EPS) * nw_ref[...]).astype(BF16)
    tm = x.shape[0]

    def proj(w_ref, c0, width):
        return _dot(hn, w_ref[:, c0:c0 + width])

    fix_rows = lax.broadcasted_iota(jnp.int32, (2 * 8, 1), 0) + (CONV_SHIFT - 8)
    first = CONV_HALO - CONV_SHIFT - CONV_K // 2
    lo = CONV_SHIFT - 8

    def conv_head(c):
        lanes = slice(c, c + A_HEAD_DIM)
        head = c // A_HEAD_DIM
        w = cw_ref[:, 0, lanes]
        taps = [w[j:j + 1, :] for j in range(CONV_K)]
        scale = A_HEAD_DIM ** -0.5 if c < A_WIDTH else 1.0
        for r in range(0, tm, CONV_ROWS):
            y = ext_s[head, pl.ds(zero + (first + r), CONV_ROWS), :] * taps[0]
            for j in range(1, CONV_K):
                y = y + ext_s[head, pl.ds(zero + (first + r + j), CONV_ROWS), :] * taps[j]
            if r == 0:
                cur0 = ext_s[head, CONV_HALO:CONV_HALO + 1, :]
                cur1 = ext_s[head, CONV_HALO + 1:CONV_HALO + 2, :]
                old0 = ext_s[head, CONV_HALO - 2:CONV_HALO - 1, :]
                old1 = ext_s[head, CONV_HALO - 1:CONV_HALO, :]
                cross = (jnp.where(fix_rows == CONV_SHIFT - 2, taps[4] * cur0, 0.0)
                         + jnp.where(fix_rows == CONV_SHIFT - 1, taps[3] * cur0 + taps[4] * cur1, 0.0)
                         + jnp.where(fix_rows == CONV_SHIFT, taps[0] * old0 + taps[1] * old1, 0.0)
                         + jnp.where(fix_rows == CONV_SHIFT + 1, taps[0] * old1, 0.0))
                fixed = y[lo:lo + 16] - jnp.where(seq_start, cross, 0.0)
                y = jnp.concatenate([y[:lo], fixed, y[lo + 16:]], axis=0)
            y = _silu(y)
            if c < 2 * A_WIDTH:
                y = y * (lax.rsqrt(jnp.sum(y * y, axis=-1, keepdims=True) + EPS) * scale)
            qkv_ref[r:r + CONV_ROWS, lanes] = y.astype(BF16)

    pending = list(range(0, 3 * A_WIDTH, A_HEAD_DIM))

    def conv_some(n):
        for _ in range(n):
            if pending:
                conv_head(pending.pop(0))

    for c in range(0, 3 * A_WIDTH, CONV_COLS):
        acc = proj(wa_ref, c, CONV_COLS)
        for hc in range(0, CONV_COLS, A_HEAD_DIM):
            ext_s[(c + hc) // A_HEAD_DIM, CONV_HALO:CONV_HALO + tm, :] = acc[:, hc:hc + A_HEAD_DIM]
        if c >= CONV_COLS:
            conv_some(1)

    step = 512
    for c in range(0, A_WIDTH, step):
        ga_ref[:, c:c + step] = _silu(proj(wa_ref, 3 * A_WIDTH + c, step)).astype(BF16)
        conv_some(2)

    acc = _dot(hn, wg_ref[...])
    lane = lax.broadcasted_iota(jnp.int32, (tm, LANES), 1)
    g = -jnp.exp(alog_ref[...]) * _softplus(acc + dtb_ref[...])
    val = jnp.where(lane < 2 * A_HEADS, g, jax.nn.sigmoid(acc))
    val = jnp.where(lane < GATE_COLS, val, 0.0)
    pos = lax.broadcasted_iota(jnp.int32, (tm, LANES), 0) % CHUNK
    pre = val
    suf = val
    s = 1
    while s < CHUNK:
        pre = pre + jnp.where(pos >= s, pltpu.roll(pre, s, axis=0), 0.0)
        suf = suf + jnp.where(pos < CHUNK - s, pltpu.roll(suf, tm - s, axis=0), 0.0)
        s *= 2
    res = jnp.where(lane < A_HEADS, pre, jnp.where(lane < 2 * A_HEADS, suf, val))
    ab_ref[...] = res
    abt_ref[0] = res.T
    conv_some(1)

    cos = cos_ref[...]
    sin = sin_ref[...]
    qb = _head_rms(proj(wt_ref, T_QB, B_WIDTH), e64_ref[...], B_HEAD_DIM, qnb_ref[...])
    for c in range(0, B_WIDTH, LANES):
        t = qb[:, c:c + LANES]
        qb_ref[:, c:c + LANES] = (t * cos + _rope_partner(t) * sin).astype(BF16)
    conv_some(2)
    kvb = proj(wt_ref, T_KB, 2 * LANES)
    kb = _head_rms(kvb[:, :LANES], e64_ref[:LANES, :LANES], B_HEAD_DIM, knb_ref[...])
    kb_ref[...] = (kb * cos + _rope_partner(kb) * sin).astype(BF16)
    vb_ref[...] = kvb[:, LANES:].astype(BF16)
    conv_some(1)
    gb_ref[...] = _silu(proj(wt_ref, T_ZB, B_WIDTH)).astype(BF16)
    conv_some(2)
    qc_ref[...] = _head_rms(proj(wt_ref, T_QC, C_WIDTH), e128_ref[...], C_HEAD_DIM, qnc_ref[...]).astype(BF16)
    conv_some(2)
    gc_ref[...] = _silu(proj(wt_ref, T_ZC, C_WIDTH)).astype(BF16)
    conv_some(len(pending))
    ext_s[:, 0:CONV_HALO, :] = ext_s[:, tm:tm + CONV_HALO, :]


def _block_diag_ones(width, block):
    idx = np.arange(width) // block
    return jnp.asarray(idx[:, None] == idx[None, :], dtype=BF16)


def _rope_tables(seq):
    d = B_HEAD_DIM
    inv = ROPE_THETA ** (-jnp.arange(0, d, 2, dtype=F32) / d)
    ang = jnp.arange(seq, dtype=F32)[:, None] * inv[None, :]
    cos = jnp.cos(ang)
    sin = jnp.sin(ang)
    cos_h = jnp.concatenate([cos, cos], axis=-1)
    sin_h = jnp.concatenate([-sin, sin], axis=-1)
    reps = LANES // d
    return jnp.tile(cos_h, (1, reps)), jnp.tile(sin_h, (1, reps))


def _in_proj(x2, norm_w, w_a, w_g, w_t, conv_w, cos, sin, alog_row, dtb_row, qnb, knb, qnc, seq, tm=256):
    t = x2.shape[0]
    tiles_per_seq = seq // tm
    nt = t // tm
    row = lambda i: (jnp.minimum(i, nt - 1), 0)
    const = lambda i: (0, 0)
    pos = lambda i: (jnp.minimum(i, nt - 1) % tiles_per_seq, 0)
    in_specs = [
        pl.BlockSpec((tm, D_MODEL), row),
        pl.BlockSpec((1, D_MODEL), const),
        pl.BlockSpec((D_MODEL, A_COLS), const, pipeline_mode=pl.Buffered(1)),
        pl.BlockSpec((D_MODEL, LANES), const, pipeline_mode=pl.Buffered(1)),
        pl.BlockSpec((D_MODEL, T_COLS), const, pipeline_mode=pl.Buffered(1)),
        pl.BlockSpec((CONV_K, 1, 3 * A_WIDTH), lambda i: (0, 0, 0)),
        pl.BlockSpec((tm, LANES), pos),
        pl.BlockSpec((tm, LANES), pos),
        pl.BlockSpec((1, LANES), const),
        pl.BlockSpec((1, LANES), const),
        pl.BlockSpec((1, B_WIDTH), const),
        pl.BlockSpec((1, LANES), const),
        pl.BlockSpec((1, C_WIDTH), const),
        pl.BlockSpec((B_WIDTH, B_WIDTH), const),
        pl.BlockSpec((C_WIDTH, C_WIDTH), const),
    ]
    out_shape = [
        jax.ShapeDtypeStruct((t + tm, 3 * A_WIDTH), BF16),
        jax.ShapeDtypeStruct((t, A_WIDTH), BF16),
        jax.ShapeDtypeStruct((t, LANES), F32),
        jax.ShapeDtypeStruct((t // tm, LANES, tm), F32),
        jax.ShapeDtypeStruct((t, B_WIDTH), BF16),
        jax.ShapeDtypeStruct((t, LANES), BF16),
        jax.ShapeDtypeStruct((t, LANES), BF16),
        jax.ShapeDtypeStruct((t, B_WIDTH), BF16),
        jax.ShapeDtypeStruct((t, C_WIDTH), BF16),
        jax.ShapeDtypeStruct((t, C_WIDTH), BF16),
    ]
    out_specs = [
        pl.BlockSpec((tm, 3 * A_WIDTH), lambda i: (i, 0)),
        pl.BlockSpec((tm, A_WIDTH), row),
        pl.BlockSpec((tm, LANES), row),
        pl.BlockSpec((1, LANES, tm), lambda i: (jnp.minimum(i, nt - 1), 0, 0)),
        pl.BlockSpec((tm, B_WIDTH), row),
        pl.BlockSpec((tm, LANES), row),
        pl.BlockSpec((tm, LANES), row),
        pl.BlockSpec((tm, B_WIDTH), row),
        pl.BlockSpec((tm, C_WIDTH), row),
        pl.BlockSpec((tm, C_WIDTH), row),
    ]
    return pl.pallas_call(
        functools.partial(_in_proj_kernel, tiles_per_seq),
        grid=(nt + 1,),
        in_specs=in_specs,
        out_specs=out_specs,
        out_shape=out_shape,
        scratch_shapes=[pltpu.VMEM((3 * A_HEADS, CONV_HALO + tm, A_HEAD_DIM), F32)],
        compiler_params=pltpu.CompilerParams(dimension_semantics=("arbitrary",), vmem_limit_bytes=VMEM_LIMIT),
        name="in_proj",
    )(x2, norm_w, w_a, w_g, w_t, conv_w, cos, sin, alog_row, dtb_row, qnb, knb, qnc,
      _block_diag_ones(B_WIDTH, B_HEAD_DIM), _block_diag_ones(C_WIDTH, C_HEAD_DIM))


def _mem_proj_kernel(m_ref, nw_ref, w_ref, knc_ref, e128_ref, km_ref, vm_ref):
    x = m_ref[...]
    ms = jnp.mean(x * x, axis=-1, keepdims=True)
    mn = (x * lax.rsqrt(ms + EPS) * nw_ref[...]).astype(BF16)
    km = _dot(mn, w_ref[:, :C_WIDTH])
    km_ref[...] = _head_rms(km, e128_ref[...], C_HEAD_DIM, knc_ref[...]).astype(BF16)
    vm_ref[...] = _dot(mn, w_ref[:, C_WIDTH:]).astype(BF16)


def _mem_proj(mem2, mem_norm_w, w_kv, knc, tm=256):
    t = mem2.shape[0]
    row = lambda i: (i, 0)
    const = lambda i: (0, 0)
    return pl.pallas_call(
        _mem_proj_kernel,
        grid=(t // tm,),
        in_specs=[
            pl.BlockSpec((tm, D_MODEL), row),
            pl.BlockSpec((1, D_MODEL), const),
            pl.BlockSpec((D_MODEL, 2 * C_WIDTH), const),
            pl.BlockSpec((1, C_WIDTH), const),
            pl.BlockSpec((C_WIDTH, C_WIDTH), const),
        ],
        out_specs=[pl.BlockSpec((tm, C_WIDTH), row), pl.BlockSpec((tm, C_WIDTH), row)],
        out_shape=[jax.ShapeDtypeStruct((t, C_WIDTH), BF16), jax.ShapeDtypeStruct((t, C_WIDTH), BF16)],
        compiler_params=pltpu.CompilerParams(dimension_semantics=("arbitrary",), vmem_limit_bytes=VMEM_LIMIT),
        name="mem_proj",
    )(mem2, mem_norm_w, w_kv, knc, _block_diag_ones(C_WIDTH, C_HEAD_DIM))


HEADS_PER_STEP = 2
PACK = 4
PACK_ROWS = PACK * CHUNK
SQUARINGS = 5
PREP_PACKS = 2


def _deltanet_kernel(q_ref, k_ref, v_ref, qt_ref, kt_ref, vt_ref, ab_ref, gfr_ref, gbr_ref, ga_ref, on_ref,
                     out_ref,
                     qs, ks, vs, kq_s, b_s, cd_s, o_s):
    seq = q_ref.shape[0]
    nc = seq // CHUNK
    hb = HEADS_PER_STEP
    h0 = pl.program_id(1) * hb
    slab = 256

    def slab_rows(i):
        return pl.ds(pl.multiple_of(i * slab, slab), slab)

    def stage(src_ref, tail_ref, dst_ref, hs):
        cols = slice(hs * LANES, (hs + 1) * LANES)

        def copy(i, carry):
            src_rows = pl.ds(pl.multiple_of(CONV_SHIFT + i * slab, CONV_SHIFT), slab)
            dst_ref[hs, slab_rows(i), :] = src_ref[src_rows, cols].astype(F32)
            return carry

        lax.fori_loop(0, seq // slab - 1, copy, 0)
        last = seq - slab
        dst_ref[hs, last:seq - CONV_SHIFT, :] = src_ref[last + CONV_SHIFT:seq, cols].astype(F32)
        dst_ref[hs, seq - CONV_SHIFT:seq, :] = tail_ref[:, cols].astype(F32)

    for hs in range(hb):
        stage(q_ref, qt_ref, qs, hs)
        stage(k_ref, kt_ref, ks, hs)
        stage(v_ref, vt_ref, vs, hs)

    ii = lax.broadcasted_iota(jnp.int32, (CHUNK, PACK_ROWS), 0)
    jl = lax.broadcasted_iota(jnp.int32, (CHUNK, PACK_ROWS), 1)
    lb = jl // CHUNK
    jj = jl % CHUNK
    block_diag = (lax.broadcasted_iota(jnp.int32, (PACK_ROWS, PACK_ROWS), 0) // CHUNK
                  == lax.broadcasted_iota(jnp.int32, (PACK_ROWS, PACK_ROWS), 1) // CHUNK)

    def pack_diag(full):
        out = full[0:CHUNK]
        for c in range(1, PACK):
            out = jnp.where(lb == c, full[c * CHUNK:(c + 1) * CHUNK], out)
        return out

    def pack_cols(col):
        out = None
        for c in range(PACK):
            blk = col[c * CHUNK:(c + 1) * CHUNK]
            blk = jnp.concatenate([blk, blk], axis=1)
            out = blk if out is None else jnp.where(lb == c, blk, out)
        return out

    def to_block_diag(packed):
        return jnp.where(block_diag, jnp.concatenate([packed] * PACK, axis=0), 0.0).astype(BF16)

    lane4 = lax.broadcasted_iota(jnp.int32, (PACK_ROWS, LANES), 1)
    npk = seq // PACK_ROWS
    chains = [(hs, d) for hs in range(hb) for d in range(2)]

    def prep_gen(it):
        members = []
        for pk in range(PREP_PACKS):
            for d in range(2):
                fwd = d == 0
                p = it * PREP_PACKS + pk
                if not fwd:
                    p = npk - 1 - p
                rows = pl.ds(pl.multiple_of(p * PACK_ROWS, PACK_ROWS), PACK_ROWS)
                ab4 = ab_ref[rows, :]

                def gate_col(col, ab4=ab4):
                    c = jnp.sum(jnp.where(lane4 == col, ab4, 0.0), axis=-1, keepdims=True)
                    return jnp.broadcast_to(c, (PACK_ROWS, LANES))

                incl = (ii >= jj) if fwd else (ii <= jj)
                strict = (ii > jj) if fwd else (ii < jj)
                for hs in range(hb):
                    k4 = ks[hs, rows, :]
                    q4 = qs[hs, rows, :]
                    v4 = vs[hs, rows, :]
                    k4b = k4.astype(BF16)
                    kk_p = pack_diag(_dot_nt(k4b, k4b))
                    qk_p = pack_diag(_dot_nt(q4.astype(BF16), k4b))
                    gc4 = gate_col(d * A_HEADS + h0 + hs)
                    bt4 = gate_col((2 + d) * A_HEADS + h0 + hs)
                    gcr = (gfr_ref if fwd else gbr_ref)[p, pl.ds(h0 + hs, 1), :]
                    decay = jnp.where(incl, jnp.exp(jnp.where(incl, pack_cols(gc4) - gcr, 0.0)), 0.0)
                    n_cat = jnp.where(strict, -(kk_p * decay * pack_cols(bt4)), 0.0)
                    qkm = jnp.where(incl, qk_p * decay, 0.0)
                    members.append((hs, d, k4, q4, v4, gc4, bt4, n_cat, qkm, p, rows))
        yield

        eye = jnp.where(ii == jj, 1.0, 0.0)
        tinv = [eye + m[7] for m in members]
        pw = [_dot(m[7].astype(BF16), to_block_diag(m[7])) for m in members]
        yield
        for level in range(1, SQUARINGS + 1):
            last = level == SQUARINGS
            for i in range(len(members)):
                rhs = to_block_diag(pw[i])
                if last:
                    tinv[i] = tinv[i] + _dot(tinv[i].astype(BF16), rhs)
                else:
                    both = _dot(jnp.concatenate([pw[i], tinv[i]], axis=0).astype(BF16), rhs)
                    pw[i] = both[:CHUNK]
                    tinv[i] = tinv[i] + both[CHUNK:]
            yield

        uws = []
        for (hs, d, k4, q4, v4, gc4, bt4, _, qkm, p, rows), t_p in zip(members, tinv):
            eg = jnp.exp(gc4)
            rhs = jnp.concatenate([v4 * bt4, k4 * bt4 * eg], axis=1).astype(BF16)
            uws.append((_dot(to_block_diag(t_p), rhs), eg))
        yield

        folded = []
        for (hs, d, k4, q4, v4, gc4, bt4, _, qkm, p, rows), (uw, eg) in zip(members, uws):
            fwd = d == 0
            uwb = uw.astype(BF16)
            per_chunk = []
            for c in range(PACK):
                rs = slice(c * CHUNK, (c + 1) * CHUNK)
                last_row = (c + 1) * CHUNK - 1 if fwd else c * CHUNK
                gl = gc4[last_row:last_row + 1, :]
                kd = k4[rs] * jnp.exp(gl - gc4[rs])
                lhs = jnp.concatenate([qkm[:, c * CHUNK:(c + 1) * CHUNK], kd.T], axis=0).astype(BF16)
                per_chunk.append((_dot(lhs, uwb[rs]), gl))
            folded.append(per_chunk)
        yield

        for (hs, d, k4, q4, v4, gc4, bt4, _, qkm, p, rows), (uw, eg), per_chunk in zip(members, uws, folded):
            qd = q4 * eg
            for c, (res, gl) in enumerate(per_chunk):
                n = p * PACK + c
                rs = slice(c * CHUNK, (c + 1) * CHUNK)
                o_s[hs, d, pl.ds(pl.multiple_of(n * CHUNK, CHUNK), CHUNK), :] = res[:CHUNK, :LANES]
                kq_s[hs, d, n, 0:LANES, :] = res[CHUNK:, LANES:].astype(BF16)
                kq_s[hs, d, n, LANES:LANES + CHUNK, :] = (qd[rs] - res[:CHUNK, LANES:]).astype(BF16)
                b_s[hs, d, n] = res[CHUNK:, :LANES]
                cd_s[hs, d, n] = jnp.broadcast_to(jnp.exp(gl), (8, LANES))

    steps_per_stage = PREP_PACKS * PACK

    def seq_gen(it, states, result):
        states = list(states)
        for k in range(steps_per_stage):
            i = it * steps_per_stage + k
            ns = (i, nc - 1 - i)
            rows = [pl.ds(pl.multiple_of(n * CHUNK, CHUNK), CHUNK) for n in ns]
            outs = [_dot(kq_s[hs, d, ns[d]], st.astype(BF16)) for (hs, d), st in zip(chains, states)]
            for c, ((hs, d), out) in enumerate(zip(chains, outs)):
                o_s[hs, d, rows[d], :] = o_s[hs, d, rows[d], :] + out[LANES:]
                states[c] = states[c] * cd_s[hs, d, ns[d]][0:1, :] - out[:LANES] + b_s[hs, d, ns[d]]
            yield
        result.append(tuple(states))

    def run(gen):
        for _ in gen:
            pass

    def fused_body(j, states):
        result = []
        pg = prep_gen(j + 1)
        sg = seq_gen(j, states, result)
        next(pg)
        for _ in range(steps_per_stage):
            next(pg)
            next(sg)
        run(sg)
        run(pg)
        return result[0]

    n_stages = npk // PREP_PACKS
    s0 = jnp.zeros((A_HEAD_DIM, A_HEAD_DIM), F32)
    run(prep_gen(0))
    states = lax.fori_loop(0, n_stages - 1, fused_body, tuple(s0 for _ in chains))
    run(seq_gen(n_stages - 1, states, []))

    def finish(i, carry):
        for hs in range(hb):
            cols = slice(hs * LANES, (hs + 1) * LANES)
            o = o_s[hs, 0, slab_rows(i), :] + o_s[hs, 1, slab_rows(i), :]
            y = o * lax.rsqrt(jnp.mean(o * o, axis=-1, keepdims=True) + EPS) * on_ref[...]
            out_ref[slab_rows(i), cols] = (y * ga_ref[slab_rows(i), cols].astype(F32)).astype(BF16)
        return carry

    lax.fori_loop(0, seq // slab, finish, 0, unroll=2)


def _deltanet(qkv, ab, abt3, ga, o_norm, batch, seq):
    nc = seq // CHUNK
    npk = seq // PACK_ROWS
    t = batch * seq
    hb = HEADS_PER_STEP
    hw = hb * LANES
    groups = A_HEADS // hb
    tail_blocks = seq // CONV_SHIFT
    scratch = [
        pltpu.VMEM((hb, seq, LANES), F32),
        pltpu.VMEM((hb, seq, LANES), F32),
        pltpu.VMEM((hb, seq, LANES), F32),
        pltpu.VMEM((hb, 2, nc, LANES + CHUNK, LANES), BF16),
        pltpu.VMEM((hb, 2, nc, LANES, LANES), F32),
        pltpu.VMEM((hb, 2, nc, 8, LANES), F32),
        pltpu.VMEM((hb, 2, seq, LANES), F32),
    ]
    return pl.pallas_call(
        _deltanet_kernel,
        grid=(batch, groups),
        in_specs=[
            pl.BlockSpec((seq, hw), lambda b, g: (b, g)),
            pl.BlockSpec((seq, hw), lambda b, g: (b, groups + g)),
            pl.BlockSpec((seq, hw), lambda b, g: (b, 2 * groups + g)),
            pl.BlockSpec((CONV_SHIFT, hw), lambda b, g: ((b + 1) * tail_blocks, g)),
            pl.BlockSpec((CONV_SHIFT, hw), lambda b, g: ((b + 1) * tail_blocks, groups + g)),
            pl.BlockSpec((CONV_SHIFT, hw), lambda b, g: ((b + 1) * tail_blocks, 2 * groups + g)),
            pl.BlockSpec((seq, LANES), lambda b, g: (b, 0)),
            pl.BlockSpec((npk, A_HEADS, PACK_ROWS), lambda b, g: (b, 0, 0)),
            pl.BlockSpec((npk, A_HEADS, PACK_ROWS), lambda b, g: (b, 1, 0)),
            pl.BlockSpec((seq, hw), lambda b, g: (b, g)),
            pl.BlockSpec((1, LANES), lambda b, g: (0, 0)),
        ],
        out_specs=pl.BlockSpec((seq, hw), lambda b, g: (b, g)),
        out_shape=jax.ShapeDtypeStruct((t, A_WIDTH), BF16),
        scratch_shapes=scratch,
        compiler_params=pltpu.CompilerParams(dimension_semantics=("arbitrary", "arbitrary"),
                                             vmem_limit_bytes=VMEM_LIMIT),
        name="deltanet",
    )(qkv, qkv, qkv, qkv, qkv, qkv, ab, abt3, abt3, ga, o_norm)


Q_BLOCKS = 4


def _attn_kernel(sink_ref, qb_ref, kb_ref, vb_ref, gb_ref, qc_ref, km_ref, vm_ref, gc_ref, out_ref):
    nb = pl.num_programs(1) * Q_BLOCKS
    w = WINDOW
    lane = lax.broadcasted_iota(jnp.int32, (w, LANES), 1)
    lower = lane < B_HEAD_DIM
    ri = lax.broadcasted_iota(jnp.int32, (2 * w, w), 0) % w
    ci = lax.broadcasted_iota(jnp.int32, (2 * w, w), 1)
    top = lax.broadcasted_iota(jnp.int32, (2 * w, 1), 0) < w

    first_blk = pl.program_id(1) * Q_BLOCKS - 1
    kv = []
    for i in range(Q_BLOCKS + 2):
        j = first_blk + i
        start = pl.multiple_of(jnp.clip(j, 0, nb - 1) * w, w)
        kblk = kb_ref[pl.ds(start, w), :].astype(F32)
        vblk = vb_ref[pl.ds(start, w), :].astype(F32)
        ksw = pltpu.roll(kblk, B_HEAD_DIM, axis=1)
        vsw = pltpu.roll(vblk, B_HEAD_DIM, axis=1)
        variants = {}
        for hk in range(B_KV_HEADS):
            for half in range(2):
                keep = lower if half == 0 else jnp.logical_not(lower)
                in_place = hk == half
                variants[hk, half] = (jnp.where(keep, kblk if in_place else ksw, 0.0).astype(BF16),
                                      jnp.where(keep, vblk if in_place else vsw, 0.0).astype(BF16))
        kv.append((variants, jnp.logical_and(j >= 0, j < nb)))

    for qi in range(Q_BLOCKS):
        rows = slice(qi * w, (qi + 1) * w)
        masks = (jnp.logical_and(ci >= ri, kv[qi][1]), None, jnp.logical_and(ci <= ri, kv[qi + 2][1]))
        for hk in range(B_KV_HEADS):
            c0 = hk * 2 * LANES
            q2 = jnp.concatenate([qb_ref[rows, c0:c0 + LANES], qb_ref[rows, c0 + LANES:c0 + 2 * LANES]], axis=0)
            acc = jnp.zeros((2 * w, LANES), F32)
            for half in range(2):
                sink_col = jnp.where(top, sink_ref[4 * hk + half], sink_ref[4 * hk + 2 + half])
                ss = []
                for rel in range(3):
                    s = _dot_nt(q2, kv[qi + rel][0][hk, half][0])
                    if masks[rel] is not None:
                        s = jnp.where(masks[rel], s, -jnp.inf)
                    ss.append(s)
                m = jnp.maximum(sink_col, jnp.max(jnp.maximum(jnp.maximum(ss[0], ss[1]), ss[2]),
                                                  axis=-1, keepdims=True))
                ps = [jnp.exp(s - m) for s in ss]
                den = jnp.exp(sink_col - m) + jnp.sum(ps[0] + ps[1] + ps[2], axis=-1, keepdims=True)
                pv = jnp.zeros((2 * w, LANES), F32)
                for rel in range(3):
                    pv = pv + _dot(ps[rel].astype(BF16), kv[qi + rel][0][hk, half][1])
                acc = acc + pv / den
            out_ref[rows, c0:c0 + LANES] = (acc[:w] * gb_ref[rows, c0:c0 + LANES].astype(F32)).astype(BF16)
            out_ref[rows, c0 + LANES:c0 + 2 * LANES] = (
                acc[w:] * gb_ref[rows, c0 + LANES:c0 + 2 * LANES].astype(F32)).astype(BF16)

    for hc in range(C_HEADS):
        c0 = hc * C_HEAD_DIM
        s = _dot_nt(qc_ref[:, c0:c0 + C_HEAD_DIM], km_ref[:, c0:c0 + C_HEAD_DIM])
        m = jnp.max(s, axis=-1, keepdims=True)
        p = jnp.exp(s - m)
        den = jnp.sum(p, axis=-1, keepdims=True)
        o = _dot(p.astype(BF16), vm_ref[:, c0:c0 + C_HEAD_DIM]) / den
        out_ref[:, B_WIDTH + c0:B_WIDTH + c0 + C_HEAD_DIM] = (
            o * gc_ref[:, c0:c0 + C_HEAD_DIM].astype(F32)).astype(BF16)


def _attention(sink, qb, kb, vb, gb, qc, km, vm, gcg, batch, seq):
    nb = seq // (WINDOW * Q_BLOCKS)
    qw = WINDOW * Q_BLOCKS
    t = batch * seq
    mlen = km.shape[0] // batch
    qrow = lambda b, n: (b * nb + n, 0)
    per_b = lambda b, n: (b, 0)
    return pl.pallas_call(
        _attn_kernel,
        grid=(batch, nb),
        in_specs=[
            pl.BlockSpec(memory_space=pltpu.SMEM),
            pl.BlockSpec((qw, B_WIDTH), qrow),
            pl.BlockSpec((seq, LANES), per_b),
            pl.BlockSpec((seq, LANES), per_b),
            pl.BlockSpec((qw, B_WIDTH), qrow),
            pl.BlockSpec((qw, C_WIDTH), qrow),
            pl.BlockSpec((mlen, C_WIDTH), per_b),
            pl.BlockSpec((mlen, C_WIDTH), per_b),
            pl.BlockSpec((qw, C_WIDTH), qrow),
        ],
        out_specs=pl.BlockSpec((qw, B_WIDTH + C_WIDTH), qrow),
        out_shape=jax.ShapeDtypeStruct((t, B_WIDTH + C_WIDTH), BF16),
        compiler_params=pltpu.CompilerParams(dimension_semantics=("arbitrary", "arbitrary"),
                                             vmem_limit_bytes=VMEM_LIMIT),
        name="attention",
    )(sink, qb, kb, vb, gb, qc, km, vm, gcg)


def _out_proj_kernel(x_ref, ma_ref, mbc_ref, wa_ref, wbc_ref, out_ref):
    out_ref[...] = x_ref[...] + _dot(ma_ref[...], wa_ref[...]) + _dot(mbc_ref[...], wbc_ref[...])


def _out_proj(x2, mixed_a, mixed_bc, w_out, tm=512):
    t = x2.shape[0]
    row = lambda i: (i, 0)
    const = lambda i: (0, 0)
    return pl.pallas_call(
        _out_proj_kernel,
        grid=(t // tm,),
        in_specs=[
            pl.BlockSpec((tm, D_MODEL), row),
            pl.BlockSpec((tm, A_WIDTH), row),
            pl.BlockSpec((tm, B_WIDTH + C_WIDTH), row),
            pl.BlockSpec((A_WIDTH, D_MODEL), const),
            pl.BlockSpec((B_WIDTH + C_WIDTH, D_MODEL), lambda i: (1, 0)),
        ],
        out_specs=pl.BlockSpec((tm, D_MODEL), row),
        out_shape=jax.ShapeDtypeStruct((t, D_MODEL), F32),
        compiler_params=pltpu.CompilerParams(dimension_semantics=("arbitrary",), vmem_limit_bytes=VMEM_LIMIT),
        name="out_proj",
    )(x2, mixed_a, mixed_bc, w_out, w_out)


def _layer(h, mem, norm_w, w_in, conv_w_a, a_log_fwd, a_log_bwd, dt_bias_fwd, dt_bias_bwd, o_norm_a,
           q_norm_b, k_norm_b, sink_b, mem_norm_w, w_mem_kv, q_norm_c, k_norm_c, w_out):
    batch, seq, _ = h.shape
    t = batch * seq
    x2 = h.reshape(t, D_MODEL)

    w_a, w_g, w_t = _w_prep(w_in)
    zeros = jnp.zeros((LANES - 2 * A_HEADS,), F32)
    alog_row = jnp.concatenate([a_log_fwd, a_log_bwd, zeros]).reshape(1, LANES)
    dtb_row = jnp.concatenate([dt_bias_fwd, dt_bias_bwd, zeros]).reshape(1, LANES)
    cos, sin = _rope_tables(seq)

    qkv, ga, ab, abt, qb, kb, vb, gb, qc, gcg = _in_proj(
        x2, norm_w.reshape(1, D_MODEL), w_a, w_g, w_t, conv_w_a, cos, sin, alog_row, dtb_row,
        (jnp.tile(q_norm_b, B_Q_HEADS) * B_HEAD_DIM ** -0.5).reshape(1, B_WIDTH),
        jnp.tile(k_norm_b, B_KV_HEADS).reshape(1, LANES),
        (jnp.tile(q_norm_c, C_HEADS) * C_HEAD_DIM ** -0.5).reshape(1, C_WIDTH), seq)

    mlen = mem.shape[1]
    km, vm = _mem_proj(mem.reshape(batch * mlen, D_MODEL), mem_norm_w.reshape(1, D_MODEL),
                       w_mem_kv.astype(BF16), jnp.tile(k_norm_c, C_HEADS).reshape(1, C_WIDTH))

    mixed_a = _deltanet(qkv, ab, abt, ga,
                        o_norm_a.reshape(1, A_HEAD_DIM), batch, seq)
    mixed_bc = _attention(sink_b, qb, kb, vb, gb, qc, km, vm, gcg, batch, seq)

    out = _out_proj(x2, mixed_a, mixed_bc, w_out.astype(BF16))
    return out.reshape(batch, seq, D_MODEL)


def kernel(x, mem, norm_w, w_in, conv_w_a, a_log_fwd, a_log_bwd, dt_bias_fwd, dt_bias_bwd, o_norm_a,
           q_norm_b, k_norm_b, sink_b, mem_norm_w, w_mem_kv, q_norm_c, k_norm_c, w_out):
    h = x
    conv_taps = jnp.swapaxes(conv_w_a, 0, 1)
    for l in range(norm_w.shape[0]):
        h = _layer(h, mem, norm_w[l], w_in[l], conv_taps[:, l:l + 1], a_log_fwd[l], a_log_bwd[l], dt_bias_fwd[l],
                   dt_bias_bwd[l], o_norm_a[l], q_norm_b[l], k_norm_b[l], sink_b[l], mem_norm_w[l],
                   w_mem_kv[l], q_norm_c[l], k_norm_c[l], w_out[l])
    return h
```

```python
import functools

import numpy as np
import jax
import jax.numpy as jnp
from jax import lax
from jax.experimental import pallas as pl
from jax.experimental.pallas import tpu as pltpu

F32 = jnp.float32
BF16 = jnp.bfloat16

D_MODEL = 2048
A_WIDTH = 1024
A_HEAD_DIM = 128
A_HEADS = 8
CONV_K = 5
CHUNK = 64
B_WIDTH = 512
B_HEAD_DIM = 64
B_Q_HEADS = 8
B_KV_HEADS = 2
WINDOW = 128
C_WIDTH = 512
C_HEADS = 4
C_HEAD_DIM = 128
ROPE_THETA = 10000.0
EPS = 1e-6

LANES = 128
GATE_COLS = 4 * A_HEADS
A_COLS = 4 * A_WIDTH
T_QB = 0
T_KB = T_QB + B_WIDTH
T_VB = T_KB + B_KV_HEADS * B_HEAD_DIM
T_ZB = T_VB + B_KV_HEADS * B_HEAD_DIM
T_QC = T_ZB + B_WIDTH
T_ZC = T_QC + C_WIDTH
T_COLS = T_ZC + C_WIDTH

VMEM_LIMIT = 56 * 1024 * 1024
CONV_SHIFT = 16
CONV_HALO = 24
CONV_COLS = 256
CONV_ROWS = 128

_NT = (((1,), (1,)), ((), ()))
_TN = (((0,), (0,)), ((), ()))


def _dot(a, b):
    return jnp.dot(a, b, preferred_element_type=F32)


def _dot_nt(a, b):
    return lax.dot_general(a, b, _NT, preferred_element_type=F32)


def _silu(z):
    return z * jax.nn.sigmoid(z)


def _softplus(z):
    return jnp.maximum(z, 0.0) + jnp.log1p(jnp.exp(-jnp.abs(z)))


def _rope_partner(x):
    lane = lax.broadcasted_iota(jnp.int32, x.shape, 1)
    first_half = (lane % B_HEAD_DIM) < (B_HEAD_DIM // 2)
    up = pltpu.roll(x, LANES - B_HEAD_DIM // 2, axis=1)
    down = pltpu.roll(x, B_HEAD_DIM // 2, axis=1)
    return jnp.where(first_half, up, down)


def _head_rms(x, ones_blockdiag, head_dim, w):
    ss = _dot((x * x).astype(BF16), ones_blockdiag)
    return x * lax.rsqrt(ss * (1.0 / head_dim) + EPS) * w


def _w_main_kernel(w_ref, o_ref):
    o_ref[...] = w_ref[...].T.astype(BF16)


def _w_rest_kernel(w_ref, wg_ref, wt_ref):
    g = w_ref[0:GATE_COLS, :].T
    wg_ref[...] = jnp.concatenate([g, jnp.zeros((g.shape[0], LANES - GATE_COLS), F32)], axis=1).astype(BF16)
    step = 256
    for c in range(0, T_COLS, step):
        wt_ref[:, c:c + step] = w_ref[GATE_COLS + c:GATE_COLS + c + step, :].T.astype(BF16)


def _w_prep(w_t, tr=256):
    cols = w_t.shape[1]
    w_a = pl.pallas_call(
        _w_main_kernel,
        grid=(A_COLS // tr,),
        in_specs=[pl.BlockSpec((tr, cols), lambda i: (i, 0))],
        out_specs=pl.BlockSpec((cols, tr), lambda i: (0, i)),
        out_shape=jax.ShapeDtypeStruct((cols, A_COLS), BF16),
        compiler_params=pltpu.CompilerParams(dimension_semantics=("arbitrary",), vmem_limit_bytes=VMEM_LIMIT),
        name="w_main",
    )(w_t)
    rest = w_t[A_COLS:]
    w_g, w_tail = pl.pallas_call(
        _w_rest_kernel,
        grid=(1,),
        in_specs=[pl.BlockSpec(rest.shape, lambda i: (0, 0), pipeline_mode=pl.Buffered(1))],
        out_specs=[pl.BlockSpec((cols, LANES), lambda i: (0, 0)), pl.BlockSpec((cols, T_COLS), lambda i: (0, 0))],
        out_shape=[jax.ShapeDtypeStruct((cols, LANES), BF16), jax.ShapeDtypeStruct((cols, T_COLS), BF16)],
        compiler_params=pltpu.CompilerParams(dimension_semantics=("arbitrary",), vmem_limit_bytes=VMEM_LIMIT),
        name="w_rest",
    )(rest)
    return w_a, w_g, w_tail


def _in_proj_kernel(tiles_per_seq, x_ref, nw_ref, wa_ref, wg_ref, wt_ref, cw_ref, cos_ref, sin_ref, alog_ref,
                    dtb_ref, qnb_ref, knb_ref, qnc_ref, e64_ref,
                    qkv_ref, ga_ref, ab_ref, abt_ref, qb_ref, kb_ref, vb_ref, gb_ref, qc_ref, gc_ref,
                    ext_s):
    step_id = pl.program_id(0)

    @pl.when(step_id == 0)
    def _():
        ext_s[:, 0:CONV_HALO, :] = jnp.zeros((ext_s.shape[0], CONV_HALO, A_HEAD_DIM), F32)

    seq_start = (step_id % tiles_per_seq) == 0
    zero = jnp.minimum(step_id, 0)
    x = x_ref[...]
    ms = jnp.mean(x * x, axis=-1, keepdims=True)
    hn = (x * lax.rsqrt(ms + EPS) * nw_ref[...]).astype(BF16)
    tm = x.shape[0]

    def proj(w_ref, c0, width):
        return _dot(hn, w_ref[:, c0:c0 + width])

    fix_rows = lax.broadcasted_iota(jnp.int32, (2 * 8, 1), 0) + (CONV_SHIFT - 8)
    first = CONV_HALO - CONV_SHIFT - CONV_K // 2
    lo = CONV_SHIFT - 8

    def conv_head(c):
        lanes = slice(c, c + A_HEAD_DIM)
        head = c // A_HEAD_DIM
        w = cw_ref[:, 0, lanes]
        taps = [w[j:j + 1, :] for j in range(CONV_K)]
        scale = A_HEAD_DIM ** -0.5 if c < A_WIDTH else 1.0
        for r in range(0, tm, CONV_ROWS):
            y = ext_s[head, pl.ds(zero + (first + r), CONV_ROWS), :] * taps[0]
            for j in range(1, CONV_K):
                y = y + ext_s[head, pl.ds(zero + (first + r + j), CONV_ROWS), :] * taps[j]
            if r == 0:
                cur0 = ext_s[head, CONV_HALO:CONV_HALO + 1, :]
                cur1 = ext_s[head, CONV_HALO + 1:CONV_HALO + 2, :]
                old0 = ext_s[head, CONV_HALO - 2:CONV_HALO - 1, :]
                old1 = ext_s[head, CONV_HALO - 1:CONV_HALO, :]
                cross = (jnp.where(fix_rows == CONV_SHIFT - 2, taps[4] * cur0, 0.0)
                         + jnp.where(fix_rows == CONV_SHIFT - 1, taps[3] * cur0 + taps[4] * cur1, 0.0)
                         + jnp.where(fix_rows == CONV_SHIFT, taps[0] * old0 + taps[1] * old1, 0.0)
                         + jnp.where(fix_rows == CONV_SHIFT + 1, taps[0] * old1, 0.0))
                fixed = y[lo:lo + 16] - jnp.where(seq_start, cross, 0.0)
                y = jnp.concatenate([y[:lo], fixed, y[lo + 16:]], axis=0)
            y = _silu(y)
            if c < 2 * A_WIDTH:
                y = y * (lax.rsqrt(jnp.sum(y * y, axis=-1, keepdims=True) + EPS) * scale)
            qkv_ref[r:r + CONV_ROWS, lanes] = y.astype(BF16)

    pending = list(range(0, 3 * A_WIDTH, A_HEAD_DIM))

    def conv_some(n):
        for _ in range(n):
            if pending:
                conv_head(pending.pop(0))

    for c in range(0, 3 * A_WIDTH, CONV_COLS):
        acc = proj(wa_ref, c, CONV_COLS)
        for hc in range(0, CONV_COLS, A_HEAD_DIM):
            ext_s[(c + hc) // A_HEAD_DIM, CONV_HALO:CONV_HALO + tm, :] = acc[:, hc:hc + A_HEAD_DIM]
        if c >= CONV_COLS:
            conv_some(1)

    step = 512
    for c in range(0, A_WIDTH, step):
        ga_ref[:, c:c + step] = _silu(proj(wa_ref, 3 * A_WIDTH + c, step)).astype(BF16)
        conv_some(2)

    acc = _dot(hn, wg_ref[...])
    lane = lax.broadcasted_iota(jnp.int32, (tm, LANES), 1)
    g = -jnp.exp(alog_ref[...]) * _softplus(acc + dtb_ref[...])
    val = jnp.where(lane < 2 * A_HEADS, g, jax.nn.sigmoid(acc))
    val = jnp.where(lane < GATE_COLS, val, 0.0)
    pos = lax.broadcasted_iota(jnp.int32, (tm, LANES), 0) % CHUNK
    pre = val
    suf = val
    s = 1
    while s < CHUNK:
        pre = pre + jnp.where(pos >= s, pltpu.roll(pre, s, axis=0), 0.0)
        suf = suf + jnp.where(pos < CHUNK - s, pltpu.roll(suf, tm - s, axis=0), 0.0)
        s *= 2
    res = jnp.where(lane < A_HEADS, pre, jnp.where(lane < 2 * A_HEADS, suf, val))
    ab_ref[...] = res
    abt_ref[0] = res.T
    conv_some(1)

    cos = cos_ref[...]
    sin = sin_ref[...]
    qb = _head_rms(proj(wt_ref, T_QB, B_WIDTH), e64_ref[...], B_HEAD_DIM, qnb_ref[...])
    for c in range(0, B_WIDTH, LANES):
        t = qb[:, c:c + LANES]
        qb_ref[:, c:c + LANES] = (t * cos + _rope_partner(t) * sin).astype(BF16)
    conv_some(2)
    kvb = proj(wt_ref, T_KB, 2 * LANES)
    kb = _head_rms(kvb[:, :LANES], e64_ref[:LANES, :LANES], B_HEAD_DIM, knb_ref[...])
    kb_ref[...] = (kb * cos + _rope_partner(kb) * sin).astype(BF16)
    vb_ref[...] = kvb[:, LANES:].astype(BF16)
    conv_some(1)
    gb_ref[...] = _silu(proj(wt_ref, T_ZB, B_WIDTH)).astype(BF16)
    conv_some(2)
    qc = proj(wt_ref, T_QC, C_WIDTH)
    for c in range(0, C_WIDTH, C_HEAD_DIM):
        t = qc[:, c:c + C_HEAD_DIM]
        t = t * lax.rsqrt(jnp.mean(t * t, axis=-1, keepdims=True) + EPS) * qnc_ref[:, c:c + C_HEAD_DIM]
        qc_ref[:, c:c + C_HEAD_DIM] = t.astype(BF16)
    conv_some(2)
    gc_ref[...] = _silu(proj(wt_ref, T_ZC, C_WIDTH)).astype(BF16)
    conv_some(len(pending))
    ext_s[:, 0:CONV_HALO, :] = ext_s[:, tm:tm + CONV_HALO, :]


def _block_diag_ones(width, block):
    idx = np.arange(width) // block
    return jnp.asarray(idx[:, None] == idx[None, :], dtype=BF16)


def _rope_tables(seq):
    d = B_HEAD_DIM
    inv = ROPE_THETA ** (-jnp.arange(0, d, 2, dtype=F32) / d)
    ang = jnp.arange(seq, dtype=F32)[:, None] * inv[None, :]
    cos = jnp.cos(ang)
    sin = jnp.sin(ang)
    cos_h = jnp.concatenate([cos, cos], axis=-1)
    sin_h = jnp.concatenate([-sin, sin], axis=-1)
    reps = LANES // d
    return jnp.tile(cos_h, (1, reps)), jnp.tile(sin_h, (1, reps))


def _in_proj(x2, norm_w, w_a, w_g, w_t, conv_w, cos, sin, alog_row, dtb_row, qnb, knb, qnc, seq, tm=256):
    t = x2.shape[0]
    tiles_per_seq = seq // tm
    nt = t // tm
    row = lambda i: (jnp.minimum(i, nt - 1), 0)
    const = lambda i: (0, 0)
    pos = lambda i: (jnp.minimum(i, nt - 1) % tiles_per_seq, 0)
    in_specs = [
        pl.BlockSpec((tm, D_MODEL), row),
        pl.BlockSpec((1, D_MODEL), const),
        pl.BlockSpec((D_MODEL, A_COLS), const, pipeline_mode=pl.Buffered(1)),
        pl.BlockSpec((D_MODEL, LANES), const, pipeline_mode=pl.Buffered(1)),
        pl.BlockSpec((D_MODEL, T_COLS), const, pipeline_mode=pl.Buffered(1)),
        pl.BlockSpec((CONV_K, 1, 3 * A_WIDTH), lambda i: (0, 0, 0)),
        pl.BlockSpec((tm, LANES), pos),
        pl.BlockSpec((tm, LANES), pos),
        pl.BlockSpec((1, LANES), const),
        pl.BlockSpec((1, LANES), const),
        pl.BlockSpec((1, B_WIDTH), const),
        pl.BlockSpec((1, LANES), const),
        pl.BlockSpec((1, C_WIDTH), const),
        pl.BlockSpec((B_WIDTH, B_WIDTH), const),
    ]
    out_shape = [
        jax.ShapeDtypeStruct((t + tm, 3 * A_WIDTH), BF16),
        jax.ShapeDtypeStruct((t, A_WIDTH), BF16),
        jax.ShapeDtypeStruct((t, LANES), F32),
        jax.ShapeDtypeStruct((t // tm, LANES, tm), F32),
        jax.ShapeDtypeStruct((t, B_WIDTH), BF16),
        jax.ShapeDtypeStruct((t, LANES), BF16),
        jax.ShapeDtypeStruct((t, LANES), BF16),
        jax.ShapeDtypeStruct((t, B_WIDTH), BF16),
        jax.ShapeDtypeStruct((t, C_WIDTH), BF16),
        jax.ShapeDtypeStruct((t, C_WIDTH), BF16),
    ]
    out_specs = [
        pl.BlockSpec((tm, 3 * A_WIDTH), lambda i: (i, 0)),
        pl.BlockSpec((tm, A_WIDTH), row),
        pl.BlockSpec((tm, LANES), row),
        pl.BlockSpec((1, LANES, tm), lambda i: (jnp.minimum(i, nt - 1), 0, 0)),
        pl.BlockSpec((tm, B_WIDTH), row),
        pl.BlockSpec((tm, LANES), row),
        pl.BlockSpec((tm, LANES), row),
        pl.BlockSpec((tm, B_WIDTH), row),
        pl.BlockSpec((tm, C_WIDTH), row),
        pl.BlockSpec((tm, C_WIDTH), row),
    ]
    return pl.pallas_call(
        functools.partial(_in_proj_kernel, tiles_per_seq),
        grid=(nt + 1,),
        in_specs=in_specs,
        out_specs=out_specs,
        out_shape=out_shape,
        scratch_shapes=[pltpu.VMEM((3 * A_HEADS, CONV_HALO + tm, A_HEAD_DIM), F32)],
        compiler_params=pltpu.CompilerParams(dimension_semantics=("arbitrary",), vmem_limit_bytes=VMEM_LIMIT),
        name="in_proj",
    )(x2, norm_w, w_a, w_g, w_t, conv_w, cos, sin, alog_row, dtb_row, qnb, knb, qnc,
      _block_diag_ones(B_WIDTH, B_HEAD_DIM))


def _mem_proj_kernel(m_ref, nw_ref, w_ref, knc_ref, e128_ref, km_ref, vm_ref):
    x = m_ref[...]
    ms = jnp.mean(x * x, axis=-1, keepdims=True)
    mn = (x * lax.rsqrt(ms + EPS) * nw_ref[...]).astype(BF16)
    km = _dot(mn, w_ref[:, :C_WIDTH])
    km_ref[...] = _head_rms(km, e128_ref[...], C_HEAD_DIM, knc_ref[...]).astype(BF16)
    vm_ref[...] = _dot(mn, w_ref[:, C_WIDTH:]).astype(BF16)


def _mem_proj(mem2, mem_norm_w, w_kv, knc, tm=256):
    t = mem2.shape[0]
    row = lambda i: (i, 0)
    const = lambda i: (0, 0)
    return pl.pallas_call(
        _mem_proj_kernel,
        grid=(t // tm,),
        in_specs=[
            pl.BlockSpec((tm, D_MODEL), row),
            pl.BlockSpec((1, D_MODEL), const),
            pl.BlockSpec((D_MODEL, 2 * C_WIDTH), const),
            pl.BlockSpec((1, C_WIDTH), const),
            pl.BlockSpec((C_WIDTH, C_WIDTH), const),
        ],
        out_specs=[pl.BlockSpec((tm, C_WIDTH), row), pl.BlockSpec((tm, C_WIDTH), row)],
        out_shape=[jax.ShapeDtypeStruct((t, C_WIDTH), BF16), jax.ShapeDtypeStruct((t, C_WIDTH), BF16)],
        compiler_params=pltpu.CompilerParams(dimension_semantics=("arbitrary",), vmem_limit_bytes=VMEM_LIMIT),
        name="mem_proj",
    )(mem2, mem_norm_w, w_kv, knc, _block_diag_ones(C_WIDTH, C_HEAD_DIM))


HEADS_PER_STEP = 2
PACK = 4
PACK_ROWS = PACK * CHUNK
SQUARINGS = 5
PREP_PACKS = 2
PREP_LEVELS = SQUARINGS + 3


def _deltanet_kernel(q_ref, k_ref, v_ref, qt_ref, kt_ref, vt_ref, ab_ref, gfr_ref, gbr_ref, ga_ref, on_ref,
                     out_ref,
                     qs, ks, vs, kq_s, b_s, cd_s, o_s):
    seq = q_ref.shape[0]
    nc = seq // CHUNK
    hb = HEADS_PER_STEP
    h0 = pl.program_id(1) * hb
    slab = 256

    def slab_rows(i):
        return pl.ds(pl.multiple_of(i * slab, slab), slab)

    def stage(src_ref, tail_ref, dst_ref, hs):
        cols = slice(hs * LANES, (hs + 1) * LANES)

        def copy(i, carry):
            src_rows = pl.ds(pl.multiple_of(CONV_SHIFT + i * slab, CONV_SHIFT), slab)
            dst_ref[hs, slab_rows(i), :] = src_ref[src_rows, cols].astype(F32)
            return carry

        lax.fori_loop(0, seq // slab - 1, copy, 0)
        last = seq - slab
        dst_ref[hs, last:seq - CONV_SHIFT, :] = src_ref[last + CONV_SHIFT:seq, cols].astype(F32)
        dst_ref[hs, seq - CONV_SHIFT:seq, :] = tail_ref[:, cols].astype(F32)

    for hs in range(hb):
        stage(q_ref, qt_ref, qs, hs)
        stage(k_ref, kt_ref, ks, hs)
        stage(v_ref, vt_ref, vs, hs)

    ii = lax.broadcasted_iota(jnp.int32, (CHUNK, PACK_ROWS), 0)
    jl = lax.broadcasted_iota(jnp.int32, (CHUNK, PACK_ROWS), 1)
    lb = jl // CHUNK
    jj = jl % CHUNK
    block_diag = (lax.broadcasted_iota(jnp.int32, (PACK_ROWS, PACK_ROWS), 0) // CHUNK
                  == lax.broadcasted_iota(jnp.int32, (PACK_ROWS, PACK_ROWS), 1) // CHUNK)

    def pack_diag(full):
        out = full[0:CHUNK]
        for c in range(1, PACK):
            out = jnp.where(lb == c, full[c * CHUNK:(c + 1) * CHUNK], out)
        return out

    def pack_cols(col):
        out = None
        for c in range(PACK):
            blk = col[c * CHUNK:(c + 1) * CHUNK]
            blk = jnp.concatenate([blk, blk], axis=1)
            out = blk if out is None else jnp.where(lb == c, blk, out)
        return out

    def to_block_diag(packed):
        return jnp.where(block_diag, jnp.concatenate([packed] * PACK, axis=0), 0.0).astype(BF16)

    lane4 = lax.broadcasted_iota(jnp.int32, (PACK_ROWS, LANES), 1)
    npk = seq // PACK_ROWS
    chains = [(hs, d) for hs in range(hb) for d in range(2)]

    def prep_gen(it):
        members = []
        for pk in range(PREP_PACKS):
            for d in range(2):
                fwd = d == 0
                p = it * PREP_PACKS + pk
                if not fwd:
                    p = npk - 1 - p
                rows = pl.ds(pl.multiple_of(p * PACK_ROWS, PACK_ROWS), PACK_ROWS)
                ab4 = ab_ref[rows, :]

                def gate_col(col, ab4=ab4):
                    c = jnp.sum(jnp.where(lane4 == col, ab4, 0.0), axis=-1, keepdims=True)
                    return jnp.broadcast_to(c, (PACK_ROWS, LANES))

                incl = (ii >= jj) if fwd else (ii <= jj)
                strict = (ii > jj) if fwd else (ii < jj)
                for hs in range(hb):
                    k4 = ks[hs, rows, :]
                    q4 = qs[hs, rows, :]
                    v4 = vs[hs, rows, :]
                    k4b = k4.astype(BF16)
                    kk_p = pack_diag(_dot_nt(k4b, k4b))
                    qk_p = pack_diag(_dot_nt(q4.astype(BF16), k4b))
                    gc4 = gate_col(d * A_HEADS + h0 + hs)
                    bt4 = gate_col((2 + d) * A_HEADS + h0 + hs)
                    gcr = (gfr_ref if fwd else gbr_ref)[p, pl.ds(h0 + hs, 1), :]
                    decay = jnp.where(incl, jnp.exp(jnp.where(incl, pack_cols(gc4) - gcr, 0.0)), 0.0)
                    n_cat = jnp.where(strict, -(kk_p * decay * pack_cols(bt4)), 0.0)
                    qkm = jnp.where(incl, qk_p * decay, 0.0)
                    members.append((hs, d, k4, q4, v4, gc4, bt4, n_cat, qkm, p, rows))
        yield

        eye = jnp.where(ii == jj, 1.0, 0.0)
        tinv = [eye + m[7] for m in members]
        pw = [_dot(m[7].astype(BF16), to_block_diag(m[7])) for m in members]
        yield
        for level in range(1, SQUARINGS + 1):
            last = level == SQUARINGS
            for i in range(len(members)):
                rhs = to_block_diag(pw[i])
                if last:
                    tinv[i] = tinv[i] + _dot(tinv[i].astype(BF16), rhs)
                else:
                    both = _dot(jnp.concatenate([pw[i], tinv[i]], axis=0).astype(BF16), rhs)
                    pw[i] = both[:CHUNK]
                    tinv[i] = tinv[i] + both[CHUNK:]
            yield

        uws = []
        for (hs, d, k4, q4, v4, gc4, bt4, _, qkm, p, rows), t_p in zip(members, tinv):
            eg = jnp.exp(gc4)
            rhs = jnp.concatenate([v4 * bt4, k4 * bt4 * eg], axis=1).astype(BF16)
            uws.append((_dot(to_block_diag(t_p), rhs), eg))
        yield

        folded = []
        for (hs, d, k4, q4, v4, gc4, bt4, _, qkm, p, rows), (uw, eg) in zip(members, uws):
            fwd = d == 0
            uwb = uw.astype(BF16)
            per_chunk = []
            for c in range(PACK):
                rs = slice(c * CHUNK, (c + 1) * CHUNK)
                last_row = (c + 1) * CHUNK - 1 if fwd else c * CHUNK
                gl = gc4[last_row:last_row + 1, :]
                kd = k4[rs] * jnp.exp(gl - gc4[rs])
                lhs = jnp.concatenate([qkm[:, c * CHUNK:(c + 1) * CHUNK], kd.T], axis=0).astype(BF16)
                per_chunk.append((_dot(lhs, uwb[rs]), gl))
            folded.append(per_chunk)
        yield

        for (hs, d, k4, q4, v4, gc4, bt4, _, qkm, p, rows), (uw, eg), per_chunk in zip(members, uws, folded):
            qd = q4 * eg
            for c, (res, gl) in enumerate(per_chunk):
                n = p * PACK + c
                rs = slice(c * CHUNK, (c + 1) * CHUNK)
                o_s[hs, d, pl.ds(pl.multiple_of(n * CHUNK, CHUNK), CHUNK), :] = res[:CHUNK, :LANES]
                kq_s[hs, d, n, 0:LANES, :] = res[CHUNK:, LANES:].astype(BF16)
                kq_s[hs, d, n, LANES:LANES + CHUNK, :] = (qd[rs] - res[:CHUNK, LANES:]).astype(BF16)
                b_s[hs, d, n] = res[CHUNK:, :LANES]
                cd_s[hs, d, n] = jnp.broadcast_to(jnp.exp(gl), (8, LANES))

    steps_per_stage = PREP_PACKS * PACK

    def seq_gen(it, states, result):
        states = list(states)
        for k in range(steps_per_stage):
            i = it * steps_per_stage + k
            ns = (i, nc - 1 - i)
            rows = [pl.ds(pl.multiple_of(n * CHUNK, CHUNK), CHUNK) for n in ns]
            outs = [_dot(kq_s[hs, d, ns[d]], st.astype(BF16)) for (hs, d), st in zip(chains, states)]
            for c, ((hs, d), out) in enumerate(zip(chains, outs)):
                o_s[hs, d, rows[d], :] = o_s[hs, d, rows[d], :] + out[LANES:]
                states[c] = states[c] * cd_s[hs, d, ns[d]][0:1, :] - out[:LANES] + b_s[hs, d, ns[d]]
            yield
        result.append(tuple(states))

    def run(gen):
        for _ in gen:
            pass

    def fused_body(j, states):
        result = []
        pg = prep_gen(j + 1)
        sg = seq_gen(j, states, result)
        next(pg)
        for _ in range(steps_per_stage):
            for _ in range(PREP_LEVELS // steps_per_stage):
                next(pg)
            next(sg)
        run(sg)
        run(pg)
        return result[0]

    n_stages = npk // PREP_PACKS
    s0 = jnp.zeros((A_HEAD_DIM, A_HEAD_DIM), F32)
    run(prep_gen(0))
    states = lax.fori_loop(0, n_stages - 1, fused_body, tuple(s0 for _ in chains))
    run(seq_gen(n_stages - 1, states, []))

    def finish(i, carry):
        for hs in range(hb):
            cols = slice(hs * LANES, (hs + 1) * LANES)
            o = o_s[hs, 0, slab_rows(i), :] + o_s[hs, 1, slab_rows(i), :]
            y = o * lax.rsqrt(jnp.mean(o * o, axis=-1, keepdims=True) + EPS) * on_ref[...]
            out_ref[slab_rows(i), cols] = (y * ga_ref[slab_rows(i), cols].astype(F32)).astype(BF16)
        return carry

    lax.fori_loop(0, seq // slab, finish, 0, unroll=2)


def _deltanet(qkv, ab, abt3, ga, o_norm, batch, seq):
    nc = seq // CHUNK
    npk = seq // PACK_ROWS
    t = batch * seq
    hb = HEADS_PER_STEP
    hw = hb * LANES
    groups = A_HEADS // hb
    tail_blocks = seq // CONV_SHIFT
    scratch = [
        pltpu.VMEM((hb, seq, LANES), F32),
        pltpu.VMEM((hb, seq, LANES), F32),
        pltpu.VMEM((hb, seq, LANES), F32),
        pltpu.VMEM((hb, 2, nc, LANES + CHUNK, LANES), BF16),
        pltpu.VMEM((hb, 2, nc, LANES, LANES), F32),
        pltpu.VMEM((hb, 2, nc, 8, LANES), F32),
        pltpu.VMEM((hb, 2, seq, LANES), F32),
    ]
    return pl.pallas_call(
        _deltanet_kernel,
        grid=(batch, groups),
        in_specs=[
            pl.BlockSpec((seq, hw), lambda b, g: (b, g)),
            pl.BlockSpec((seq, hw), lambda b, g: (b, groups + g)),
            pl.BlockSpec((seq, hw), lambda b, g: (b, 2 * groups + g)),
            pl.BlockSpec((CONV_SHIFT, hw), lambda b, g: ((b + 1) * tail_blocks, g)),
            pl.BlockSpec((CONV_SHIFT, hw), lambda b, g: ((b + 1) * tail_blocks, groups + g)),
            pl.BlockSpec((CONV_SHIFT, hw), lambda b, g: ((b + 1) * tail_blocks, 2 * groups + g)),
            pl.BlockSpec((seq, LANES), lambda b, g: (b, 0)),
            pl.BlockSpec((npk, A_HEADS, PACK_ROWS), lambda b, g: (b, 0, 0)),
            pl.BlockSpec((npk, A_HEADS, PACK_ROWS), lambda b, g: (b, 1, 0)),
            pl.BlockSpec((seq, hw), lambda b, g: (b, g)),
            pl.BlockSpec((1, LANES), lambda b, g: (0, 0)),
        ],
        out_specs=pl.BlockSpec((seq, hw), lambda b, g: (b, g)),
        out_shape=jax.ShapeDtypeStruct((t, A_WIDTH), BF16),
        scratch_shapes=scratch,
        compiler_params=pltpu.CompilerParams(dimension_semantics=("arbitrary", "arbitrary"),
                                             vmem_limit_bytes=VMEM_LIMIT),
        name="deltanet",
    )(qkv, qkv, qkv, qkv, qkv, qkv, ab, abt3, abt3, ga, o_norm)


Q_BLOCKS = 4


def _attn_kernel(sink_ref, qb_ref, kb_ref, vb_ref, gb_ref, qc_ref, km_ref, vm_ref, gc_ref, out_ref):
    nb = pl.num_programs(1) * Q_BLOCKS
    w = WINDOW
    lane = lax.broadcasted_iota(jnp.int32, (w, LANES), 1)
    lower = lane < B_HEAD_DIM
    ri = lax.broadcasted_iota(jnp.int32, (2 * w, w), 0) % w
    ci = lax.broadcasted_iota(jnp.int32, (2 * w, w), 1)
    top = lax.broadcasted_iota(jnp.int32, (2 * w, 1), 0) < w

    first_blk = pl.program_id(1) * Q_BLOCKS - 1
    kv = []
    for i in range(Q_BLOCKS + 2):
        j = first_blk + i
        start = pl.multiple_of(jnp.clip(j, 0, nb - 1) * w, w)
        kblk = kb_ref[pl.ds(start, w), :].astype(F32)
        vblk = vb_ref[pl.ds(start, w), :].astype(F32)
        ksw = pltpu.roll(kblk, B_HEAD_DIM, axis=1)
        vsw = pltpu.roll(vblk, B_HEAD_DIM, axis=1)
        variants = {}
        for hk in range(B_KV_HEADS):
            for half in range(2):
                keep = lower if half == 0 else jnp.logical_not(lower)
                in_place = hk == half
                variants[hk, half] = (jnp.where(keep, kblk if in_place else ksw, 0.0).astype(BF16),
                                      jnp.where(keep, vblk if in_place else vsw, 0.0).astype(BF16))
        kv.append((variants, jnp.logical_and(j >= 0, j < nb)))

    for qi in range(Q_BLOCKS):
        rows = slice(qi * w, (qi + 1) * w)
        masks = (jnp.logical_and(ci >= ri, kv[qi][1]), None, jnp.logical_and(ci <= ri, kv[qi + 2][1]))
        for hk in range(B_KV_HEADS):
            c0 = hk * 2 * LANES
            q2 = jnp.concatenate([qb_ref[rows, c0:c0 + LANES], qb_ref[rows, c0 + LANES:c0 + 2 * LANES]], axis=0)
            acc = jnp.zeros((2 * w, LANES), F32)
            for half in range(2):
                sink_col = jnp.where(top, sink_ref[4 * hk + half], sink_ref[4 * hk + 2 + half])
                ss = []
                for rel in range(3):
                    s = _dot_nt(q2, kv[qi + rel][0][hk, half][0])
                    if masks[rel] is not None:
                        s = jnp.where(masks[rel], s, -jnp.inf)
                    ss.append(s)
                m = jnp.maximum(sink_col, jnp.max(jnp.maximum(jnp.maximum(ss[0], ss[1]), ss[2]),
                                                  axis=-1, keepdims=True))
                ps = [jnp.exp(s - m) for s in ss]
                den = jnp.exp(sink_col - m) + jnp.sum(ps[0] + ps[1] + ps[2], axis=-1, keepdims=True)
                pv = jnp.zeros((2 * w, LANES), F32)
                for rel in range(3):
                    pv = pv + _dot(ps[rel].astype(BF16), kv[qi + rel][0][hk, half][1])
                acc = acc + pv / den
            out_ref[rows, c0:c0 + LANES] = (acc[:w] * gb_ref[rows, c0:c0 + LANES].astype(F32)).astype(BF16)
            out_ref[rows, c0 + LANES:c0 + 2 * LANES] = (
                acc[w:] * gb_ref[rows, c0 + LANES:c0 + 2 * LANES].astype(F32)).astype(BF16)

    for hc in range(C_HEADS):
        c0 = hc * C_HEAD_DIM
        s = _dot_nt(qc_ref[:, c0:c0 + C_HEAD_DIM], km_ref[:, c0:c0 + C_HEAD_DIM])
        m = jnp.max(s, axis=-1, keepdims=True)
        p = jnp.exp(s - m)
        den = jnp.sum(p, axis=-1, keepdims=True)
        o = _dot(p.astype(BF16), vm_ref[:, c0:c0 + C_HEAD_DIM]) / den
        out_ref[:, B_WIDTH + c0:B_WIDTH + c0 + C_HEAD_DIM] = (
            o * gc_ref[:, c0:c0 + C_HEAD_DIM].astype(F32)).astype(BF16)


def _attention(sink, qb, kb, vb, gb, qc, km, vm, gcg, batch, seq):
    nb = seq // (WINDOW * Q_BLOCKS)
    qw = WINDOW * Q_BLOCKS
    t = batch * seq
    mlen = km.shape[0] // batch
    qrow = lambda b, n: (b * nb + n, 0)
    per_b = lambda b, n: (b, 0)
    return pl.pallas_call(
        _attn_kernel,
        grid=(batch, nb),
        in_specs=[
            pl.BlockSpec(memory_space=pltpu.SMEM),
            pl.BlockSpec((qw, B_WIDTH), qrow),
            pl.BlockSpec((seq, LANES), per_b),
            pl.BlockSpec((seq, LANES), per_b),
            pl.BlockSpec((qw, B_WIDTH), qrow),
            pl.BlockSpec((qw, C_WIDTH), qrow),
            pl.BlockSpec((mlen, C_WIDTH), per_b),
            pl.BlockSpec((mlen, C_WIDTH), per_b),
            pl.BlockSpec((qw, C_WIDTH), qrow),
        ],
        out_specs=pl.BlockSpec((qw, B_WIDTH + C_WIDTH), qrow),
        out_shape=jax.ShapeDtypeStruct((t, B_WIDTH + C_WIDTH), BF16),
        compiler_params=pltpu.CompilerParams(dimension_semantics=("arbitrary", "arbitrary"),
                                             vmem_limit_bytes=VMEM_LIMIT),
        name="attention",
    )(sink, qb, kb, vb, gb, qc, km, vm, gcg)


def _out_proj_kernel(x_ref, ma_ref, mbc_ref, wa_ref, wbc_ref, out_ref):
    out_ref[...] = x_ref[...] + _dot(ma_ref[...], wa_ref[...]) + _dot(mbc_ref[...], wbc_ref[...])


def _out_proj(x2, mixed_a, mixed_bc, w_out, tm=512):
    t = x2.shape[0]
    row = lambda i: (i, 0)
    const = lambda i: (0, 0)
    return pl.pallas_call(
        _out_proj_kernel,
        grid=(t // tm,),
        in_specs=[
            pl.BlockSpec((tm, D_MODEL), row),
            pl.BlockSpec((tm, A_WIDTH), row),
            pl.BlockSpec((tm, B_WIDTH + C_WIDTH), row),
            pl.BlockSpec((A_WIDTH, D_MODEL), const),
            pl.BlockSpec((B_WIDTH + C_WIDTH, D_MODEL), lambda i: (1, 0)),
        ],
        out_specs=pl.BlockSpec((tm, D_MODEL), row),
        out_shape=jax.ShapeDtypeStruct((t, D_MODEL), F32),
        compiler_params=pltpu.CompilerParams(dimension_semantics=("arbitrary",), vmem_limit_bytes=VMEM_LIMIT),
        name="out_proj",
    )(x2, mixed_a, mixed_bc, w_out, w_out)


def _layer(h, mem, norm_w, w_in_t, conv_w_a, a_log_fwd, a_log_bwd, dt_bias_fwd, dt_bias_bwd, o_norm_a,
           q_norm_b, k_norm_b, sink_b, mem_norm_w, w_mem_kv, q_norm_c, k_norm_c, w_out):
    batch, seq, _ = h.shape
    t = batch * seq
    x2 = h.reshape(t, D_MODEL)

    w_a, w_g, w_t = _w_prep(w_in_t)
    zeros = jnp.zeros((LANES - 2 * A_HEADS,), F32)
    alog_row = jnp.concatenate([a_log_fwd, a_log_bwd, zeros]).reshape(1, LANES)
    dtb_row = jnp.concatenate([dt_bias_fwd, dt_bias_bwd, zeros]).reshape(1, LANES)
    cos, sin = _rope_tables(seq)

    qkv, ga, ab, abt, qb, kb, vb, gb, qc, gcg = _in_proj(
        x2, norm_w.reshape(1, D_MODEL), w_a, w_g, w_t, conv_w_a, cos, sin, alog_row, dtb_row,
        (jnp.tile(q_norm_b, B_Q_HEADS) * B_HEAD_DIM ** -0.5).reshape(1, B_WIDTH),
        jnp.tile(k_norm_b, B_KV_HEADS).reshape(1, LANES),
        (jnp.tile(q_norm_c, C_HEADS) * C_HEAD_DIM ** -0.5).reshape(1, C_WIDTH), seq)

    mlen = mem.shape[1]
    km, vm = _mem_proj(mem.reshape(batch * mlen, D_MODEL), mem_norm_w.reshape(1, D_MODEL),
                       w_mem_kv.astype(BF16), jnp.tile(k_norm_c, C_HEADS).reshape(1, C_WIDTH))

    mixed_a = _deltanet(qkv, ab, abt, ga,
                        o_norm_a.reshape(1, A_HEAD_DIM), batch, seq)
    mixed_bc = _attention(sink_b, qb, kb, vb, gb, qc, km, vm, gcg, batch, seq)

    out = _out_proj(x2, mixed_a, mixed_bc, w_out.astype(BF16))
    return out.reshape(batch, seq, D_MODEL)


def kernel(x, mem, norm_w, w_in, conv_w_a, a_log_fwd, a_log_bwd, dt_bias_fwd, dt_bias_bwd, o_norm_a,
           q_norm_b, k_norm_b, sink_b, mem_norm_w, w_mem_kv, q_norm_c, k_norm_c, w_out):
    h = x
    w_in_t = jnp.swapaxes(w_in, 1, 2)
    conv_taps = jnp.swapaxes(conv_w_a, 0, 1)
    for l in range(norm_w.shape[0]):
        h = _layer(h, mem, norm_w[l], w_in_t[l], conv_taps[:, l:l + 1], a_log_fwd[l], a_log_bwd[l], dt_bias_fwd[l],
                   dt_bias_bwd[l], o_norm_a[l], q_norm_b[l], k_norm_b[l], sink_b[l], mem_norm_w[l],
                   w_mem_kv[l], q_norm_c[l], k_norm_c[l], w_out[l])
    return h
```

```python
import functools

import numpy as np
import jax
import jax.numpy as jnp
from jax import lax
from jax.experimental import pallas as pl
from jax.experimental.pallas import tpu as pltpu

F32 = jnp.float32
BF16 = jnp.bfloat16

D_MODEL = 2048
A_WIDTH = 1024
A_HEAD_DIM = 128
A_HEADS = 8
CONV_K = 5
CHUNK = 64
B_WIDTH = 512
B_HEAD_DIM = 64
B_Q_HEADS = 8
B_KV_HEADS = 2
WINDOW = 128
C_WIDTH = 512
C_HEADS = 4
C_HEAD_DIM = 128
ROPE_THETA = 10000.0
EPS = 1e-6

LANES = 128
GATE_COLS = 4 * A_HEADS
A_COLS = 4 * A_WIDTH
T_QB = 0
T_KB = T_QB + B_WIDTH
T_VB = T_KB + B_KV_HEADS * B_HEAD_DIM
T_ZB = T_VB + B_KV_HEADS * B_HEAD_DIM
T_QC = T_ZB + B_WIDTH
T_ZC = T_QC + C_WIDTH
T_COLS = T_ZC + C_WIDTH

VMEM_LIMIT = 56 * 1024 * 1024
CONV_SHIFT = 16
CONV_HALO = 24
CONV_COLS = 256
CONV_ROWS = 128

_NT = (((1,), (1,)), ((), ()))
_TN = (((0,), (0,)), ((), ()))


def _dot(a, b):
    return jnp.dot(a, b, preferred_element_type=F32)


def _dot_nt(a, b):
    return lax.dot_general(a, b, _NT, preferred_element_type=F32)


def _silu(z):
    return z * jax.nn.sigmoid(z)


def _softplus(z):
    return jnp.maximum(z, 0.0) + jnp.log1p(jnp.exp(-jnp.abs(z)))


def _rope_partner(x):
    lane = lax.broadcasted_iota(jnp.int32, x.shape, 1)
    first_half = (lane % B_HEAD_DIM) < (B_HEAD_DIM // 2)
    up = pltpu.roll(x, LANES - B_HEAD_DIM // 2, axis=1)
    down = pltpu.roll(x, B_HEAD_DIM // 2, axis=1)
    return jnp.where(first_half, up, down)


def _head_rms(x, ones_blockdiag, head_dim, w):
    ss = _dot((x * x).astype(BF16), ones_blockdiag)
    return x * lax.rsqrt(ss * (1.0 / head_dim) + EPS) * w


def _w_main_kernel(w_ref, o_ref):
    o_ref[...] = w_ref[...].T.astype(BF16)


def _w_rest_kernel(g_ref, w_ref, wg_ref, wt_ref):
    g = g_ref[...].T
    wg_ref[...] = jnp.concatenate([g, jnp.zeros((g.shape[0], LANES - GATE_COLS), F32)], axis=1).astype(BF16)
    wt_ref[...] = w_ref[...].T.astype(BF16)


def _w_prep(w_t, tr=256):
    cols = w_t.shape[1]
    w_a = pl.pallas_call(
        _w_main_kernel,
        grid=(A_COLS // tr,),
        in_specs=[pl.BlockSpec((tr, cols), lambda i: (i, 0))],
        out_specs=pl.BlockSpec((cols, tr), lambda i: (0, i)),
        out_shape=jax.ShapeDtypeStruct((cols, A_COLS), BF16),
        compiler_params=pltpu.CompilerParams(dimension_semantics=("arbitrary",), vmem_limit_bytes=VMEM_LIMIT),
        name="w_main",
    )(w_t)
    w_g, w_tail = pl.pallas_call(
        _w_rest_kernel,
        grid=(T_COLS // tr,),
        in_specs=[pl.BlockSpec((GATE_COLS, cols), lambda i: (A_COLS // GATE_COLS, 0)),
                  pl.BlockSpec((pl.Element(tr), pl.Element(cols)),
                               lambda i: (pl.multiple_of(A_COLS + GATE_COLS + i * tr, GATE_COLS), 0))],
        out_specs=[pl.BlockSpec((cols, LANES), lambda i: (0, 0)), pl.BlockSpec((cols, tr), lambda i: (0, i))],
        out_shape=[jax.ShapeDtypeStruct((cols, LANES), BF16), jax.ShapeDtypeStruct((cols, T_COLS), BF16)],
        compiler_params=pltpu.CompilerParams(dimension_semantics=("arbitrary",), vmem_limit_bytes=VMEM_LIMIT),
        name="w_rest",
    )(w_t, w_t)
    return w_a, w_g, w_tail


def _in_proj_kernel(tiles_per_seq, x_ref, nw_ref, wa_ref, wg_ref, wt_ref, cw_ref, cos_ref, sin_ref, alog_ref,
                    dtb_ref, qnb_ref, knb_ref, qnc_ref, e64_ref,
                    qkv_ref, ga_ref, ab_ref, abt_ref, qb_ref, kb_ref, vb_ref, gb_ref, qc_ref, gc_ref,
                    ext_s):
    step_id = pl.program_id(0)

    @pl.when(step_id == 0)
    def _():
        ext_s[:, 0:CONV_HALO, :] = jnp.zeros((ext_s.shape[0], CONV_HALO, A_HEAD_DIM), F32)

    seq_start = (step_id % tiles_per_seq) == 0
    zero = jnp.minimum(step_id, 0)
    x = x_ref[...]
    ms = jnp.mean(x * x, axis=-1, keepdims=True)
    hn = (x * lax.rsqrt(ms + EPS) * nw_ref[...]).astype(BF16)
    tm = x.shape[0]

    def proj(w_ref, c0, width):
        return _dot(hn, w_ref[:, c0:c0 + width])

    fix_rows = lax.broadcasted_iota(jnp.int32, (2 * 8, 1), 0) + (CONV_SHIFT - 8)
    first = CONV_HALO - CONV_SHIFT - CONV_K // 2
    lo = CONV_SHIFT - 8

    def conv_head(c):
        lanes = slice(c, c + A_HEAD_DIM)
        head = c // A_HEAD_DIM
        w = cw_ref[:, 0, lanes]
        taps = [w[j:j + 1, :] for j in range(CONV_K)]
        scale = A_HEAD_DIM ** -0.5 if c < A_WIDTH else 1.0
        for r in range(0, tm, CONV_ROWS):
            y = ext_s[head, pl.ds(zero + (first + r), CONV_ROWS), :] * taps[0]
            for j in range(1, CONV_K):
                y = y + ext_s[head, pl.ds(zero + (first + r + j), CONV_ROWS), :] * taps[j]
            if r == 0:
                cur0 = ext_s[head, CONV_HALO:CONV_HALO + 1, :]
                cur1 = ext_s[head, CONV_HALO + 1:CONV_HALO + 2, :]
                old0 = ext_s[head, CONV_HALO - 2:CONV_HALO - 1, :]
                old1 = ext_s[head, CONV_HALO - 1:CONV_HALO, :]
                cross = (jnp.where(fix_rows == CONV_SHIFT - 2, taps[4] * cur0, 0.0)
                         + jnp.where(fix_rows == CONV_SHIFT - 1, taps[3] * cur0 + taps[4] * cur1, 0.0)
                         + jnp.where(fix_rows == CONV_SHIFT, taps[0] * old0 + taps[1] * old1, 0.0)
                         + jnp.where(fix_rows == CONV_SHIFT + 1, taps[0] * old1, 0.0))
                fixed = y[lo:lo + 16] - jnp.where(seq_start, cross, 0.0)
                y = jnp.concatenate([y[:lo], fixed, y[lo + 16:]], axis=0)
            y = _silu(y)
            if c < 2 * A_WIDTH:
                y = y * (lax.rsqrt(jnp.sum(y * y, axis=-1, keepdims=True) + EPS) * scale)
            qkv_ref[r:r + CONV_ROWS, lanes] = y.astype(BF16)

    pending = list(range(0, 3 * A_WIDTH, A_HEAD_DIM))

    def conv_some(n):
        for _ in range(n):
            if pending:
                conv_head(pending.pop(0))

    for c in range(0, 3 * A_WIDTH, CONV_COLS):
        acc = proj(wa_ref, c, CONV_COLS)
        for hc in range(0, CONV_COLS, A_HEAD_DIM):
            ext_s[(c + hc) // A_HEAD_DIM, CONV_HALO:CONV_HALO + tm, :] = acc[:, hc:hc + A_HEAD_DIM]
        if c >= CONV_COLS:
            conv_some(1)

    step = 512
    for c in range(0, A_WIDTH, step):
        ga_ref[:, c:c + step] = _silu(proj(wa_ref, 3 * A_WIDTH + c, step)).astype(BF16)
        conv_some(2)

    acc = _dot(hn, wg_ref[...])
    lane = lax.broadcasted_iota(jnp.int32, (tm, LANES), 1)
    g = -jnp.exp(alog_ref[...]) * _softplus(acc + dtb_ref[...])
    val = jnp.where(lane < 2 * A_HEADS, g, jax.nn.sigmoid(acc))
    val = jnp.where(lane < GATE_COLS, val, 0.0)
    pos = lax.broadcasted_iota(jnp.int32, (tm, LANES), 0) % CHUNK
    pre = val
    suf = val
    s = 1
    while s < CHUNK:
        pre = pre + jnp.where(pos >= s, pltpu.roll(pre, s, axis=0), 0.0)
        suf = suf + jnp.where(pos < CHUNK - s, pltpu.roll(suf, tm - s, axis=0), 0.0)
        s *= 2
    res = jnp.where(lane < A_HEADS, pre, jnp.where(lane < 2 * A_HEADS, suf, val))
    ab_ref[...] = res
    abt_ref[0] = res.T
    conv_some(1)

    cos = cos_ref[...]
    sin = sin_ref[...]
    qb = _head_rms(proj(wt_ref, T_QB, B_WIDTH), e64_ref[...], B_HEAD_DIM, qnb_ref[...])
    for c in range(0, B_WIDTH, LANES):
        t = qb[:, c:c + LANES]
        qb_ref[:, c:c + LANES] = (t * cos + _rope_partner(t) * sin).astype(BF16)
    conv_some(2)
    kvb = proj(wt_ref, T_KB, 2 * LANES)
    kb = _head_rms(kvb[:, :LANES], e64_ref[:LANES, :LANES], B_HEAD_DIM, knb_ref[...])
    kb_ref[...] = (kb * cos + _rope_partner(kb) * sin).astype(BF16)
    vb_ref[...] = kvb[:, LANES:].astype(BF16)
    conv_some(1)
    gb_ref[...] = _silu(proj(wt_ref, T_ZB, B_WIDTH)).astype(BF16)
    conv_some(2)
    qc = proj(wt_ref, T_QC, C_WIDTH)
    for c in range(0, C_WIDTH, C_HEAD_DIM):
        t = qc[:, c:c + C_HEAD_DIM]
        t = t * lax.rsqrt(jnp.mean(t * t, axis=-1, keepdims=True) + EPS) * qnc_ref[:, c:c + C_HEAD_DIM]
        qc_ref[:, c:c + C_HEAD_DIM] = t.astype(BF16)
    conv_some(2)
    gc_ref[...] = _silu(proj(wt_ref, T_ZC, C_WIDTH)).astype(BF16)
    conv_some(len(pending))
    ext_s[:, 0:CONV_HALO, :] = ext_s[:, tm:tm + CONV_HALO, :]


def _block_diag_ones(width, block):
    idx = np.arange(width) // block
    return jnp.asarray(idx[:, None] == idx[None, :], dtype=BF16)


def _rope_tables(seq):
    d = B_HEAD_DIM
    inv = ROPE_THETA ** (-jnp.arange(0, d, 2, dtype=F32) / d)
    ang = jnp.arange(seq, dtype=F32)[:, None] * inv[None, :]
    cos = jnp.cos(ang)
    sin = jnp.sin(ang)
    cos_h = jnp.concatenate([cos, cos], axis=-1)
    sin_h = jnp.concatenate([-sin, sin], axis=-1)
    reps = LANES // d
    return jnp.tile(cos_h, (1, reps)), jnp.tile(sin_h, (1, reps))


def _in_proj(x2, norm_w, w_a, w_g, w_t, conv_w, cos, sin, alog_row, dtb_row, qnb, knb, qnc, seq, tm=256):
    t = x2.shape[0]
    tiles_per_seq = seq // tm
    nt = t // tm
    row = lambda i: (jnp.minimum(i, nt - 1), 0)
    const = lambda i: (0, 0)
    pos = lambda i: (jnp.minimum(i, nt - 1) % tiles_per_seq, 0)
    in_specs = [
        pl.BlockSpec((tm, D_MODEL), row),
        pl.BlockSpec((1, D_MODEL), const),
        pl.BlockSpec((D_MODEL, A_COLS), const, pipeline_mode=pl.Buffered(1)),
        pl.BlockSpec((D_MODEL, LANES), const, pipeline_mode=pl.Buffered(1)),
        pl.BlockSpec((D_MODEL, T_COLS), const, pipeline_mode=pl.Buffered(1)),
        pl.BlockSpec((CONV_K, 1, 3 * A_WIDTH), lambda i: (0, 0, 0)),
        pl.BlockSpec((tm, LANES), pos),
        pl.BlockSpec((tm, LANES), pos),
        pl.BlockSpec((1, LANES), const),
        pl.BlockSpec((1, LANES), const),
        pl.BlockSpec((1, B_WIDTH), const),
        pl.BlockSpec((1, LANES), const),
        pl.BlockSpec((1, C_WIDTH), const),
        pl.BlockSpec((B_WIDTH, B_WIDTH), const),
    ]
    out_shape = [
        jax.ShapeDtypeStruct((t + tm, 3 * A_WIDTH), BF16),
        jax.ShapeDtypeStruct((t, A_WIDTH), BF16),
        jax.ShapeDtypeStruct((t, LANES), F32),
        jax.ShapeDtypeStruct((t // tm, LANES, tm), F32),
        jax.ShapeDtypeStruct((t, B_WIDTH), BF16),
        jax.ShapeDtypeStruct((t, LANES), BF16),
        jax.ShapeDtypeStruct((t, LANES), BF16),
        jax.ShapeDtypeStruct((t, B_WIDTH), BF16),
        jax.ShapeDtypeStruct((t, C_WIDTH), BF16),
        jax.ShapeDtypeStruct((t, C_WIDTH), BF16),
    ]
    out_specs = [
        pl.BlockSpec((tm, 3 * A_WIDTH), lambda i: (i, 0)),
        pl.BlockSpec((tm, A_WIDTH), row),
        pl.BlockSpec((tm, LANES), row),
        pl.BlockSpec((1, LANES, tm), lambda i: (jnp.minimum(i, nt - 1), 0, 0)),
        pl.BlockSpec((tm, B_WIDTH), row),
        pl.BlockSpec((tm, LANES), row),
        pl.BlockSpec((tm, LANES), row),
        pl.BlockSpec((tm, B_WIDTH), row),
        pl.BlockSpec((tm, C_WIDTH), row),
        pl.BlockSpec((tm, C_WIDTH), row),
    ]
    return pl.pallas_call(
        functools.partial(_in_proj_kernel, tiles_per_seq),
        grid=(nt + 1,),
        in_specs=in_specs,
        out_specs=out_specs,
        out_shape=out_shape,
        scratch_shapes=[pltpu.VMEM((3 * A_HEADS, CONV_HALO + tm, A_HEAD_DIM), F32)],
        compiler_params=pltpu.CompilerParams(dimension_semantics=("arbitrary",), vmem_limit_bytes=VMEM_LIMIT),
        name="in_proj",
    )(x2, norm_w, w_a, w_g, w_t, conv_w, cos, sin, alog_row, dtb_row, qnb, knb, qnc,
      _block_diag_ones(B_WIDTH, B_HEAD_DIM))


def _mem_proj_kernel(m_ref, nw_ref, w_ref, knc_ref, e128_ref, km_ref, vm_ref):
    x = m_ref[...]
    ms = jnp.mean(x * x, axis=-1, keepdims=True)
    mn = (x * lax.rsqrt(ms + EPS) * nw_ref[...]).astype(BF16)
    km = _dot(mn, w_ref[:, :C_WIDTH])
    km_ref[...] = _head_rms(km, e128_ref[...], C_HEAD_DIM, knc_ref[...]).astype(BF16)
    vm_ref[...] = _dot(mn, w_ref[:, C_WIDTH:]).astype(BF16)


def _mem_proj(mem2, mem_norm_w, w_kv, knc, tm=256):
    t = mem2.shape[0]
    row = lambda i: (i, 0)
    const = lambda i: (0, 0)
    return pl.pallas_call(
        _mem_proj_kernel,
        grid=(t // tm,),
        in_specs=[
            pl.BlockSpec((tm, D_MODEL), row),
            pl.BlockSpec((1, D_MODEL), const),
            pl.BlockSpec((D_MODEL, 2 * C_WIDTH), const),
            pl.BlockSpec((1, C_WIDTH), const),
            pl.BlockSpec((C_WIDTH, C_WIDTH), const),
        ],
        out_specs=[pl.BlockSpec((tm, C_WIDTH), row), pl.BlockSpec((tm, C_WIDTH), row)],
        out_shape=[jax.ShapeDtypeStruct((t, C_WIDTH), BF16), jax.ShapeDtypeStruct((t, C_WIDTH), BF16)],
        compiler_params=pltpu.CompilerParams(dimension_semantics=("arbitrary",), vmem_limit_bytes=VMEM_LIMIT),
        name="mem_proj",
    )(mem2, mem_norm_w, w_kv, knc, _block_diag_ones(C_WIDTH, C_HEAD_DIM))


HEADS_PER_STEP = 2
PACK = 4
PACK_ROWS = PACK * CHUNK
SQUARINGS = 5
PREP_PACKS = 2
PREP_LEVELS = SQUARINGS + 3


def _deltanet_kernel(q_ref, k_ref, v_ref, qt_ref, kt_ref, vt_ref, ab_ref, gfr_ref, gbr_ref, ga_ref, on_ref,
                     out_ref,
                     qs, ks, vs, kq_s, b_s, cd_s, o_s):
    seq = q_ref.shape[0]
    nc = seq // CHUNK
    hb = HEADS_PER_STEP
    h0 = pl.program_id(1) * hb
    slab = 256

    def slab_rows(i):
        return pl.ds(pl.multiple_of(i * slab, slab), slab)

    def stage(src_ref, tail_ref, dst_ref, hs):
        cols = slice(hs * LANES, (hs + 1) * LANES)

        def copy(i, carry):
            src_rows = pl.ds(pl.multiple_of(CONV_SHIFT + i * slab, CONV_SHIFT), slab)
            dst_ref[hs, slab_rows(i), :] = src_ref[src_rows, cols].astype(F32)
            return carry

        lax.fori_loop(0, seq // slab - 1, copy, 0)
        last = seq - slab
        dst_ref[hs, last:seq - CONV_SHIFT, :] = src_ref[last + CONV_SHIFT:seq, cols].astype(F32)
        dst_ref[hs, seq - CONV_SHIFT:seq, :] = tail_ref[:, cols].astype(F32)

    for hs in range(hb):
        stage(q_ref, qt_ref, qs, hs)
        stage(k_ref, kt_ref, ks, hs)
        stage(v_ref, vt_ref, vs, hs)

    ii = lax.broadcasted_iota(jnp.int32, (CHUNK, PACK_ROWS), 0)
    jl = lax.broadcasted_iota(jnp.int32, (CHUNK, PACK_ROWS), 1)
    lb = jl // CHUNK
    jj = jl % CHUNK
    block_diag = (lax.broadcasted_iota(jnp.int32, (PACK_ROWS, PACK_ROWS), 0) // CHUNK
                  == lax.broadcasted_iota(jnp.int32, (PACK_ROWS, PACK_ROWS), 1) // CHUNK)

    def pack_diag(full):
        out = full[0:CHUNK]
        for c in range(1, PACK):
            out = jnp.where(lb == c, full[c * CHUNK:(c + 1) * CHUNK], out)
        return out

    def pack_cols(col):
        out = None
        for c in range(PACK):
            blk = col[c * CHUNK:(c + 1) * CHUNK]
            blk = jnp.concatenate([blk, blk], axis=1)
            out = blk if out is None else jnp.where(lb == c, blk, out)
        return out

    def to_block_diag(packed):
        return jnp.where(block_diag, jnp.concatenate([packed] * PACK, axis=0), 0.0).astype(BF16)

    lane4 = lax.broadcasted_iota(jnp.int32, (PACK_ROWS, LANES), 1)
    npk = seq // PACK_ROWS
    chains = [(hs, d) for hs in range(hb) for d in range(2)]

    def prep_gen(it):
        members = []
        for pk in range(PREP_PACKS):
            for d in range(2):
                fwd = d == 0
                p = it * PREP_PACKS + pk
                if not fwd:
                    p = npk - 1 - p
                rows = pl.ds(pl.multiple_of(p * PACK_ROWS, PACK_ROWS), PACK_ROWS)
                ab4 = ab_ref[rows, :]

                def gate_col(col, ab4=ab4):
                    c = jnp.sum(jnp.where(lane4 == col, ab4, 0.0), axis=-1, keepdims=True)
                    return jnp.broadcast_to(c, (PACK_ROWS, LANES))

                incl = (ii >= jj) if fwd else (ii <= jj)
                strict = (ii > jj) if fwd else (ii < jj)
                for hs in range(hb):
                    k4 = ks[hs, rows, :]
                    q4 = qs[hs, rows, :]
                    v4 = vs[hs, rows, :]
                    k4b = k4.astype(BF16)
                    kk_p = pack_diag(_dot_nt(k4b, k4b))
                    qk_p = pack_diag(_dot_nt(q4.astype(BF16), k4b))
                    gc4 = gate_col(d * A_HEADS + h0 + hs)
                    bt4 = gate_col((2 + d) * A_HEADS + h0 + hs)
                    gcr = (gfr_ref if fwd else gbr_ref)[p, pl.ds(h0 + hs, 1), :]
                    decay = jnp.where(incl, jnp.exp(jnp.where(incl, pack_cols(gc4) - gcr, 0.0)), 0.0)
                    n_cat = jnp.where(strict, -(kk_p * decay * pack_cols(bt4)), 0.0)
                    qkm = jnp.where(incl, qk_p * decay, 0.0)
                    members.append((hs, d, k4, q4, v4, gc4, bt4, n_cat, qkm, p, rows))
        yield

        eye = jnp.where(ii == jj, 1.0, 0.0)
        tinv = [eye + m[7] for m in members]
        pw = [_dot(m[7].astype(BF16), to_block_diag(m[7])) for m in members]
        yield
        for level in range(1, SQUARINGS + 1):
            last = level == SQUARINGS
            for i in range(len(members)):
                rhs = to_block_diag(pw[i])
                if last:
                    tinv[i] = tinv[i] + _dot(tinv[i].astype(BF16), rhs)
                else:
                    both = _dot(jnp.concatenate([pw[i], tinv[i]], axis=0).astype(BF16), rhs)
                    pw[i] = both[:CHUNK]
                    tinv[i] = tinv[i] + both[CHUNK:]
            yield

        uws = []
        for (hs, d, k4, q4, v4, gc4, bt4, _, qkm, p, rows), t_p in zip(members, tinv):
            eg = jnp.exp(gc4)
            rhs = jnp.concatenate([v4 * bt4, k4 * bt4 * eg], axis=1).astype(BF16)
            uws.append((_dot(to_block_diag(t_p), rhs), eg))
        yield

        folded = []
        for (hs, d, k4, q4, v4, gc4, bt4, _, qkm, p, rows), (uw, eg) in zip(members, uws):
            fwd = d == 0
            uwb = uw.astype(BF16)
            per_chunk = []
            for c in range(PACK):
                rs = slice(c * CHUNK, (c + 1) * CHUNK)
                last_row = (c + 1) * CHUNK - 1 if fwd else c * CHUNK
                gl = gc4[last_row:last_row + 1, :]
                kd = k4[rs] * jnp.exp(gl - gc4[rs])
                lhs = jnp.concatenate([qkm[:, c * CHUNK:(c + 1) * CHUNK], kd.T], axis=0).astype(BF16)
                per_chunk.append((_dot(lhs, uwb[rs]), gl))
            folded.append(per_chunk)
        yield

        for (hs, d, k4, q4, v4, gc4, bt4, _, qkm, p, rows), (uw, eg), per_chunk in zip(members, uws, folded):
            qd = q4 * eg
            for c, (res, gl) in enumerate(per_chunk):
                n = p * PACK + c
                rs = slice(c * CHUNK, (c + 1) * CHUNK)
                o_s[hs, d, pl.ds(pl.multiple_of(n * CHUNK, CHUNK), CHUNK), :] = res[:CHUNK, :LANES]
                kq_s[hs, d, n, 0:LANES, :] = res[CHUNK:, LANES:].astype(BF16)
                kq_s[hs, d, n, LANES:LANES + CHUNK, :] = (qd[rs] - res[:CHUNK, LANES:]).astype(BF16)
                b_s[hs, d, n] = res[CHUNK:, :LANES]
                cd_s[hs, d, n] = jnp.broadcast_to(jnp.exp(gl), (8, LANES))

    steps_per_stage = PREP_PACKS * PACK

    def seq_gen(it, states, result):
        states = list(states)
        for k in range(steps_per_stage):
            i = it * steps_per_stage + k
            ns = (i, nc - 1 - i)
            rows = [pl.ds(pl.multiple_of(n * CHUNK, CHUNK), CHUNK) for n in ns]
            outs = [_dot(kq_s[hs, d, ns[d]], st.astype(BF16)) for (hs, d), st in zip(chains, states)]
            for c, ((hs, d), out) in enumerate(zip(chains, outs)):
                o_s[hs, d, rows[d], :] = o_s[hs, d, rows[d], :] + out[LANES:]
                states[c] = states[c] * cd_s[hs, d, ns[d]][0:1, :] - out[:LANES] + b_s[hs, d, ns[d]]
            yield
        result.append(tuple(states))

    def run(gen):
        for _ in gen:
            pass

    def fused_body(j, states):
        result = []
        pg = prep_gen(j + 1)
        sg = seq_gen(j, states, result)
        next(pg)
        for _ in range(steps_per_stage):
            for _ in range(PREP_LEVELS // steps_per_stage):
                next(pg)
            next(sg)
        run(sg)
        run(pg)
        return result[0]

    n_stages = npk // PREP_PACKS
    s0 = jnp.zeros((A_HEAD_DIM, A_HEAD_DIM), F32)
    run(prep_gen(0))
    states = lax.fori_loop(0, n_stages - 1, fused_body, tuple(s0 for _ in chains))
    run(seq_gen(n_stages - 1, states, []))

    def finish(i, carry):
        for hs in range(hb):
            cols = slice(hs * LANES, (hs + 1) * LANES)
            o = o_s[hs, 0, slab_rows(i), :] + o_s[hs, 1, slab_rows(i), :]
            y = o * lax.rsqrt(jnp.mean(o * o, axis=-1, keepdims=True) + EPS) * on_ref[...]
            out_ref[slab_rows(i), cols] = (y * ga_ref[slab_rows(i), cols].astype(F32)).astype(BF16)
        return carry

    lax.fori_loop(0, seq // slab, finish, 0, unroll=2)


def _deltanet(qkv, ab, abt3, ga, o_norm, batch, seq):
    nc = seq // CHUNK
    npk = seq // PACK_ROWS
    t = batch * seq
    hb = HEADS_PER_STEP
    hw = hb * LANES
    groups = A_HEADS // hb
    tail_blocks = seq // CONV_SHIFT
    scratch = [
        pltpu.VMEM((hb, seq, LANES), F32),
        pltpu.VMEM((hb, seq, LANES), F32),
        pltpu.VMEM((hb, seq, LANES), F32),
        pltpu.VMEM((hb, 2, nc, LANES + CHUNK, LANES), BF16),
        pltpu.VMEM((hb, 2, nc, LANES, LANES), F32),
        pltpu.VMEM((hb, 2, nc, 8, LANES), F32),
        pltpu.VMEM((hb, 2, seq, LANES), F32),
    ]
    return pl.pallas_call(
        _deltanet_kernel,
        grid=(batch, groups),
        in_specs=[
            pl.BlockSpec((seq, hw), lambda b, g: (b, g)),
            pl.BlockSpec((seq, hw), lambda b, g: (b, groups + g)),
            pl.BlockSpec((seq, hw), lambda b, g: (b, 2 * groups + g)),
            pl.BlockSpec((CONV_SHIFT, hw), lambda b, g: ((b + 1) * tail_blocks, g)),
            pl.BlockSpec((CONV_SHIFT, hw), lambda b, g: ((b + 1) * tail_blocks, groups + g)),
            pl.BlockSpec((CONV_SHIFT, hw), lambda b, g: ((b + 1) * tail_blocks, 2 * groups + g)),
            pl.BlockSpec((seq, LANES), lambda b, g: (b, 0)),
            pl.BlockSpec((npk, A_HEADS, PACK_ROWS), lambda b, g: (b, 0, 0)),
            pl.BlockSpec((npk, A_HEADS, PACK_ROWS), lambda b, g: (b, 1, 0)),
            pl.BlockSpec((seq, hw), lambda b, g: (b, g)),
            pl.BlockSpec((1, LANES), lambda b, g: (0, 0)),
        ],
        out_specs=pl.BlockSpec((seq, hw), lambda b, g: (b, g)),
        out_shape=jax.ShapeDtypeStruct((t, A_WIDTH), BF16),
        scratch_shapes=scratch,
        compiler_params=pltpu.CompilerParams(dimension_semantics=("arbitrary", "arbitrary"),
                                             vmem_limit_bytes=VMEM_LIMIT),
        name="deltanet",
    )(qkv, qkv, qkv, qkv, qkv, qkv, ab, abt3, abt3, ga, o_norm)


Q_BLOCKS = 4


def _attn_kernel(sink_ref, qb_ref, kb_ref, vb_ref, gb_ref, qc_ref, km_ref, vm_ref, gc_ref, out_ref):
    nb = pl.num_programs(1) * Q_BLOCKS
    w = WINDOW
    lane = lax.broadcasted_iota(jnp.int32, (w, LANES), 1)
    lower = lane < B_HEAD_DIM
    ri = lax.broadcasted_iota(jnp.int32, (2 * w, w), 0) % w
    ci = lax.broadcasted_iota(jnp.int32, (2 * w, w), 1)
    top = lax.broadcasted_iota(jnp.int32, (2 * w, 1), 0) < w

    first_blk = pl.program_id(1) * Q_BLOCKS - 1
    kv = []
    for i in range(Q_BLOCKS + 2):
        j = first_blk + i
        start = pl.multiple_of(jnp.clip(j, 0, nb - 1) * w, w)
        kblk = kb_ref[pl.ds(start, w), :].astype(F32)
        vblk = vb_ref[pl.ds(start, w), :].astype(F32)
        ksw = pltpu.roll(kblk, B_HEAD_DIM, axis=1)
        vsw = pltpu.roll(vblk, B_HEAD_DIM, axis=1)
        variants = {}
        for hk in range(B_KV_HEADS):
            for half in range(2):
                keep = lower if half == 0 else jnp.logical_not(lower)
                in_place = hk == half
                variants[hk, half] = (jnp.where(keep, kblk if in_place else ksw, 0.0).astype(BF16),
                                      jnp.where(keep, vblk if in_place else vsw, 0.0).astype(BF16))
        kv.append((variants, jnp.logical_and(j >= 0, j < nb)))

    for qi in range(Q_BLOCKS):
        rows = slice(qi * w, (qi + 1) * w)
        masks = (jnp.logical_and(ci >= ri, kv[qi][1]), None, jnp.logical_and(ci <= ri, kv[qi + 2][1]))
        for hk in range(B_KV_HEADS):
            c0 = hk * 2 * LANES
            q2 = jnp.concatenate([qb_ref[rows, c0:c0 + LANES], qb_ref[rows, c0 + LANES:c0 + 2 * LANES]], axis=0)
            acc = jnp.zeros((2 * w, LANES), F32)
            for half in range(2):
                sink_col = jnp.where(top, sink_ref[4 * hk + half], sink_ref[4 * hk + 2 + half])
                ss = []
                for rel in range(3):
                    s = _dot_nt(q2, kv[qi + rel][0][hk, half][0])
                    if masks[rel] is not None:
                        s = jnp.where(masks[rel], s, -jnp.inf)
                    ss.append(s)
                m = jnp.maximum(sink_col, jnp.max(jnp.maximum(jnp.maximum(ss[0], ss[1]), ss[2]),
                                                  axis=-1, keepdims=True))
                ps = [jnp.exp(s - m) for s in ss]
                den = jnp.exp(sink_col - m) + jnp.sum(ps[0] + ps[1] + ps[2], axis=-1, keepdims=True)
                pv = jnp.zeros((2 * w, LANES), F32)
                for rel in range(3):
                    pv = pv + _dot(ps[rel].astype(BF16), kv[qi + rel][0][hk, half][1])
                acc = acc + pv / den
            out_ref[rows, c0:c0 + LANES] = (acc[:w] * gb_ref[rows, c0:c0 + LANES].astype(F32)).astype(BF16)
            out_ref[rows, c0 + LANES:c0 + 2 * LANES] = (
                acc[w:] * gb_ref[rows, c0 + LANES:c0 + 2 * LANES].astype(F32)).astype(BF16)

    for hc in range(C_HEADS):
        c0 = hc * C_HEAD_DIM
        s = _dot_nt(qc_ref[:, c0:c0 + C_HEAD_DIM], km_ref[:, c0:c0 + C_HEAD_DIM])
        m = jnp.max(s, axis=-1, keepdims=True)
        p = jnp.exp(s - m)
        den = jnp.sum(p, axis=-1, keepdims=True)
        o = _dot(p.astype(BF16), vm_ref[:, c0:c0 + C_HEAD_DIM]) / den
        out_ref[:, B_WIDTH + c0:B_WIDTH + c0 + C_HEAD_DIM] = (
            o * gc_ref[:, c0:c0 + C_HEAD_DIM].astype(F32)).astype(BF16)


def _attention(sink, qb, kb, vb, gb, qc, km, vm, gcg, batch, seq):
    nb = seq // (WINDOW * Q_BLOCKS)
    qw = WINDOW * Q_BLOCKS
    t = batch * seq
    mlen = km.shape[0] // batch
    qrow = lambda b, n: (b * nb + n, 0)
    per_b = lambda b, n: (b, 0)
    return pl.pallas_call(
        _attn_kernel,
        grid=(batch, nb),
        in_specs=[
            pl.BlockSpec(memory_space=pltpu.SMEM),
            pl.BlockSpec((qw, B_WIDTH), qrow),
            pl.BlockSpec((seq, LANES), per_b),
            pl.BlockSpec((seq, LANES), per_b),
            pl.BlockSpec((qw, B_WIDTH), qrow),
            pl.BlockSpec((qw, C_WIDTH), qrow),
            pl.BlockSpec((mlen, C_WIDTH), per_b),
            pl.BlockSpec((mlen, C_WIDTH), per_b),
            pl.BlockSpec((qw, C_WIDTH), qrow),
        ],
        out_specs=pl.BlockSpec((qw, B_WIDTH + C_WIDTH), qrow),
        out_shape=jax.ShapeDtypeStruct((t, B_WIDTH + C_WIDTH), BF16),
        compiler_params=pltpu.CompilerParams(dimension_semantics=("arbitrary", "arbitrary"),
                                             vmem_limit_bytes=VMEM_LIMIT),
        name="attention",
    )(sink, qb, kb, vb, gb, qc, km, vm, gcg)


def _out_proj_kernel(x_ref, ma_ref, mbc_ref, wa_ref, wbc_ref, out_ref):
    out_ref[...] = x_ref[...] + _dot(ma_ref[...], wa_ref[...]) + _dot(mbc_ref[...], wbc_ref[...])


def _out_proj(x2, mixed_a, mixed_bc, w_out, tm=512):
    t = x2.shape[0]
    row = lambda i: (i, 0)
    const = lambda i: (0, 0)
    return pl.pallas_call(
        _out_proj_kernel,
        grid=(t // tm,),
        in_specs=[
            pl.BlockSpec((tm, D_MODEL), row),
            pl.BlockSpec((tm, A_WIDTH), row),
            pl.BlockSpec((tm, B_WIDTH + C_WIDTH), row),
            pl.BlockSpec((A_WIDTH, D_MODEL), const),
            pl.BlockSpec((B_WIDTH + C_WIDTH, D_MODEL), lambda i: (1, 0)),
        ],
        out_specs=pl.BlockSpec((tm, D_MODEL), row),
        out_shape=jax.ShapeDtypeStruct((t, D_MODEL), F32),
        compiler_params=pltpu.CompilerParams(dimension_semantics=("arbitrary",), vmem_limit_bytes=VMEM_LIMIT),
        name="out_proj",
    )(x2, mixed_a, mixed_bc, w_out, w_out)


def _layer(h, mem, norm_w, w_in_t, conv_w_a, a_log_fwd, a_log_bwd, dt_bias_fwd, dt_bias_bwd, o_norm_a,
           q_norm_b, k_norm_b, sink_b, mem_norm_w, w_mem_kv, q_norm_c, k_norm_c, w_out):
    batch, seq, _ = h.shape
    t = batch * seq
    x2 = h.reshape(t, D_MODEL)

    w_a, w_g, w_t = _w_prep(w_in_t)
    zeros = jnp.zeros((LANES - 2 * A_HEADS,), F32)
    alog_row = jnp.concatenate([a_log_fwd, a_log_bwd, zeros]).reshape(1, LANES)
    dtb_row = jnp.concatenate([dt_bias_fwd, dt_bias_bwd, zeros]).reshape(1, LANES)
    cos, sin = _rope_tables(seq)

    qkv, ga, ab, abt, qb, kb, vb, gb, qc, gcg = _in_proj(
        x2, norm_w.reshape(1, D_MODEL), w_a, w_g, w_t, conv_w_a, cos, sin, alog_row, dtb_row,
        (jnp.tile(q_norm_b, B_Q_HEADS) * B_HEAD_DIM ** -0.5).reshape(1, B_WIDTH),
        jnp.tile(k_norm_b, B_KV_HEADS).reshape(1, LANES),
        (jnp.tile(q_norm_c, C_HEADS) * C_HEAD_DIM ** -0.5).reshape(1, C_WIDTH), seq)

    mlen = mem.shape[1]
    km, vm = _mem_proj(mem.reshape(batch * mlen, D_MODEL), mem_norm_w.reshape(1, D_MODEL),
                       w_mem_kv.astype(BF16), jnp.tile(k_norm_c, C_HEADS).reshape(1, C_WIDTH))

    mixed_a = _deltanet(qkv, ab, abt, ga,
                        o_norm_a.reshape(1, A_HEAD_DIM), batch, seq)
    mixed_bc = _attention(sink_b, qb, kb, vb, gb, qc, km, vm, gcg, batch, seq)

    out = _out_proj(x2, mixed_a, mixed_bc, w_out.astype(BF16))
    return out.reshape(batch, seq, D_MODEL)


def kernel(x, mem, norm_w, w_in, conv_w_a, a_log_fwd, a_log_bwd, dt_bias_fwd, dt_bias_bwd, o_norm_a,
           q_norm_b, k_norm_b, sink_b, mem_norm_w, w_mem_kv, q_norm_c, k_norm_c, w_out):
    h = x
    w_in_t = jnp.swapaxes(w_in, 1, 2)
    conv_taps = jnp.swapaxes(conv_w_a, 0, 1)
    for l in range(norm_w.shape[0]):
        h = _layer(h, mem, norm_w[l], w_in_t[l], conv_taps[:, l:l + 1], a_log_fwd[l], a_log_bwd[l], dt_bias_fwd[l],
                   dt_bias_bwd[l], o_norm_a[l], q_norm_b[l], k_norm_b[l], sink_b[l], mem_norm_w[l],
                   w_mem_kv[l], q_norm_c[l], k_norm_c[l], w_out[l])
    return h
```

```python
import functools

import numpy as np
import jax
import jax.numpy as jnp
from jax import lax
from jax.experimental import pallas as pl
from jax.experimental.pallas import tpu as pltpu

F32 = jnp.float32
BF16 = jnp.bfloat16

D_MODEL = 2048
A_WIDTH = 1024
A_HEAD_DIM = 128
A_HEADS = 8
CONV_K = 5
CHUNK = 64
B_WIDTH = 512
B_HEAD_DIM = 64
B_Q_HEADS = 8
B_KV_HEADS = 2
WINDOW = 128
C_WIDTH = 512
C_HEADS = 4
C_HEAD_DIM = 128
ROPE_THETA = 10000.0
EPS = 1e-6

LANES = 128
GATE_COLS = 4 * A_HEADS
A_COLS = 4 * A_WIDTH
T_QB = 0
T_KB = T_QB + B_WIDTH
T_VB = T_KB + B_KV_HEADS * B_HEAD_DIM
T_ZB = T_VB + B_KV_HEADS * B_HEAD_DIM
T_QC = T_ZB + B_WIDTH
T_ZC = T_QC + C_WIDTH
T_COLS = T_ZC + C_WIDTH

VMEM_LIMIT = 56 * 1024 * 1024
CONV_SHIFT = 16
CONV_HALO = 24
CONV_COLS = 256
CONV_ROWS = 128

_NT = (((1,), (1,)), ((), ()))
_TN = (((0,), (0,)), ((), ()))


def _dot(a, b):
    return jnp.dot(a, b, preferred_element_type=F32)


def _dot_nt(a, b):
    return lax.dot_general(a, b, _NT, preferred_element_type=F32)


def _silu(z):
    return z * jax.nn.sigmoid(z)


def _softplus(z):
    return jnp.maximum(z, 0.0) + jnp.log1p(jnp.exp(-jnp.abs(z)))


def _rope_partner(x):
    lane = lax.broadcasted_iota(jnp.int32, x.shape, 1)
    first_half = (lane % B_HEAD_DIM) < (B_HEAD_DIM // 2)
    up = pltpu.roll(x, LANES - B_HEAD_DIM // 2, axis=1)
    down = pltpu.roll(x, B_HEAD_DIM // 2, axis=1)
    return jnp.where(first_half, up, down)


def _head_rms(x, ones_blockdiag, head_dim, w):
    ss = _dot((x * x).astype(BF16), ones_blockdiag)
    return x * lax.rsqrt(ss * (1.0 / head_dim) + EPS) * w


def _w_main_kernel(w_ref, o_ref):
    o_ref[...] = w_ref[...].T.astype(BF16)


def _w_rest_kernel(g_ref, w_ref, wg_ref, wt_ref):
    g = g_ref[...].T
    wg_ref[...] = jnp.concatenate([g, jnp.zeros((g.shape[0], LANES - GATE_COLS), F32)], axis=1).astype(BF16)
    wt_ref[...] = w_ref[...].T.astype(BF16)


def _w_prep(w_t, tr=256):
    cols = w_t.shape[1]
    w_a = pl.pallas_call(
        _w_main_kernel,
        grid=(A_COLS // tr,),
        in_specs=[pl.BlockSpec((tr, cols), lambda i: (i, 0))],
        out_specs=pl.BlockSpec((cols, tr), lambda i: (0, i)),
        out_shape=jax.ShapeDtypeStruct((cols, A_COLS), BF16),
        compiler_params=pltpu.CompilerParams(dimension_semantics=("arbitrary",), vmem_limit_bytes=VMEM_LIMIT),
        name="w_main",
    )(w_t)
    w_g, w_tail = pl.pallas_call(
        _w_rest_kernel,
        grid=(T_COLS // tr,),
        in_specs=[pl.BlockSpec((GATE_COLS, cols), lambda i: (A_COLS // GATE_COLS, 0)),
                  pl.BlockSpec((pl.Element(tr), pl.Element(cols)),
                               lambda i: (pl.multiple_of(A_COLS + GATE_COLS + i * tr, GATE_COLS), 0))],
        out_specs=[pl.BlockSpec((cols, LANES), lambda i: (0, 0)), pl.BlockSpec((cols, tr), lambda i: (0, i))],
        out_shape=[jax.ShapeDtypeStruct((cols, LANES), BF16), jax.ShapeDtypeStruct((cols, T_COLS), BF16)],
        compiler_params=pltpu.CompilerParams(dimension_semantics=("arbitrary",), vmem_limit_bytes=VMEM_LIMIT),
        name="w_rest",
    )(w_t, w_t)
    return w_a, w_g, w_tail


def _in_proj_kernel(tiles_per_seq, x_ref, nw_ref, wa_ref, wg_ref, wt_ref, cw_ref, cos_ref, sin_ref, alog_ref,
                    dtb_ref, qnb_ref, knb_ref, qnc_ref, e64_ref,
                    qkv_ref, ga_ref, ab_ref, abt_ref, qb_ref, kb_ref, vb_ref, gb_ref, qc_ref, gc_ref,
                    ext_s):
    step_id = pl.program_id(0)

    @pl.when(step_id == 0)
    def _():
        ext_s[:, 0:CONV_HALO, :] = jnp.zeros((ext_s.shape[0], CONV_HALO, A_HEAD_DIM), F32)

    seq_start = (step_id % tiles_per_seq) == 0
    zero = jnp.minimum(step_id, 0)
    x = x_ref[...]
    ms = jnp.mean(x * x, axis=-1, keepdims=True)
    hn = (x * lax.rsqrt(ms + EPS) * nw_ref[...]).astype(BF16)
    tm = x.shape[0]

    def proj(w_ref, c0, width):
        return _dot(hn, w_ref[:, c0:c0 + width])

    fix_rows = lax.broadcasted_iota(jnp.int32, (2 * 8, 1), 0) + (CONV_SHIFT - 8)
    first = CONV_HALO - CONV_SHIFT - CONV_K // 2
    lo = CONV_SHIFT - 8

    def conv_head(c):
        lanes = slice(c, c + A_HEAD_DIM)
        head = c // A_HEAD_DIM
        w = cw_ref[:, 0, lanes]
        taps = [w[j:j + 1, :] for j in range(CONV_K)]
        scale = A_HEAD_DIM ** -0.5 if c < A_WIDTH else 1.0
        for r in range(0, tm, CONV_ROWS):
            y = ext_s[head, pl.ds(zero + (first + r), CONV_ROWS), :] * taps[0]
            for j in range(1, CONV_K):
                y = y + ext_s[head, pl.ds(zero + (first + r + j), CONV_ROWS), :] * taps[j]
            if r == 0:
                cur0 = ext_s[head, CONV_HALO:CONV_HALO + 1, :]
                cur1 = ext_s[head, CONV_HALO + 1:CONV_HALO + 2, :]
                old0 = ext_s[head, CONV_HALO - 2:CONV_HALO - 1, :]
                old1 = ext_s[head, CONV_HALO - 1:CONV_HALO, :]
                cross = (jnp.where(fix_rows == CONV_SHIFT - 2, taps[4] * cur0, 0.0)
                         + jnp.where(fix_rows == CONV_SHIFT - 1, taps[3] * cur0 + taps[4] * cur1, 0.0)
                         + jnp.where(fix_rows == CONV_SHIFT, taps[0] * old0 + taps[1] * old1, 0.0)
                         + jnp.where(fix_rows == CONV_SHIFT + 1, taps[0] * old1, 0.0))
                fixed = y[lo:lo + 16] - jnp.where(seq_start, cross, 0.0)
                y = jnp.concatenate([y[:lo], fixed, y[lo + 16:]], axis=0)
            y = _silu(y)
            if c < 2 * A_WIDTH:
                y = y * (lax.rsqrt(jnp.sum(y * y, axis=-1, keepdims=True) + EPS) * scale)
            qkv_ref[r:r + CONV_ROWS, lanes] = y.astype(BF16)

    pending = list(range(0, 3 * A_WIDTH, A_HEAD_DIM))

    def conv_some(n):
        for _ in range(n):
            if pending:
                conv_head(pending.pop(0))

    for c in range(0, 3 * A_WIDTH, CONV_COLS):
        acc = proj(wa_ref, c, CONV_COLS)
        for hc in range(0, CONV_COLS, A_HEAD_DIM):
            ext_s[(c + hc) // A_HEAD_DIM, CONV_HALO:CONV_HALO + tm, :] = acc[:, hc:hc + A_HEAD_DIM]
        if c >= CONV_COLS:
            conv_some(1)

    step = 512
    for c in range(0, A_WIDTH, step):
        ga_ref[:, c:c + step] = _silu(proj(wa_ref, 3 * A_WIDTH + c, step)).astype(BF16)
        conv_some(2)

    acc = _dot(hn, wg_ref[...])
    lane = lax.broadcasted_iota(jnp.int32, (tm, LANES), 1)
    g = -jnp.exp(alog_ref[...]) * _softplus(acc + dtb_ref[...])
    val = jnp.where(lane < 2 * A_HEADS, g, jax.nn.sigmoid(acc))
    val = jnp.where(lane < GATE_COLS, val, 0.0)
    pos = lax.broadcasted_iota(jnp.int32, (tm, LANES), 0) % CHUNK
    pre = val
    suf = val
    s = 1
    while s < CHUNK:
        pre = pre + jnp.where(pos >= s, pltpu.roll(pre, s, axis=0), 0.0)
        suf = suf + jnp.where(pos < CHUNK - s, pltpu.roll(suf, tm - s, axis=0), 0.0)
        s *= 2
    res = jnp.where(lane < A_HEADS, pre, jnp.where(lane < 2 * A_HEADS, suf, val))
    ab_ref[...] = res
    abt_ref[0] = res.T
    conv_some(1)

    cos = cos_ref[...]
    sin = sin_ref[...]
    qb = proj(wt_ref, T_QB, B_WIDTH)
    low_half = lax.broadcasted_iota(jnp.int32, (tm, LANES), 1) < B_HEAD_DIM
    for c in range(0, B_WIDTH, LANES):
        t = qb[:, c:c + LANES]
        sq = t * t
        s_lo = jnp.sum(jnp.where(low_half, sq, 0.0), axis=-1, keepdims=True)
        s_hi = jnp.sum(jnp.where(low_half, 0.0, sq), axis=-1, keepdims=True)
        ss = jnp.where(low_half, s_lo, s_hi)
        t = t * lax.rsqrt(ss * (1.0 / B_HEAD_DIM) + EPS) * qnb_ref[:, c:c + LANES]
        qb_ref[:, c:c + LANES] = (t * cos + _rope_partner(t) * sin).astype(BF16)
    conv_some(2)
    kvb = proj(wt_ref, T_KB, 2 * LANES)
    kb = _head_rms(kvb[:, :LANES], e64_ref[:LANES, :LANES], B_HEAD_DIM, knb_ref[...])
    kb_ref[...] = (kb * cos + _rope_partner(kb) * sin).astype(BF16)
    vb_ref[...] = kvb[:, LANES:].astype(BF16)
    conv_some(1)
    gb_ref[...] = _silu(proj(wt_ref, T_ZB, B_WIDTH)).astype(BF16)
    conv_some(2)
    qc = proj(wt_ref, T_QC, C_WIDTH)
    for c in range(0, C_WIDTH, C_HEAD_DIM):
        t = qc[:, c:c + C_HEAD_DIM]
        t = t * lax.rsqrt(jnp.mean(t * t, axis=-1, keepdims=True) + EPS) * qnc_ref[:, c:c + C_HEAD_DIM]
        qc_ref[:, c:c + C_HEAD_DIM] = t.astype(BF16)
    conv_some(2)
    gc_ref[...] = _silu(proj(wt_ref, T_ZC, C_WIDTH)).astype(BF16)
    conv_some(len(pending))
    ext_s[:, 0:CONV_HALO, :] = ext_s[:, tm:tm + CONV_HALO, :]


def _block_diag_ones(width, block):
    idx = np.arange(width) // block
    return jnp.asarray(idx[:, None] == idx[None, :], dtype=BF16)


def _rope_tables(seq):
    d = B_HEAD_DIM
    inv = ROPE_THETA ** (-jnp.arange(0, d, 2, dtype=F32) / d)
    ang = jnp.arange(seq, dtype=F32)[:, None] * inv[None, :]
    cos = jnp.cos(ang)
    sin = jnp.sin(ang)
    cos_h = jnp.concatenate([cos, cos], axis=-1)
    sin_h = jnp.concatenate([-sin, sin], axis=-1)
    reps = LANES // d
    return jnp.tile(cos_h, (1, reps)), jnp.tile(sin_h, (1, reps))


def _in_proj(x2, norm_w, w_a, w_g, w_t, conv_w, cos, sin, alog_row, dtb_row, qnb, knb, qnc, seq, tm=256):
    t = x2.shape[0]
    tiles_per_seq = seq // tm
    nt = t // tm
    row = lambda i: (jnp.minimum(i, nt - 1), 0)
    const = lambda i: (0, 0)
    pos = lambda i: (jnp.minimum(i, nt - 1) % tiles_per_seq, 0)
    in_specs = [
        pl.BlockSpec((tm, D_MODEL), row),
        pl.BlockSpec((1, D_MODEL), const),
        pl.BlockSpec((D_MODEL, A_COLS), const, pipeline_mode=pl.Buffered(1)),
        pl.BlockSpec((D_MODEL, LANES), const, pipeline_mode=pl.Buffered(1)),
        pl.BlockSpec((D_MODEL, T_COLS), const, pipeline_mode=pl.Buffered(1)),
        pl.BlockSpec((CONV_K, 1, 3 * A_WIDTH), lambda i: (0, 0, 0)),
        pl.BlockSpec((tm, LANES), pos),
        pl.BlockSpec((tm, LANES), pos),
        pl.BlockSpec((1, LANES), const),
        pl.BlockSpec((1, LANES), const),
        pl.BlockSpec((1, B_WIDTH), const),
        pl.BlockSpec((1, LANES), const),
        pl.BlockSpec((1, C_WIDTH), const),
        pl.BlockSpec((B_WIDTH, B_WIDTH), const),
    ]
    out_shape = [
        jax.ShapeDtypeStruct((t + tm, 3 * A_WIDTH), BF16),
        jax.ShapeDtypeStruct((t, A_WIDTH), BF16),
        jax.ShapeDtypeStruct((t, LANES), F32),
        jax.ShapeDtypeStruct((t // tm, LANES, tm), F32),
        jax.ShapeDtypeStruct((t, B_WIDTH), BF16),
        jax.ShapeDtypeStruct((t, LANES), BF16),
        jax.ShapeDtypeStruct((t, LANES), BF16),
        jax.ShapeDtypeStruct((t, B_WIDTH), BF16),
        jax.ShapeDtypeStruct((t, C_WIDTH), BF16),
        jax.ShapeDtypeStruct((t, C_WIDTH), BF16),
    ]
    out_specs = [
        pl.BlockSpec((tm, 3 * A_WIDTH), lambda i: (i, 0)),
        pl.BlockSpec((tm, A_WIDTH), row),
        pl.BlockSpec((tm, LANES), row),
        pl.BlockSpec((1, LANES, tm), lambda i: (jnp.minimum(i, nt - 1), 0, 0)),
        pl.BlockSpec((tm, B_WIDTH), row),
        pl.BlockSpec((tm, LANES), row),
        pl.BlockSpec((tm, LANES), row),
        pl.BlockSpec((tm, B_WIDTH), row),
        pl.BlockSpec((tm, C_WIDTH), row),
        pl.BlockSpec((tm, C_WIDTH), row),
    ]
    return pl.pallas_call(
        functools.partial(_in_proj_kernel, tiles_per_seq),
        grid=(nt + 1,),
        in_specs=in_specs,
        out_specs=out_specs,
        out_shape=out_shape,
        scratch_shapes=[pltpu.VMEM((3 * A_HEADS, CONV_HALO + tm, A_HEAD_DIM), F32)],
        compiler_params=pltpu.CompilerParams(dimension_semantics=("arbitrary",), vmem_limit_bytes=VMEM_LIMIT),
        name="in_proj",
    )(x2, norm_w, w_a, w_g, w_t, conv_w, cos, sin, alog_row, dtb_row, qnb, knb, qnc,
      _block_diag_ones(B_WIDTH, B_HEAD_DIM))


def _mem_proj_kernel(m_ref, nw_ref, w_ref, knc_ref, e128_ref, km_ref, vm_ref):
    x = m_ref[...]
    ms = jnp.mean(x * x, axis=-1, keepdims=True)
    mn = (x * lax.rsqrt(ms + EPS) * nw_ref[...]).astype(BF16)
    km = _dot(mn, w_ref[:, :C_WIDTH])
    km_ref[...] = _head_rms(km, e128_ref[...], C_HEAD_DIM, knc_ref[...]).astype(BF16)
    vm_ref[...] = _dot(mn, w_ref[:, C_WIDTH:]).astype(BF16)


def _mem_proj(mem2, mem_norm_w, w_kv, knc, tm=256):
    t = mem2.shape[0]
    row = lambda i: (i, 0)
    const = lambda i: (0, 0)
    return pl.pallas_call(
        _mem_proj_kernel,
        grid=(t // tm,),
        in_specs=[
            pl.BlockSpec((tm, D_MODEL), row),
            pl.BlockSpec((1, D_MODEL), const),
            pl.BlockSpec((D_MODEL, 2 * C_WIDTH), const),
            pl.BlockSpec((1, C_WIDTH), const),
            pl.BlockSpec((C_WIDTH, C_WIDTH), const),
        ],
        out_specs=[pl.BlockSpec((tm, C_WIDTH), row), pl.BlockSpec((tm, C_WIDTH), row)],
        out_shape=[jax.ShapeDtypeStruct((t, C_WIDTH), BF16), jax.ShapeDtypeStruct((t, C_WIDTH), BF16)],
        compiler_params=pltpu.CompilerParams(dimension_semantics=("arbitrary",), vmem_limit_bytes=VMEM_LIMIT),
        name="mem_proj",
    )(mem2, mem_norm_w, w_kv, knc, _block_diag_ones(C_WIDTH, C_HEAD_DIM))


HEADS_PER_STEP = 2
PACK = 4
PACK_ROWS = PACK * CHUNK
SQUARINGS = 5
PREP_PACKS = 2
PREP_LEVELS = SQUARINGS + 3


def _deltanet_kernel(q_ref, k_ref, v_ref, qt_ref, kt_ref, vt_ref, ab_ref, gfr_ref, gbr_ref, ga_ref, on_ref,
                     out_ref,
                     qs, ks, vs, kq_s, b_s, cd_s, o_s):
    seq = q_ref.shape[0]
    nc = seq // CHUNK
    hb = HEADS_PER_STEP
    h0 = pl.program_id(1) * hb
    slab = 256

    def slab_rows(i):
        return pl.ds(pl.multiple_of(i * slab, slab), slab)

    def stage(src_ref, tail_ref, dst_ref, hs):
        cols = slice(hs * LANES, (hs + 1) * LANES)

        def copy(i, carry):
            src_rows = pl.ds(pl.multiple_of(CONV_SHIFT + i * slab, CONV_SHIFT), slab)
            dst_ref[hs, slab_rows(i), :] = src_ref[src_rows, cols].astype(F32)
            return carry

        lax.fori_loop(0, seq // slab - 1, copy, 0)
        last = seq - slab
        dst_ref[hs, last:seq - CONV_SHIFT, :] = src_ref[last + CONV_SHIFT:seq, cols].astype(F32)
        dst_ref[hs, seq - CONV_SHIFT:seq, :] = tail_ref[:, cols].astype(F32)

    for hs in range(hb):
        stage(q_ref, qt_ref, qs, hs)
        stage(k_ref, kt_ref, ks, hs)
        stage(v_ref, vt_ref, vs, hs)

    ii = lax.broadcasted_iota(jnp.int32, (CHUNK, PACK_ROWS), 0)
    jl = lax.broadcasted_iota(jnp.int32, (CHUNK, PACK_ROWS), 1)
    lb = jl // CHUNK
    jj = jl % CHUNK
    block_diag = (lax.broadcasted_iota(jnp.int32, (PACK_ROWS, PACK_ROWS), 0) // CHUNK
                  == lax.broadcasted_iota(jnp.int32, (PACK_ROWS, PACK_ROWS), 1) // CHUNK)

    def pack_diag(full):
        out = full[0:CHUNK]
        for c in range(1, PACK):
            out = jnp.where(lb == c, full[c * CHUNK:(c + 1) * CHUNK], out)
        return out

    def pack_cols(col):
        out = None
        for c in range(PACK):
            blk = col[c * CHUNK:(c + 1) * CHUNK]
            blk = jnp.concatenate([blk, blk], axis=1)
            out = blk if out is None else jnp.where(lb == c, blk, out)
        return out

    def to_block_diag(packed):
        return jnp.where(block_diag, jnp.concatenate([packed] * PACK, axis=0), 0.0).astype(BF16)

    lane4 = lax.broadcasted_iota(jnp.int32, (PACK_ROWS, LANES), 1)
    npk = seq // PACK_ROWS
    chains = [(hs, d) for hs in range(hb) for d in range(2)]

    def prep_gen(it):
        members = []
        for pk in range(PREP_PACKS):
            for d in range(2):
                fwd = d == 0
                p = it * PREP_PACKS + pk
                if not fwd:
                    p = npk - 1 - p
                rows = pl.ds(pl.multiple_of(p * PACK_ROWS, PACK_ROWS), PACK_ROWS)
                ab4 = ab_ref[rows, :]

                def gate_col(col, ab4=ab4):
                    c = jnp.sum(jnp.where(lane4 == col, ab4, 0.0), axis=-1, keepdims=True)
                    return jnp.broadcast_to(c, (PACK_ROWS, LANES))

                incl = (ii >= jj) if fwd else (ii <= jj)
                strict = (ii > jj) if fwd else (ii < jj)
                for hs in range(hb):
                    k4 = ks[hs, rows, :]
                    q4 = qs[hs, rows, :]
                    v4 = vs[hs, rows, :]
                    k4b = k4.astype(BF16)
                    kk_p = pack_diag(_dot_nt(k4b, k4b))
                    qk_p = pack_diag(_dot_nt(q4.astype(BF16), k4b))
                    gc4 = gate_col(d * A_HEADS + h0 + hs)
                    bt4 = gate_col((2 + d) * A_HEADS + h0 + hs)
                    gcr = (gfr_ref if fwd else gbr_ref)[p, pl.ds(h0 + hs, 1), :]
                    decay = jnp.where(incl, jnp.exp(jnp.where(incl, pack_cols(gc4) - gcr, 0.0)), 0.0)
                    n_cat = jnp.where(strict, -(kk_p * decay * pack_cols(bt4)), 0.0)
                    qkm = jnp.where(incl, qk_p * decay, 0.0)
                    members.append((hs, d, k4, q4, v4, gc4, bt4, n_cat, qkm, p, rows))
        yield

        eye = jnp.where(ii == jj, 1.0, 0.0)
        tinv = [eye + m[7] for m in members]
        pw = [_dot(m[7].astype(BF16), to_block_diag(m[7])) for m in members]
        yield
        for level in range(1, SQUARINGS + 1):
            last = level == SQUARINGS
            for i in range(len(members)):
                rhs = to_block_diag(pw[i])
                if last:
                    tinv[i] = tinv[i] + _dot(tinv[i].astype(BF16), rhs)
                else:
                    both = _dot(jnp.concatenate([pw[i], tinv[i]], axis=0).astype(BF16), rhs)
                    pw[i] = both[:CHUNK]
                    tinv[i] = tinv[i] + both[CHUNK:]
            yield

        uws = []
        for (hs, d, k4, q4, v4, gc4, bt4, _, qkm, p, rows), t_p in zip(members, tinv):
            eg = jnp.exp(gc4)
            rhs = jnp.concatenate([v4 * bt4, k4 * bt4 * eg], axis=1).astype(BF16)
            uws.append((_dot(to_block_diag(t_p), rhs), eg))
        yield

        folded = []
        for (hs, d, k4, q4, v4, gc4, bt4, _, qkm, p, rows), (uw, eg) in zip(members, uws):
            fwd = d == 0
            uwb = uw.astype(BF16)
            per_chunk = []
            for c in range(PACK):
                rs = slice(c * CHUNK, (c + 1) * CHUNK)
                last_row = (c + 1) * CHUNK - 1 if fwd else c * CHUNK
                gl = gc4[last_row:last_row + 1, :]
                kd = k4[rs] * jnp.exp(gl - gc4[rs])
                lhs = jnp.concatenate([qkm[:, c * CHUNK:(c + 1) * CHUNK], kd.T], axis=0).astype(BF16)
                per_chunk.append((_dot(lhs, uwb[rs]), gl))
            folded.append(per_chunk)
        yield

        for (hs, d, k4, q4, v4, gc4, bt4, _, qkm, p, rows), (uw, eg), per_chunk in zip(members, uws, folded):
            qd = q4 * eg
            for c, (res, gl) in enumerate(per_chunk):
                n = p * PACK + c
                rs = slice(c * CHUNK, (c + 1) * CHUNK)
                o_s[hs, d, pl.ds(pl.multiple_of(n * CHUNK, CHUNK), CHUNK), :] = res[:CHUNK, :LANES]
                kq_s[hs, d, n, 0:LANES, :] = res[CHUNK:, LANES:].astype(BF16)
                kq_s[hs, d, n, LANES:LANES + CHUNK, :] = (qd[rs] - res[:CHUNK, LANES:]).astype(BF16)
                b_s[hs, d, n] = res[CHUNK:, :LANES]
                cd_s[hs, d, n] = jnp.broadcast_to(jnp.exp(gl), (8, LANES))

    steps_per_stage = PREP_PACKS * PACK

    def seq_gen(it, states, result):
        states = list(states)
        for k in range(steps_per_stage):
            i = it * steps_per_stage + k
            ns = (i, nc - 1 - i)
            rows = [pl.ds(pl.multiple_of(n * CHUNK, CHUNK), CHUNK) for n in ns]
            outs = [_dot(kq_s[hs, d, ns[d]], st.astype(BF16)) for (hs, d), st in zip(chains, states)]
            for c, ((hs, d), out) in enumerate(zip(chains, outs)):
                o_s[hs, d, rows[d], :] = o_s[hs, d, rows[d], :] + out[LANES:]
                states[c] = states[c] * cd_s[hs, d, ns[d]][0:1, :] - out[:LANES] + b_s[hs, d, ns[d]]
            yield
        result.append(tuple(states))

    def run(gen):
        for _ in gen:
            pass

    def fused_body(j, states):
        result = []
        pg = prep_gen(j + 1)
        sg = seq_gen(j, states, result)
        next(pg)
        for _ in range(steps_per_stage):
            for _ in range(PREP_LEVELS // steps_per_stage):
                next(pg)
            next(sg)
        run(sg)
        run(pg)
        return result[0]

    n_stages = npk // PREP_PACKS
    s0 = jnp.zeros((A_HEAD_DIM, A_HEAD_DIM), F32)
    run(prep_gen(0))
    states = lax.fori_loop(0, n_stages - 1, fused_body, tuple(s0 for _ in chains))
    run(seq_gen(n_stages - 1, states, []))

    def finish(i, carry):
        for hs in range(hb):
            cols = slice(hs * LANES, (hs + 1) * LANES)
            o = o_s[hs, 0, slab_rows(i), :] + o_s[hs, 1, slab_rows(i), :]
            y = o * lax.rsqrt(jnp.mean(o * o, axis=-1, keepdims=True) + EPS) * on_ref[...]
            out_ref[slab_rows(i), cols] = (y * ga_ref[slab_rows(i), cols].astype(F32)).astype(BF16)
        return carry

    lax.fori_loop(0, seq // slab, finish, 0, unroll=2)


def _deltanet(qkv, ab, abt3, ga, o_norm, batch, seq):
    nc = seq // CHUNK
    npk = seq // PACK_ROWS
    t = batch * seq
    hb = HEADS_PER_STEP
    hw = hb * LANES
    groups = A_HEADS // hb
    tail_blocks = seq // CONV_SHIFT
    scratch = [
        pltpu.VMEM((hb, seq, LANES), F32),
        pltpu.VMEM((hb, seq, LANES), F32),
        pltpu.VMEM((hb, seq, LANES), F32),
        pltpu.VMEM((hb, 2, nc, LANES + CHUNK, LANES), BF16),
        pltpu.VMEM((hb, 2, nc, LANES, LANES), F32),
        pltpu.VMEM((hb, 2, nc, 8, LANES), F32),
        pltpu.VMEM((hb, 2, seq, LANES), F32),
    ]
    return pl.pallas_call(
        _deltanet_kernel,
        grid=(batch, groups),
        in_specs=[
            pl.BlockSpec((seq, hw), lambda b, g: (b, g)),
            pl.BlockSpec((seq, hw), lambda b, g: (b, groups + g)),
            pl.BlockSpec((seq, hw), lambda b, g: (b, 2 * groups + g)),
            pl.BlockSpec((CONV_SHIFT, hw), lambda b, g: ((b + 1) * tail_blocks, g)),
            pl.BlockSpec((CONV_SHIFT, hw), lambda b, g: ((b + 1) * tail_blocks, groups + g)),
            pl.BlockSpec((CONV_SHIFT, hw), lambda b, g: ((b + 1) * tail_blocks, 2 * groups + g)),
            pl.BlockSpec((seq, LANES), lambda b, g: (b, 0)),
            pl.BlockSpec((npk, A_HEADS, PACK_ROWS), lambda b, g: (b, 0, 0)),
            pl.BlockSpec((npk, A_HEADS, PACK_ROWS), lambda b, g: (b, 1, 0)),
            pl.BlockSpec((seq, hw), lambda b, g: (b, g)),
            pl.BlockSpec((1, LANES), lambda b, g: (0, 0)),
        ],
        out_specs=pl.BlockSpec((seq, hw), lambda b, g: (b, g)),
        out_shape=jax.ShapeDtypeStruct((t, A_WIDTH), BF16),
        scratch_shapes=scratch,
        compiler_params=pltpu.CompilerParams(dimension_semantics=("arbitrary", "arbitrary"),
                                             vmem_limit_bytes=VMEM_LIMIT),
        name="deltanet",
    )(qkv, qkv, qkv, qkv, qkv, qkv, ab, abt3, abt3, ga, o_norm)


Q_BLOCKS = 4


def _attn_kernel(sink_ref, qb_ref, kb_ref, vb_ref, gb_ref, qc_ref, km_ref, vm_ref, gc_ref, out_ref):
    nb = pl.num_programs(1) * Q_BLOCKS
    w = WINDOW
    lane = lax.broadcasted_iota(jnp.int32, (w, LANES), 1)
    lower = lane < B_HEAD_DIM
    ri = lax.broadcasted_iota(jnp.int32, (2 * w, w), 0) % w
    ci = lax.broadcasted_iota(jnp.int32, (2 * w, w), 1)
    top = lax.broadcasted_iota(jnp.int32, (2 * w, 1), 0) < w

    first_blk = pl.program_id(1) * Q_BLOCKS - 1
    kv = []
    for i in range(Q_BLOCKS + 2):
        j = first_blk + i
        start = pl.multiple_of(jnp.clip(j, 0, nb - 1) * w, w)
        kblk = kb_ref[pl.ds(start, w), :].astype(F32)
        vblk = vb_ref[pl.ds(start, w), :].astype(F32)
        ksw = pltpu.roll(kblk, B_HEAD_DIM, axis=1)
        vsw = pltpu.roll(vblk, B_HEAD_DIM, axis=1)
        variants = {}
        for hk in range(B_KV_HEADS):
            for half in range(2):
                keep = lower if half == 0 else jnp.logical_not(lower)
                in_place = hk == half
                variants[hk, half] = (jnp.where(keep, kblk if in_place else ksw, 0.0).astype(BF16),
                                      jnp.where(keep, vblk if in_place else vsw, 0.0).astype(BF16))
        kv.append((variants, jnp.logical_and(j >= 0, j < nb)))

    for qi in range(Q_BLOCKS):
        rows = slice(qi * w, (qi + 1) * w)
        masks = (jnp.logical_and(ci >= ri, kv[qi][1]), None, jnp.logical_and(ci <= ri, kv[qi + 2][1]))
        for hk in range(B_KV_HEADS):
            c0 = hk * 2 * LANES
            q2 = jnp.concatenate([qb_ref[rows, c0:c0 + LANES], qb_ref[rows, c0 + LANES:c0 + 2 * LANES]], axis=0)
            acc = jnp.zeros((2 * w, LANES), F32)
            for half in range(2):
                sink_col = jnp.where(top, sink_ref[4 * hk + half], sink_ref[4 * hk + 2 + half])
                ss = []
                for rel in range(3):
                    s = _dot_nt(q2, kv[qi + rel][0][hk, half][0])
                    if masks[rel] is not None:
                        s = jnp.where(masks[rel], s, -jnp.inf)
                    ss.append(s)
                m = jnp.maximum(sink_col, jnp.max(jnp.maximum(jnp.maximum(ss[0], ss[1]), ss[2]),
                                                  axis=-1, keepdims=True))
                ps = [jnp.exp(s - m) for s in ss]
                den = jnp.exp(sink_col - m) + jnp.sum(ps[0] + ps[1] + ps[2], axis=-1, keepdims=True)
                pv = jnp.zeros((2 * w, LANES), F32)
                for rel in range(3):
                    pv = pv + _dot(ps[rel].astype(BF16), kv[qi + rel][0][hk, half][1])
                acc = acc + pv / den
            out_ref[rows, c0:c0 + LANES] = (acc[:w] * gb_ref[rows, c0:c0 + LANES].astype(F32)).astype(BF16)
            out_ref[rows, c0 + LANES:c0 + 2 * LANES] = (
                acc[w:] * gb_ref[rows, c0 + LANES:c0 + 2 * LANES].astype(F32)).astype(BF16)

    for hc in range(C_HEADS):
        c0 = hc * C_HEAD_DIM
        s = _dot_nt(qc_ref[:, c0:c0 + C_HEAD_DIM], km_ref[:, c0:c0 + C_HEAD_DIM])
        m = jnp.max(s, axis=-1, keepdims=True)
        p = jnp.exp(s - m)
        den = jnp.sum(p, axis=-1, keepdims=True)
        o = _dot(p.astype(BF16), vm_ref[:, c0:c0 + C_HEAD_DIM]) / den
        out_ref[:, B_WIDTH + c0:B_WIDTH + c0 + C_HEAD_DIM] = (
            o * gc_ref[:, c0:c0 + C_HEAD_DIM].astype(F32)).astype(BF16)


def _attention(sink, qb, kb, vb, gb, qc, km, vm, gcg, batch, seq):
    nb = seq // (WINDOW * Q_BLOCKS)
    qw = WINDOW * Q_BLOCKS
    t = batch * seq
    mlen = km.shape[0] // batch
    qrow = lambda b, n: (b * nb + n, 0)
    per_b = lambda b, n: (b, 0)
    return pl.pallas_call(
        _attn_kernel,
        grid=(batch, nb),
        in_specs=[
            pl.BlockSpec(memory_space=pltpu.SMEM),
            pl.BlockSpec((qw, B_WIDTH), qrow),
            pl.BlockSpec((seq, LANES), per_b),
            pl.BlockSpec((seq, LANES), per_b),
            pl.BlockSpec((qw, B_WIDTH), qrow),
            pl.BlockSpec((qw, C_WIDTH), qrow),
            pl.BlockSpec((mlen, C_WIDTH), per_b),
            pl.BlockSpec((mlen, C_WIDTH), per_b),
            pl.BlockSpec((qw, C_WIDTH), qrow),
        ],
        out_specs=pl.BlockSpec((qw, B_WIDTH + C_WIDTH), qrow),
        out_shape=jax.ShapeDtypeStruct((t, B_WIDTH + C_WIDTH), BF16),
        compiler_params=pltpu.CompilerParams(dimension_semantics=("arbitrary", "arbitrary"),
                                             vmem_limit_bytes=VMEM_LIMIT),
        name="attention",
    )(sink, qb, kb, vb, gb, qc, km, vm, gcg)


def _out_proj_kernel(x_ref, ma_ref, mbc_ref, wa_ref, wbc_ref, out_ref):
    out_ref[...] = x_ref[...] + _dot(ma_ref[...], wa_ref[...]) + _dot(mbc_ref[...], wbc_ref[...])


def _out_proj(x2, mixed_a, mixed_bc, w_out, tm=512):
    t = x2.shape[0]
    row = lambda i: (i, 0)
    const = lambda i: (0, 0)
    return pl.pallas_call(
        _out_proj_kernel,
        grid=(t // tm,),
        in_specs=[
            pl.BlockSpec((tm, D_MODEL), row),
            pl.BlockSpec((tm, A_WIDTH), row),
            pl.BlockSpec((tm, B_WIDTH + C_WIDTH), row),
            pl.BlockSpec((A_WIDTH, D_MODEL), const),
            pl.BlockSpec((B_WIDTH + C_WIDTH, D_MODEL), lambda i: (1, 0)),
        ],
        out_specs=pl.BlockSpec((tm, D_MODEL), row),
        out_shape=jax.ShapeDtypeStruct((t, D_MODEL), F32),
        compiler_params=pltpu.CompilerParams(dimension_semantics=("arbitrary",), vmem_limit_bytes=VMEM_LIMIT),
        name="out_proj",
    )(x2, mixed_a, mixed_bc, w_out, w_out)


def _layer(h, mem, norm_w, w_in_t, conv_w_a, a_log_fwd, a_log_bwd, dt_bias_fwd, dt_bias_bwd, o_norm_a,
           q_norm_b, k_norm_b, sink_b, mem_norm_w, w_mem_kv, q_norm_c, k_norm_c, w_out):
    batch, seq, _ = h.shape
    t = batch * seq
    x2 = h.reshape(t, D_MODEL)

    w_a, w_g, w_t = _w_prep(w_in_t)
    zeros = jnp.zeros((LANES - 2 * A_HEADS,), F32)
    alog_row = jnp.concatenate([a_log_fwd, a_log_bwd, zeros]).reshape(1, LANES)
    dtb_row = jnp.concatenate([dt_bias_fwd, dt_bias_bwd, zeros]).reshape(1, LANES)
    cos, sin = _rope_tables(seq)

    qkv, ga, ab, abt, qb, kb, vb, gb, qc, gcg = _in_proj(
        x2, norm_w.reshape(1, D_MODEL), w_a, w_g, w_t, conv_w_a, cos, sin, alog_row, dtb_row,
        (jnp.tile(q_norm_b, B_Q_HEADS) * B_HEAD_DIM ** -0.5).reshape(1, B_WIDTH),
        jnp.tile(k_norm_b, B_KV_HEADS).reshape(1, LANES),
        (jnp.tile(q_norm_c, C_HEADS) * C_HEAD_DIM ** -0.5).reshape(1, C_WIDTH), seq)

    mlen = mem.shape[1]
    km, vm = _mem_proj(mem.reshape(batch * mlen, D_MODEL), mem_norm_w.reshape(1, D_MODEL),
                       w_mem_kv.astype(BF16), jnp.tile(k_norm_c, C_HEADS).reshape(1, C_WIDTH))

    mixed_a = _deltanet(qkv, ab, abt, ga,
                        o_norm_a.reshape(1, A_HEAD_DIM), batch, seq)
    mixed_bc = _attention(sink_b, qb, kb, vb, gb, qc, km, vm, gcg, batch, seq)

    out = _out_proj(x2, mixed_a, mixed_bc, w_out.astype(BF16))
    return out.reshape(batch, seq, D_MODEL)


def kernel(x, mem, norm_w, w_in, conv_w_a, a_log_fwd, a_log_bwd, dt_bias_fwd, dt_bias_bwd, o_norm_a,
           q_norm_b, k_norm_b, sink_b, mem_norm_w, w_mem_kv, q_norm_c, k_norm_c, w_out):
    h = x
    w_in_t = jnp.swapaxes(w_in, 1, 2)
    conv_taps = jnp.swapaxes(conv_w_a, 0, 1)
    for l in range(norm_w.shape[0]):
        h = _layer(h, mem, norm_w[l], w_in_t[l], conv_taps[:, l:l + 1], a_log_fwd[l], a_log_bwd[l], dt_bias_fwd[l],
                   dt_bias_bwd[l], o_norm_a[l], q_norm_b[l], k_norm_b[l], sink_b[l], mem_norm_w[l],
                   w_mem_kv[l], q_norm_c[l], k_norm_c[l], w_out[l])
    return h
```

```python
import functools

import jax
import jax.numpy as jnp
from jax import lax
from jax.experimental import pallas as pl
from jax.experimental.pallas import tpu as pltpu

F32 = jnp.float32
BF16 = jnp.bfloat16

D_MODEL = 2048
A_WIDTH = 1024
A_HEAD_DIM = 128
A_HEADS = 8
CONV_K = 5
CHUNK = 64
B_WIDTH = 512
B_HEAD_DIM = 64
B_Q_HEADS = 8
B_KV_HEADS = 2
WINDOW = 128
C_WIDTH = 512
C_HEADS = 4
C_HEAD_DIM = 128
ROPE_THETA = 10000.0
EPS = 1e-6

LANES = 128
GATE_COLS = 4 * A_HEADS
A_COLS = 4 * A_WIDTH
T_QB = 0
T_KB = T_QB + B_WIDTH
T_VB = T_KB + B_KV_HEADS * B_HEAD_DIM
T_ZB = T_VB + B_KV_HEADS * B_HEAD_DIM
T_QC = T_ZB + B_WIDTH
T_ZC = T_QC + C_WIDTH
T_COLS = T_ZC + C_WIDTH

VMEM_LIMIT = 56 * 1024 * 1024
CONV_SHIFT = 16
CONV_HALO = 24
CONV_COLS = 256
CONV_ROWS = 128

_NT = (((1,), (1,)), ((), ()))


def _dot(a, b):
    return jnp.dot(a, b, preferred_element_type=F32)


def _dot_nt(a, b):
    return lax.dot_general(a, b, _NT, preferred_element_type=F32)


def _silu(z):
    return z * jax.nn.sigmoid(z)


def _softplus(z):
    return jnp.maximum(z, 0.0) + jnp.log1p(jnp.exp(-jnp.abs(z)))


def _rope_partner(x):
    lane = lax.broadcasted_iota(jnp.int32, x.shape, 1)
    first_half = (lane % B_HEAD_DIM) < (B_HEAD_DIM // 2)
    up = pltpu.roll(x, LANES - B_HEAD_DIM // 2, axis=1)
    down = pltpu.roll(x, B_HEAD_DIM // 2, axis=1)
    return jnp.where(first_half, up, down)


def _half_lane_rms(t, w):
    low_half = lax.broadcasted_iota(jnp.int32, t.shape, 1) < B_HEAD_DIM
    sq = t * t
    s_lo = jnp.sum(jnp.where(low_half, sq, 0.0), axis=-1, keepdims=True)
    s_hi = jnp.sum(jnp.where(low_half, 0.0, sq), axis=-1, keepdims=True)
    ss = jnp.where(low_half, s_lo, s_hi)
    return t * lax.rsqrt(ss * (1.0 / B_HEAD_DIM) + EPS) * w


def _lane_group_rms(t, w):
    return t * lax.rsqrt(jnp.mean(t * t, axis=-1, keepdims=True) + EPS) * w


def _w_main_kernel(w_ref, o_ref):
    o_ref[...] = w_ref[...].T.astype(BF16)


def _w_rest_kernel(g_ref, w_ref, wg_ref, wt_ref):
    g = g_ref[...].T
    wg_ref[...] = jnp.concatenate([g, jnp.zeros((g.shape[0], LANES - GATE_COLS), F32)], axis=1).astype(BF16)
    wt_ref[...] = w_ref[...].T.astype(BF16)


def _w_prep(w_t, tr=256):
    cols = w_t.shape[1]
    w_a = pl.pallas_call(
        _w_main_kernel,
        grid=(A_COLS // tr,),
        in_specs=[pl.BlockSpec((tr, cols), lambda i: (i, 0))],
        out_specs=pl.BlockSpec((cols, tr), lambda i: (0, i)),
        out_shape=jax.ShapeDtypeStruct((cols, A_COLS), BF16),
        compiler_params=pltpu.CompilerParams(dimension_semantics=("arbitrary",), vmem_limit_bytes=VMEM_LIMIT),
        name="w_main",
    )(w_t)
    w_g, w_tail = pl.pallas_call(
        _w_rest_kernel,
        grid=(T_COLS // tr,),
        in_specs=[pl.BlockSpec((GATE_COLS, cols), lambda i: (A_COLS // GATE_COLS, 0)),
                  pl.BlockSpec((pl.Element(tr), pl.Element(cols)),
                               lambda i: (pl.multiple_of(A_COLS + GATE_COLS + i * tr, GATE_COLS), 0))],
        out_specs=[pl.BlockSpec((cols, LANES), lambda i: (0, 0)), pl.BlockSpec((cols, tr), lambda i: (0, i))],
        out_shape=[jax.ShapeDtypeStruct((cols, LANES), BF16), jax.ShapeDtypeStruct((cols, T_COLS), BF16)],
        compiler_params=pltpu.CompilerParams(dimension_semantics=("arbitrary",), vmem_limit_bytes=VMEM_LIMIT),
        name="w_rest",
    )(w_t, w_t)
    return w_a, w_g, w_tail


def _in_proj_kernel(tiles_per_seq, x_ref, nw_ref, wa_ref, wg_ref, wt_ref, cw_ref, cos_ref, sin_ref, alog_ref,
                    dtb_ref, qnb_ref, knb_ref, qnc_ref,
                    qkv_ref, ga_ref, ab_ref, abt_ref, qb_ref, kb_ref, vb_ref, gb_ref, qc_ref, gc_ref,
                    ext_s):
    step_id = pl.program_id(0)

    @pl.when(step_id == 0)
    def _():
        ext_s[:, 0:CONV_HALO, :] = jnp.zeros((ext_s.shape[0], CONV_HALO, A_HEAD_DIM), F32)

    seq_start = (step_id % tiles_per_seq) == 0
    zero = jnp.minimum(step_id, 0)
    x = x_ref[...]
    ms = jnp.mean(x * x, axis=-1, keepdims=True)
    hn = (x * lax.rsqrt(ms + EPS) * nw_ref[...]).astype(BF16)
    tm = x.shape[0]

    def proj(w_ref, c0, width):
        return _dot(hn, w_ref[:, c0:c0 + width])

    fix_rows = lax.broadcasted_iota(jnp.int32, (2 * 8, 1), 0) + (CONV_SHIFT - 8)
    first = CONV_HALO - CONV_SHIFT - CONV_K // 2
    lo = CONV_SHIFT - 8

    def conv_head(c):
        lanes = slice(c, c + A_HEAD_DIM)
        head = c // A_HEAD_DIM
        w = cw_ref[:, 0, lanes]
        taps = [w[j:j + 1, :] for j in range(CONV_K)]
        scale = A_HEAD_DIM ** -0.5 if c < A_WIDTH else 1.0
        for r in range(0, tm, CONV_ROWS):
            y = ext_s[head, pl.ds(zero + (first + r), CONV_ROWS), :] * taps[0]
            for j in range(1, CONV_K):
                y = y + ext_s[head, pl.ds(zero + (first + r + j), CONV_ROWS), :] * taps[j]
            if r == 0:
                cur0 = ext_s[head, CONV_HALO:CONV_HALO + 1, :]
                cur1 = ext_s[head, CONV_HALO + 1:CONV_HALO + 2, :]
                old0 = ext_s[head, CONV_HALO - 2:CONV_HALO - 1, :]
                old1 = ext_s[head, CONV_HALO - 1:CONV_HALO, :]
                cross = (jnp.where(fix_rows == CONV_SHIFT - 2, taps[4] * cur0, 0.0)
                         + jnp.where(fix_rows == CONV_SHIFT - 1, taps[3] * cur0 + taps[4] * cur1, 0.0)
                         + jnp.where(fix_rows == CONV_SHIFT, taps[0] * old0 + taps[1] * old1, 0.0)
                         + jnp.where(fix_rows == CONV_SHIFT + 1, taps[0] * old1, 0.0))
                fixed = y[lo:lo + 16] - jnp.where(seq_start, cross, 0.0)
                y = jnp.concatenate([y[:lo], fixed, y[lo + 16:]], axis=0)
            y = _silu(y)
            if c < 2 * A_WIDTH:
                y = y * (lax.rsqrt(jnp.sum(y * y, axis=-1, keepdims=True) + EPS) * scale)
            qkv_ref[r:r + CONV_ROWS, lanes] = y.astype(BF16)

    pending = list(range(0, 3 * A_WIDTH, A_HEAD_DIM))

    def conv_some(n):
        for _ in range(n):
            if pending:
                conv_head(pending.pop(0))

    for c in range(0, 3 * A_WIDTH, CONV_COLS):
        acc = proj(wa_ref, c, CONV_COLS)
        for hc in range(0, CONV_COLS, A_HEAD_DIM):
            ext_s[(c + hc) // A_HEAD_DIM, CONV_HALO:CONV_HALO + tm, :] = acc[:, hc:hc + A_HEAD_DIM]
        if c >= CONV_COLS:
            conv_some(1)

    step = 512
    for c in range(0, A_WIDTH, step):
        ga_ref[:, c:c + step] = _silu(proj(wa_ref, 3 * A_WIDTH + c, step)).astype(BF16)
        conv_some(2)

    acc = _dot(hn, wg_ref[...])
    lane = lax.broadcasted_iota(jnp.int32, (tm, LANES), 1)
    g = -jnp.exp(alog_ref[...]) * _softplus(acc + dtb_ref[...])
    val = jnp.where(lane < 2 * A_HEADS, g, jax.nn.sigmoid(acc))
    val = jnp.where(lane < GATE_COLS, val, 0.0)
    pos = lax.broadcasted_iota(jnp.int32, (tm, LANES), 0) % CHUNK
    pre = val
    suf = val
    s = 1
    while s < CHUNK:
        pre = pre + jnp.where(pos >= s, pltpu.roll(pre, s, axis=0), 0.0)
        suf = suf + jnp.where(pos < CHUNK - s, pltpu.roll(suf, tm - s, axis=0), 0.0)
        s *= 2
    res = jnp.where(lane < A_HEADS, pre, jnp.where(lane < 2 * A_HEADS, suf, val))
    ab_ref[...] = res
    abt_ref[0] = res.T
    conv_some(1)

    cos = cos_ref[...]
    sin = sin_ref[...]
    qb = proj(wt_ref, T_QB, B_WIDTH)
    for c in range(0, B_WIDTH, LANES):
        t = _half_lane_rms(qb[:, c:c + LANES], qnb_ref[:, c:c + LANES])
        qb_ref[:, c:c + LANES] = (t * cos + _rope_partner(t) * sin).astype(BF16)
    conv_some(2)
    kvb = proj(wt_ref, T_KB, 2 * LANES)
    kb = _half_lane_rms(kvb[:, :LANES], knb_ref[...])
    kb_ref[...] = (kb * cos + _rope_partner(kb) * sin).astype(BF16)
    vb_ref[...] = kvb[:, LANES:].astype(BF16)
    conv_some(1)
    gb_ref[...] = _silu(proj(wt_ref, T_ZB, B_WIDTH)).astype(BF16)
    conv_some(2)
    qc = proj(wt_ref, T_QC, C_WIDTH)
    for c in range(0, C_WIDTH, C_HEAD_DIM):
        qc_ref[:, c:c + C_HEAD_DIM] = _lane_group_rms(qc[:, c:c + C_HEAD_DIM],
                                                      qnc_ref[:, c:c + C_HEAD_DIM]).astype(BF16)
    conv_some(2)
    gc_ref[...] = _silu(proj(wt_ref, T_ZC, C_WIDTH)).astype(BF16)
    conv_some(len(pending))
    ext_s[:, 0:CONV_HALO, :] = ext_s[:, tm:tm + CONV_HALO, :]


def _rope_tables(seq):
    d = B_HEAD_DIM
    inv = ROPE_THETA ** (-jnp.arange(0, d, 2, dtype=F32) / d)
    ang = jnp.arange(seq, dtype=F32)[:, None] * inv[None, :]
    cos = jnp.cos(ang)
    sin = jnp.sin(ang)
    cos_h = jnp.concatenate([cos, cos], axis=-1)
    sin_h = jnp.concatenate([-sin, sin], axis=-1)
    reps = LANES // d
    return jnp.tile(cos_h, (1, reps)), jnp.tile(sin_h, (1, reps))


def _in_proj(x2, norm_w, w_a, w_g, w_t, conv_w, cos, sin, alog_row, dtb_row, qnb, knb, qnc, seq, tm=256):
    t = x2.shape[0]
    tiles_per_seq = seq // tm
    nt = t // tm
    row = lambda i: (jnp.minimum(i, nt - 1), 0)
    const = lambda i: (0, 0)
    pos = lambda i: (jnp.minimum(i, nt - 1) % tiles_per_seq, 0)
    in_specs = [
        pl.BlockSpec((tm, D_MODEL), row),
        pl.BlockSpec((1, D_MODEL), const),
        pl.BlockSpec((D_MODEL, A_COLS), const, pipeline_mode=pl.Buffered(1)),
        pl.BlockSpec((D_MODEL, LANES), const, pipeline_mode=pl.Buffered(1)),
        pl.BlockSpec((D_MODEL, T_COLS), const, pipeline_mode=pl.Buffered(1)),
        pl.BlockSpec((CONV_K, 1, 3 * A_WIDTH), lambda i: (0, 0, 0)),
        pl.BlockSpec((tm, LANES), pos),
        pl.BlockSpec((tm, LANES), pos),
        pl.BlockSpec((1, LANES), const),
        pl.BlockSpec((1, LANES), const),
        pl.BlockSpec((1, B_WIDTH), const),
        pl.BlockSpec((1, LANES), const),
        pl.BlockSpec((1, C_WIDTH), const),
    ]
    out_shape = [
        jax.ShapeDtypeStruct((t + tm, 3 * A_WIDTH), BF16),
        jax.ShapeDtypeStruct((t, A_WIDTH), BF16),
        jax.ShapeDtypeStruct((t, LANES), F32),
        jax.ShapeDtypeStruct((t // tm, LANES, tm), F32),
        jax.ShapeDtypeStruct((t, B_WIDTH), BF16),
        jax.ShapeDtypeStruct((t, LANES), BF16),
        jax.ShapeDtypeStruct((t, LANES), BF16),
        jax.ShapeDtypeStruct((t, B_WIDTH), BF16),
        jax.ShapeDtypeStruct((t, C_WIDTH), BF16),
        jax.ShapeDtypeStruct((t, C_WIDTH), BF16),
    ]
    out_specs = [
        pl.BlockSpec((tm, 3 * A_WIDTH), lambda i: (i, 0)),
        pl.BlockSpec((tm, A_WIDTH), row),
        pl.BlockSpec((tm, LANES), row),
        pl.BlockSpec((1, LANES, tm), lambda i: (jnp.minimum(i, nt - 1), 0, 0)),
        pl.BlockSpec((tm, B_WIDTH), row),
        pl.BlockSpec((tm, LANES), row),
        pl.BlockSpec((tm, LANES), row),
        pl.BlockSpec((tm, B_WIDTH), row),
        pl.BlockSpec((tm, C_WIDTH), row),
        pl.BlockSpec((tm, C_WIDTH), row),
    ]
    return pl.pallas_call(
        functools.partial(_in_proj_kernel, tiles_per_seq),
        grid=(nt + 1,),
        in_specs=in_specs,
        out_specs=out_specs,
        out_shape=out_shape,
        scratch_shapes=[pltpu.VMEM((3 * A_HEADS, CONV_HALO + tm, A_HEAD_DIM), F32)],
        compiler_params=pltpu.CompilerParams(dimension_semantics=("arbitrary",), vmem_limit_bytes=VMEM_LIMIT),
        name="in_proj",
    )(x2, norm_w, w_a, w_g, w_t, conv_w, cos, sin, alog_row, dtb_row, qnb, knb, qnc)


def _mem_proj_kernel(m_ref, nw_ref, w_ref, knc_ref, km_ref, vm_ref):
    x = m_ref[...]
    ms = jnp.mean(x * x, axis=-1, keepdims=True)
    mn = (x * lax.rsqrt(ms + EPS) * nw_ref[...]).astype(BF16)
    km = _dot(mn, w_ref[:, :C_WIDTH])
    for c in range(0, C_WIDTH, C_HEAD_DIM):
        km_ref[:, c:c + C_HEAD_DIM] = _lane_group_rms(km[:, c:c + C_HEAD_DIM],
                                                      knc_ref[:, c:c + C_HEAD_DIM]).astype(BF16)
    vm_ref[...] = _dot(mn, w_ref[:, C_WIDTH:]).astype(BF16)


def _mem_proj(mem2, mem_norm_w, w_kv, knc, tm=256):
    t = mem2.shape[0]
    row = lambda i: (i, 0)
    const = lambda i: (0, 0)
    return pl.pallas_call(
        _mem_proj_kernel,
        grid=(t // tm,),
        in_specs=[
            pl.BlockSpec((tm, D_MODEL), row),
            pl.BlockSpec((1, D_MODEL), const),
            pl.BlockSpec((D_MODEL, 2 * C_WIDTH), const),
            pl.BlockSpec((1, C_WIDTH), const),
        ],
        out_specs=[pl.BlockSpec((tm, C_WIDTH), row), pl.BlockSpec((tm, C_WIDTH), row)],
        out_shape=[jax.ShapeDtypeStruct((t, C_WIDTH), BF16), jax.ShapeDtypeStruct((t, C_WIDTH), BF16)],
        compiler_params=pltpu.CompilerParams(dimension_semantics=("arbitrary",), vmem_limit_bytes=VMEM_LIMIT),
        name="mem_proj",
    )(mem2, mem_norm_w, w_kv, knc)


HEADS_PER_STEP = 2
PACK = 4
PACK_ROWS = PACK * CHUNK
SQUARINGS = 5
PREP_PACKS = 2
PREP_LEVELS = SQUARINGS + 3


def _deltanet_kernel(q_ref, k_ref, v_ref, qt_ref, kt_ref, vt_ref, ab_ref, gfr_ref, gbr_ref, ga_ref, on_ref,
                     out_ref,
                     qs, ks, vs, kq_s, b_s, cd_s, o_s):
    seq = q_ref.shape[0]
    nc = seq // CHUNK
    hb = HEADS_PER_STEP
    h0 = pl.program_id(1) * hb
    slab = 256

    def slab_rows(i):
        return pl.ds(pl.multiple_of(i * slab, slab), slab)

    def stage(src_ref, tail_ref, dst_ref, hs):
        cols = slice(hs * LANES, (hs + 1) * LANES)

        def copy(i, carry):
            src_rows = pl.ds(pl.multiple_of(CONV_SHIFT + i * slab, CONV_SHIFT), slab)
            dst_ref[hs, slab_rows(i), :] = src_ref[src_rows, cols].astype(F32)
            return carry

        lax.fori_loop(0, seq // slab - 1, copy, 0)
        last = seq - slab
        dst_ref[hs, last:seq - CONV_SHIFT, :] = src_ref[last + CONV_SHIFT:seq, cols].astype(F32)
        dst_ref[hs, seq - CONV_SHIFT:seq, :] = tail_ref[:, cols].astype(F32)

    for hs in range(hb):
        stage(q_ref, qt_ref, qs, hs)
        stage(k_ref, kt_ref, ks, hs)
        stage(v_ref, vt_ref, vs, hs)

    ii = lax.broadcasted_iota(jnp.int32, (CHUNK, PACK_ROWS), 0)
    jl = lax.broadcasted_iota(jnp.int32, (CHUNK, PACK_ROWS), 1)
    lb = jl // CHUNK
    jj = jl % CHUNK
    block_diag = (lax.broadcasted_iota(jnp.int32, (PACK_ROWS, PACK_ROWS), 0) // CHUNK
                  == lax.broadcasted_iota(jnp.int32, (PACK_ROWS, PACK_ROWS), 1) // CHUNK)

    def pack_diag(full):
        out = full[0:CHUNK]
        for c in range(1, PACK):
            out = jnp.where(lb == c, full[c * CHUNK:(c + 1) * CHUNK], out)
        return out

    def pack_cols(col):
        out = None
        for c in range(PACK):
            blk = col[c * CHUNK:(c + 1) * CHUNK]
            blk = jnp.concatenate([blk, blk], axis=1)
            out = blk if out is None else jnp.where(lb == c, blk, out)
        return out

    def to_block_diag(packed):
        return jnp.where(block_diag, jnp.concatenate([packed] * PACK, axis=0), 0.0).astype(BF16)

    lane4 = lax.broadcasted_iota(jnp.int32, (PACK_ROWS, LANES), 1)
    npk = seq // PACK_ROWS
    chains = [(hs, d) for hs in range(hb) for d in range(2)]

    def prep_gen(it):
        members = []
        for pk in range(PREP_PACKS):
            for d in range(2):
                fwd = d == 0
                p = it * PREP_PACKS + pk
                if not fwd:
                    p = npk - 1 - p
                rows = pl.ds(pl.multiple_of(p * PACK_ROWS, PACK_ROWS), PACK_ROWS)
                ab4 = ab_ref[rows, :]

                def gate_col(col, ab4=ab4):
                    c = jnp.sum(jnp.where(lane4 == col, ab4, 0.0), axis=-1, keepdims=True)
                    return jnp.broadcast_to(c, (PACK_ROWS, LANES))

                incl = (ii >= jj) if fwd else (ii <= jj)
                strict = (ii > jj) if fwd else (ii < jj)
                for hs in range(hb):
                    k4 = ks[hs, rows, :]
                    q4 = qs[hs, rows, :]
                    v4 = vs[hs, rows, :]
                    k4b = k4.astype(BF16)
                    kk_p = pack_diag(_dot_nt(k4b, k4b))
                    qk_p = pack_diag(_dot_nt(q4.astype(BF16), k4b))
                    gc4 = gate_col(d * A_HEADS + h0 + hs)
                    bt4 = gate_col((2 + d) * A_HEADS + h0 + hs)
                    gcr = (gfr_ref if fwd else gbr_ref)[p, pl.ds(h0 + hs, 1), :]
                    decay = jnp.where(incl, jnp.exp(jnp.where(incl, pack_cols(gc4) - gcr, 0.0)), 0.0)
                    n_cat = jnp.where(strict, -(kk_p * decay * pack_cols(bt4)), 0.0)
                    qkm = jnp.where(incl, qk_p * decay, 0.0)
                    members.append((hs, d, k4, q4, v4, gc4, bt4, n_cat, qkm, p, rows))
        yield

        eye = jnp.where(ii == jj, 1.0, 0.0)
        tinv = [eye + m[7] for m in members]
        pw = [_dot(m[7].astype(BF16), to_block_diag(m[7])) for m in members]
        yield
        for level in range(1, SQUARINGS + 1):
            last = level == SQUARINGS
            for i in range(len(members)):
                rhs = to_block_diag(pw[i])
                if last:
                    tinv[i] = tinv[i] + _dot(tinv[i].astype(BF16), rhs)
                else:
                    both = _dot(jnp.concatenate([pw[i], tinv[i]], axis=0).astype(BF16), rhs)
                    pw[i] = both[:CHUNK]
                    tinv[i] = tinv[i] + both[CHUNK:]
            yield

        uws = []
        for (hs, d, k4, q4, v4, gc4, bt4, _, qkm, p, rows), t_p in zip(members, tinv):
            eg = jnp.exp(gc4)
            rhs = jnp.concatenate([v4 * bt4, k4 * bt4 * eg], axis=1).astype(BF16)
            uws.append((_dot(to_block_diag(t_p), rhs), eg))
        yield

        folded = []
        for (hs, d, k4, q4, v4, gc4, bt4, _, qkm, p, rows), (uw, eg) in zip(members, uws):
            fwd = d == 0
            uwb = uw.astype(BF16)
            per_chunk = []
            for c in range(PACK):
                rs = slice(c * CHUNK, (c + 1) * CHUNK)
                last_row = (c + 1) * CHUNK - 1 if fwd else c * CHUNK
                gl = gc4[last_row:last_row + 1, :]
                kd = k4[rs] * jnp.exp(gl - gc4[rs])
                lhs = jnp.concatenate([qkm[:, c * CHUNK:(c + 1) * CHUNK], kd.T], axis=0).astype(BF16)
                per_chunk.append((_dot(lhs, uwb[rs]), gl))
            folded.append(per_chunk)
        yield

        for (hs, d, k4, q4, v4, gc4, bt4, _, qkm, p, rows), (uw, eg), per_chunk in zip(members, uws, folded):
            qd = q4 * eg
            for c, (res, gl) in enumerate(per_chunk):
                n = p * PACK + c
                rs = slice(c * CHUNK, (c + 1) * CHUNK)
                o_s[hs, d, pl.ds(pl.multiple_of(n * CHUNK, CHUNK), CHUNK), :] = res[:CHUNK, :LANES]
                kq_s[hs, d, n, 0:LANES, :] = res[CHUNK:, LANES:].astype(BF16)
                kq_s[hs, d, n, LANES:LANES + CHUNK, :] = (qd[rs] - res[:CHUNK, LANES:]).astype(BF16)
                b_s[hs, d, n] = res[CHUNK:, :LANES]
                cd_s[hs, d, n] = jnp.broadcast_to(jnp.exp(gl), (8, LANES))

    steps_per_stage = PREP_PACKS * PACK

    def seq_gen(it, states, result):
        states = list(states)
        for k in range(steps_per_stage):
            i = it * steps_per_stage + k
            ns = (i, nc - 1 - i)
            rows = [pl.ds(pl.multiple_of(n * CHUNK, CHUNK), CHUNK) for n in ns]
            outs = [_dot(kq_s[hs, d, ns[d]], st.astype(BF16)) for (hs, d), st in zip(chains, states)]
            for c, ((hs, d), out) in enumerate(zip(chains, outs)):
                o_s[hs, d, rows[d], :] = o_s[hs, d, rows[d], :] + out[LANES:]
                states[c] = states[c] * cd_s[hs, d, ns[d]][0:1, :] - out[:LANES] + b_s[hs, d, ns[d]]
            yield
        result.append(tuple(states))

    def run(gen):
        for _ in gen:
            pass

    def fused_body(j, states):
        result = []
        pg = prep_gen(j + 1)
        sg = seq_gen(j, states, result)
        next(pg)
        for _ in range(steps_per_stage):
            for _ in range(PREP_LEVELS // steps_per_stage):
                next(pg)
            next(sg)
        run(sg)
        run(pg)
        return result[0]

    n_stages = npk // PREP_PACKS
    s0 = jnp.zeros((A_HEAD_DIM, A_HEAD_DIM), F32)
    run(prep_gen(0))
    states = lax.fori_loop(0, n_stages - 1, fused_body, tuple(s0 for _ in chains))
    run(seq_gen(n_stages - 1, states, []))

    def finish(i, carry):
        for hs in range(hb):
            cols = slice(hs * LANES, (hs + 1) * LANES)
            o = o_s[hs, 0, slab_rows(i), :] + o_s[hs, 1, slab_rows(i), :]
            y = o * lax.rsqrt(jnp.mean(o * o, axis=-1, keepdims=True) + EPS) * on_ref[...]
            out_ref[slab_rows(i), cols] = (y * ga_ref[slab_rows(i), cols].astype(F32)).astype(BF16)
        return carry

    lax.fori_loop(0, seq // slab, finish, 0, unroll=2)


def _deltanet(qkv, ab, abt3, ga, o_norm, batch, seq):
    nc = seq // CHUNK
    npk = seq // PACK_ROWS
    t = batch * seq
    hb = HEADS_PER_STEP
    hw = hb * LANES
    groups = A_HEADS // hb
    tail_blocks = seq // CONV_SHIFT
    scratch = [
        pltpu.VMEM((hb, seq, LANES), F32),
        pltpu.VMEM((hb, seq, LANES), F32),
        pltpu.VMEM((hb, seq, LANES), F32),
        pltpu.VMEM((hb, 2, nc, LANES + CHUNK, LANES), BF16),
        pltpu.VMEM((hb, 2, nc, LANES, LANES), F32),
        pltpu.VMEM((hb, 2, nc, 8, LANES), F32),
        pltpu.VMEM((hb, 2, seq, LANES), F32),
    ]
    return pl.pallas_call(
        _deltanet_kernel,
        grid=(batch, groups),
        in_specs=[
            pl.BlockSpec((seq, hw), lambda b, g: (b, g)),
            pl.BlockSpec((seq, hw), lambda b, g: (b, groups + g)),
            pl.BlockSpec((seq, hw), lambda b, g: (b, 2 * groups + g)),
            pl.BlockSpec((CONV_SHIFT, hw), lambda b, g: ((b + 1) * tail_blocks, g)),
            pl.BlockSpec((CONV_SHIFT, hw), lambda b, g: ((b + 1) * tail_blocks, groups + g)),
            pl.BlockSpec((CONV_SHIFT, hw), lambda b, g: ((b + 1) * tail_blocks, 2 * groups + g)),
            pl.BlockSpec((seq, LANES), lambda b, g: (b, 0)),
            pl.BlockSpec((npk, A_HEADS, PACK_ROWS), lambda b, g: (b, 0, 0)),
            pl.BlockSpec((npk, A_HEADS, PACK_ROWS), lambda b, g: (b, 1, 0)),
            pl.BlockSpec((seq, hw), lambda b, g: (b, g)),
            pl.BlockSpec((1, LANES), lambda b, g: (0, 0)),
        ],
        out_specs=pl.BlockSpec((seq, hw), lambda b, g: (b, g)),
        out_shape=jax.ShapeDtypeStruct((t, A_WIDTH), BF16),
        scratch_shapes=scratch,
        compiler_params=pltpu.CompilerParams(dimension_semantics=("arbitrary", "arbitrary"),
                                             vmem_limit_bytes=VMEM_LIMIT),
        name="deltanet",
    )(qkv, qkv, qkv, qkv, qkv, qkv, ab, abt3, abt3, ga, o_norm)


Q_BLOCKS = 4


def _attn_kernel(sink_ref, qb_ref, kb_ref, vb_ref, gb_ref, qc_ref, km_ref, vm_ref, gc_ref, out_ref):
    nb = pl.num_programs(1) * Q_BLOCKS
    w = WINDOW
    lane = lax.broadcasted_iota(jnp.int32, (w, LANES), 1)
    lower = lane < B_HEAD_DIM
    ri = lax.broadcasted_iota(jnp.int32, (2 * w, w), 0) % w
    ci = lax.broadcasted_iota(jnp.int32, (2 * w, w), 1)
    top = lax.broadcasted_iota(jnp.int32, (2 * w, 1), 0) < w

    first_blk = pl.program_id(1) * Q_BLOCKS - 1
    kv = []
    for i in range(Q_BLOCKS + 2):
        j = first_blk + i
        start = pl.multiple_of(jnp.clip(j, 0, nb - 1) * w, w)
        kblk = kb_ref[pl.ds(start, w), :].astype(F32)
        vblk = vb_ref[pl.ds(start, w), :].astype(F32)
        ksw = pltpu.roll(kblk, B_HEAD_DIM, axis=1)
        vsw = pltpu.roll(vblk, B_HEAD_DIM, axis=1)
        variants = {}
        for hk in range(B_KV_HEADS):
            for half in range(2):
                keep = lower if half == 0 else jnp.logical_not(lower)
                in_place = hk == half
                variants[hk, half] = (jnp.where(keep, kblk if in_place else ksw, 0.0).astype(BF16),
                                      jnp.where(keep, vblk if in_place else vsw, 0.0).astype(BF16))
        kv.append((variants, jnp.logical_and(j >= 0, j < nb)))

    for qi in range(Q_BLOCKS):
        rows = slice(qi * w, (qi + 1) * w)
        masks = (jnp.logical_and(ci >= ri, kv[qi][1]), None, jnp.logical_and(ci <= ri, kv[qi + 2][1]))
        for hk in range(B_KV_HEADS):
            c0 = hk * 2 * LANES
            q2 = jnp.concatenate([qb_ref[rows, c0:c0 + LANES], qb_ref[rows, c0 + LANES:c0 + 2 * LANES]], axis=0)
            acc = jnp.zeros((2 * w, LANES), F32)
            for half in range(2):
                sink_col = jnp.where(top, sink_ref[4 * hk + half], sink_ref[4 * hk + 2 + half])
                ss = []
                for rel in range(3):
                    s = _dot_nt(q2, kv[qi + rel][0][hk, half][0])
                    if masks[rel] is not None:
                        s = jnp.where(masks[rel], s, -jnp.inf)
                    ss.append(s)
                m = jnp.maximum(sink_col, jnp.max(jnp.maximum(jnp.maximum(ss[0], ss[1]), ss[2]),
                                                  axis=-1, keepdims=True))
                ps = [jnp.exp(s - m) for s in ss]
                den = jnp.exp(sink_col - m) + jnp.sum(ps[0] + ps[1] + ps[2], axis=-1, keepdims=True)
                pv = jnp.zeros((2 * w, LANES), F32)
                for rel in range(3):
                    pv = pv + _dot(ps[rel].astype(BF16), kv[qi + rel][0][hk, half][1])
                acc = acc + pv / den
            out_ref[rows, c0:c0 + LANES] = (acc[:w] * gb_ref[rows, c0:c0 + LANES].astype(F32)).astype(BF16)
            out_ref[rows, c0 + LANES:c0 + 2 * LANES] = (
                acc[w:] * gb_ref[rows, c0 + LANES:c0 + 2 * LANES].astype(F32)).astype(BF16)

    for hc in range(C_HEADS):
        c0 = hc * C_HEAD_DIM
        s = _dot_nt(qc_ref[:, c0:c0 + C_HEAD_DIM], km_ref[:, c0:c0 + C_HEAD_DIM])
        m = jnp.max(s, axis=-1, keepdims=True)
        p = jnp.exp(s - m)
        den = jnp.sum(p, axis=-1, keepdims=True)
        o = _dot(p.astype(BF16), vm_ref[:, c0:c0 + C_HEAD_DIM]) / den
        out_ref[:, B_WIDTH + c0:B_WIDTH + c0 + C_HEAD_DIM] = (
            o * gc_ref[:, c0:c0 + C_HEAD_DIM].astype(F32)).astype(BF16)


def _attention(sink, qb, kb, vb, gb, qc, km, vm, gcg, batch, seq):
    nb = seq // (WINDOW * Q_BLOCKS)
    qw = WINDOW * Q_BLOCKS
    t = batch * seq
    mlen = km.shape[0] // batch
    qrow = lambda b, n: (b * nb + n, 0)
    per_b = lambda b, n: (b, 0)
    return pl.pallas_call(
        _attn_kernel,
        grid=(batch, nb),
        in_specs=[
            pl.BlockSpec(memory_space=pltpu.SMEM),
            pl.BlockSpec((qw, B_WIDTH), qrow),
            pl.BlockSpec((seq, LANES), per_b),
            pl.BlockSpec((seq, LANES), per_b),
            pl.BlockSpec((qw, B_WIDTH), qrow),
            pl.BlockSpec((qw, C_WIDTH), qrow),
            pl.BlockSpec((mlen, C_WIDTH), per_b),
            pl.BlockSpec((mlen, C_WIDTH), per_b),
            pl.BlockSpec((qw, C_WIDTH), qrow),
        ],
        out_specs=pl.BlockSpec((qw, B_WIDTH + C_WIDTH), qrow),
        out_shape=jax.ShapeDtypeStruct((t, B_WIDTH + C_WIDTH), BF16),
        compiler_params=pltpu.CompilerParams(dimension_semantics=("arbitrary", "arbitrary"),
                                             vmem_limit_bytes=VMEM_LIMIT),
        name="attention",
    )(sink, qb, kb, vb, gb, qc, km, vm, gcg)


def _out_proj_kernel(x_ref, ma_ref, mbc_ref, wa_ref, wbc_ref, out_ref):
    out_ref[...] = x_ref[...] + _dot(ma_ref[...], wa_ref[...]) + _dot(mbc_ref[...], wbc_ref[...])


def _out_proj(x2, mixed_a, mixed_bc, w_out, tm=512):
    t = x2.shape[0]
    row = lambda i: (i, 0)
    const = lambda i: (0, 0)
    return pl.pallas_call(
        _out_proj_kernel,
        grid=(t // tm,),
        in_specs=[
            pl.BlockSpec((tm, D_MODEL), row),
            pl.BlockSpec((tm, A_WIDTH), row),
            pl.BlockSpec((tm, B_WIDTH + C_WIDTH), row),
            pl.BlockSpec((A_WIDTH, D_MODEL), const),
            pl.BlockSpec((B_WIDTH + C_WIDTH, D_MODEL), lambda i: (1, 0)),
        ],
        out_specs=pl.BlockSpec((tm, D_MODEL), row),
        out_shape=jax.ShapeDtypeStruct((t, D_MODEL), F32),
        compiler_params=pltpu.CompilerParams(dimension_semantics=("arbitrary",), vmem_limit_bytes=VMEM_LIMIT),
        name="out_proj",
    )(x2, mixed_a, mixed_bc, w_out, w_out)


def _layer(h, mem, norm_w, w_in_t, conv_w_a, a_log_fwd, a_log_bwd, dt_bias_fwd, dt_bias_bwd, o_norm_a,
           q_norm_b, k_norm_b, sink_b, mem_norm_w, w_mem_kv, q_norm_c, k_norm_c, w_out):
    batch, seq, _ = h.shape
    t = batch * seq
    x2 = h.reshape(t, D_MODEL)

    w_a, w_g, w_t = _w_prep(w_in_t)
    zeros = jnp.zeros((LANES - 2 * A_HEADS,), F32)
    alog_row = jnp.concatenate([a_log_fwd, a_log_bwd, zeros]).reshape(1, LANES)
    dtb_row = jnp.concatenate([dt_bias_fwd, dt_bias_bwd, zeros]).reshape(1, LANES)
    cos, sin = _rope_tables(seq)

    qkv, ga, ab, abt, qb, kb, vb, gb, qc, gcg = _in_proj(
        x2, norm_w.reshape(1, D_MODEL), w_a, w_g, w_t, conv_w_a, cos, sin, alog_row, dtb_row,
        (jnp.tile(q_norm_b, B_Q_HEADS) * B_HEAD_DIM ** -0.5).reshape(1, B_WIDTH),
        jnp.tile(k_norm_b, B_KV_HEADS).reshape(1, LANES),
        (jnp.tile(q_norm_c, C_HEADS) * C_HEAD_DIM ** -0.5).reshape(1, C_WIDTH), seq)

    mlen = mem.shape[1]
    km, vm = _mem_proj(mem.reshape(batch * mlen, D_MODEL), mem_norm_w.reshape(1, D_MODEL),
                       w_mem_kv.astype(BF16), jnp.tile(k_norm_c, C_HEADS).reshape(1, C_WIDTH))

    mixed_a = _deltanet(qkv, ab, abt, ga,
                        o_norm_a.reshape(1, A_HEAD_DIM), batch, seq)
    mixed_bc = _attention(sink_b, qb, kb, vb, gb, qc, km, vm, gcg, batch, seq)

    out = _out_proj(x2, mixed_a, mixed_bc, w_out.astype(BF16))
    return out.reshape(batch, seq, D_MODEL)


def kernel(x, mem, norm_w, w_in, conv_w_a, a_log_fwd, a_log_bwd, dt_bias_fwd, dt_bias_bwd, o_norm_a,
           q_norm_b, k_norm_b, sink_b, mem_norm_w, w_mem_kv, q_norm_c, k_norm_c, w_out):
    h = x
    w_in_t = jnp.swapaxes(w_in, 1, 2)
    conv_taps = jnp.swapaxes(conv_w_a, 0, 1)
    for l in range(norm_w.shape[0]):
        h = _layer(h, mem, norm_w[l], w_in_t[l], conv_taps[:, l:l + 1], a_log_fwd[l], a_log_bwd[l], dt_bias_fwd[l],
                   dt_bias_bwd[l], o_norm_a[l], q_norm_b[l], k_norm_b[l], sink_b[l], mem_norm_w[l],
                   w_mem_kv[l], q_norm_c[l], k_norm_c[l], w_out[l])
    return h
```
